```python
import jax, jax.numpy as jnp
from jax import lax
import numpy as np

D_MODEL = 1024
BATCH = 8
SEQ = 8192
DEPTH = 2

CHUNK = 64
NORM_EPS = 1e-6
HG_HEADS = 4
HG_DK = 128
HG_DV = 128
HG_WIDTH = HG_HEADS * HG_DV
GLA_HEADS = 4
GLA_DK = 64
GLA_DV = 128
GLA_RANK = 16
GLA_GATE_NORMALIZER = 16.0
GLA_WIDTH = GLA_HEADS * GLA_DV
DN_HEADS = 4
DN_DK = 128
DN_DV = 128
DN_CONV = 4
DN_WIDTH = DN_HEADS * DN_DV
BRANCH_WIDTH = 512
N_BRANCH = 3
FFN_HIDDEN = 2816
FFN_CONV = 3
IN_SPLITS = (
    HG_HEADS * HG_DK,
    HG_HEADS * HG_DK,
    HG_WIDTH,
    HG_WIDTH,
    GLA_HEADS * GLA_DK,
    GLA_HEADS * GLA_DK,
    GLA_WIDTH,
    GLA_RANK,
    GLA_WIDTH,
    DN_HEADS * (2 * DN_DK + DN_DV),
    DN_WIDTH,
    DN_HEADS,
    DN_HEADS,
    N_BRANCH * D_MODEL,
)
IN_WIDTH = 2048 + 1552 + 2056 + N_BRANCH * D_MODEL

kernel_name = 'hybrid_hgrn2_gla_gdn_convffn'


def rms_norm(x, g):
    x32 = x.astype(jnp.float32)
    y = x32 * lax.rsqrt(jnp.mean(x32 * x32, axis=-1, keepdims=True) + NORM_EPS)
    return (y * g.astype(jnp.float32)).astype(x.dtype)


def l2_norm(x):
    x32 = x.astype(jnp.float32)
    return x32 * lax.rsqrt(jnp.sum(x32 * x32, axis=-1, keepdims=True) + NORM_EPS)


def gated_head_norm(o, z, g):
    B, S, H, dv = o.shape
    return (rms_norm(o, g) * jax.nn.silu(z.reshape(B, S, H, dv))).reshape(B, S, H * dv)


def causal_dwconv(x, w):
    W = w.shape[0]
    S = x.shape[1]
    xp = jnp.pad(x, ((0, 0), (W - 1, 0), (0, 0)))
    return sum(xp[:, j:j + S, :] * w[j] for j in range(W))


def to_chunks(t):
    B, S, H, d = t.shape
    return t.reshape(B, S // CHUNK, CHUNK, H, d).transpose(0, 3, 1, 2, 4)


def from_chunks(t):
    B, H, N, C, d = t.shape
    return t.transpose(0, 2, 3, 1, 4).reshape(B, N * C, H, d)


def chunk_gla(q, k, v, log_f, scale):
    out_dtype = v.dtype
    f32 = jnp.float32
    qc, kc, vc, gc = (jnp.moveaxis(to_chunks(t.astype(f32)), 2, 0)
                      for t in (q.astype(f32) * scale, k, v, log_f))
    B, H, dk, dv = qc.shape[1], qc.shape[2], qc.shape[-1], vc.shape[-1]
    causal = jnp.tril(jnp.ones((CHUNK, CHUNK), dtype=bool))[:, :, None]

    def step(state, inp):
        q_, k_, v_, g_ = inp
        b = jnp.cumsum(g_, axis=-2)
        decay = jnp.exp(jnp.where(causal, b[..., :, None, :] - b[..., None, :, :], -jnp.inf))
        scores = jnp.einsum('bhtsd,bhsd->bhts', q_[..., :, None, :] * decay, k_)
        o = (jnp.einsum('bhtd,bhdv->bhtv', q_ * jnp.exp(b), state)
             + jnp.einsum('bhts,bhsv->bhtv', scores, v_))
        b_last = b[..., -1:, :]
        state = (state * jnp.exp(b_last)[..., 0, :, None]
                 + jnp.einsum('bhsd,bhsv->bhdv', k_ * jnp.exp(b_last - b), v_))
        return state, o

    state0 = jnp.zeros((B, H, dk, dv), f32)
    _, o = lax.scan(step, state0, (qc, kc, vc, gc))
    return from_chunks(jnp.moveaxis(o, 0, 2)).astype(out_dtype)


def chunk_gated_delta(q, k, v, g, beta):
    out_dtype = v.dtype
    f32 = jnp.float32
    q, k, v = (to_chunks(t.astype(f32)) for t in (q, k, v))
    g = to_chunks(g.astype(f32)[..., None])[..., 0]
    beta = to_chunks(beta.astype(f32)[..., None])[..., 0]
    G = jnp.cumsum(g, axis=-1)
    incl = jnp.tril(jnp.ones((CHUNK, CHUNK), dtype=bool))
    strict = jnp.tril(jnp.ones((CHUNK, CHUNK), dtype=bool), -1)
    L = jnp.exp(jnp.where(incl, G[..., :, None] - G[..., None, :], -jnp.inf))
    kb = k * beta[..., None]
    M = jnp.where(strict, jnp.einsum('bhntd,bhnsd->bhnts', kb, k) * L, 0.0)
    eye = jnp.eye(CHUNK, dtype=f32)
    rhs = jnp.concatenate([v * beta[..., None], kb * jnp.exp(G)[..., None]], axis=-1)
    sol = lax.linalg.triangular_solve(M + eye, rhs, left_side=True, lower=True,
                                      unit_diagonal=True)
    dv = v.shape[-1]
    u, w = sol[..., :dv], sol[..., dv:]
    scores = jnp.where(incl, jnp.einsum('bhntd,bhnsd->bhnts', q, k) * L, 0.0)
    q_dec = q * jnp.exp(G)[..., None]
    k_dec = k * jnp.exp(G[..., -1:] - G)[..., None]
    chunk_decay = jnp.exp(G[..., -1])
    xs = tuple(jnp.moveaxis(t, 2, 0) for t in (u, w, q_dec, scores, k_dec, chunk_decay))
    B, H, dk = q.shape[0], q.shape[1], q.shape[-1]

    def step(state, inp):
        u_, w_, qd_, sc_, kd_, cd_ = inp
        v_new = u_ - jnp.einsum('bhcd,bhdv->bhcv', w_, state)
        o = (jnp.einsum('bhcd,bhdv->bhcv', qd_, state)
             + jnp.einsum('bhts,bhsv->bhtv', sc_, v_new))
        state = state * cd_[..., None, None] + jnp.einsum('bhcd,bhcv->bhdv', kd_, v_new)
        return state, o

    state0 = jnp.zeros((B, H, dk, dv), f32)
    _, o = lax.scan(step, state0, xs)
    return from_chunks(jnp.moveaxis(o, 0, 2)).astype(out_dtype)


def token_mixer(h, w_in, lb, hg_norm_g, gla_gk_w2, gla_gk_b, gla_norm_g, dn_conv_w,
                dn_A_log, dn_dt_bias, dn_norm_g, w_branch, gate_b, w_out):
    B, S, _ = h.shape
    f32 = jnp.float32
    proj = h @ w_in
    idx = [int(i) for i in np.cumsum(IN_SPLITS)[:-1]]
    (hg_q, hg_f, hg_i, hg_g, gla_q, gla_k, gla_v, gla_gk, gla_g,
     dn_qkv, dn_z, dn_b, dn_a, gates) = jnp.split(proj, idx, axis=-1)

    lb = lb.reshape(HG_HEADS, HG_DK)
    zf = hg_f.reshape(B, S, HG_HEADS, HG_DK).astype(f32)
    log_f = jnp.logaddexp(jnp.log(lb), jnp.log1p(-lb) + jax.nn.log_sigmoid(zf))
    k_hg = (1.0 - lb) * jax.nn.sigmoid(-zf)
    q_hg = jax.nn.silu(hg_q).reshape(B, S, HG_HEADS, HG_DK)
    o = chunk_gla(q_hg, k_hg, hg_i.reshape(B, S, HG_HEADS, HG_DV), log_f, HG_DK ** -0.5)
    o_hg = gated_head_norm(o, hg_g, hg_norm_g)

    gk = (gla_gk @ gla_gk_w2 + gla_gk_b).astype(f32)
    log_a = jax.nn.log_sigmoid(gk).reshape(B, S, GLA_HEADS, GLA_DK) / GLA_GATE_NORMALIZER
    o = chunk_gla(gla_q.reshape(B, S, GLA_HEADS, GLA_DK), gla_k.reshape(B, S, GLA_HEADS, GLA_DK),
                  gla_v.reshape(B, S, GLA_HEADS, GLA_DV), log_a, GLA_DK ** -0.5)
    o_gla = gated_head_norm(o, gla_g, gla_norm_g)

    qkv = jax.nn.silu(causal_dwconv(dn_qkv, dn_conv_w))
    q_dn, k_dn, v_dn = jnp.split(qkv, [DN_HEADS * DN_DK, 2 * DN_HEADS * DN_DK], axis=-1)
    q_dn = l2_norm(q_dn.reshape(B, S, DN_HEADS, DN_DK)) * (DN_DK ** -0.5)
    k_dn = l2_norm(k_dn.reshape(B, S, DN_HEADS, DN_DK))
    beta = jax.nn.sigmoid(dn_b.astype(f32))
    g_dn = -jnp.exp(dn_A_log.astype(f32)) * jax.nn.softplus(dn_a.astype(f32) + dn_dt_bias.astype(f32))
    o = chunk_gated_delta(q_dn, k_dn, v_dn.reshape(B, S, DN_HEADS, DN_DV), g_dn, beta)
    o_dn = gated_head_norm(o, dn_z, dn_norm_g)

    gates = jax.nn.sigmoid(gates + gate_b).reshape(B, S, N_BRANCH, D_MODEL)
    merged = sum(gates[:, :, n, :] * (o_n @ w_branch[n])
                 for n, o_n in enumerate((o_hg, o_gla, o_dn)))
    return merged @ w_out


def conv_ffn(h, w_up, conv_w, conv_b, w_down):
    u = causal_dwconv(h @ w_up, conv_w) + conv_b
    a, b = jnp.split(u, 2, axis=-1)
    return (jax.nn.silu(a) * b) @ w_down


def setup_inputs(seed: int = 0) -> dict:
    key = jax.random.key(seed)
    ks = jax.random.split(key, 24)
    f32 = jnp.float32

    def nrm(k, shape, scale):
        return jax.random.normal(k, shape, f32) * scale

    dt = jnp.exp(jax.random.uniform(ks[10], (DEPTH, DN_HEADS), f32,
                                    jnp.log(0.001), jnp.log(0.1)))
    return {
        'x': nrm(ks[0], (BATCH, SEQ, D_MODEL), 1.0),
        'norm_mix_g': 1.0 + nrm(ks[1], (DEPTH, D_MODEL), 0.02),
        'w_in': nrm(ks[2], (DEPTH, D_MODEL, IN_WIDTH), D_MODEL ** -0.5),
        'hg_lower_bounds': nrm(ks[3], (DEPTH, HG_HEADS * HG_DK), 0.1),
        'hg_norm_g': 1.0 + nrm(ks[4], (DEPTH, HG_DV), 0.02),
        'gla_gk_w2': nrm(ks[5], (DEPTH, GLA_RANK, GLA_HEADS * GLA_DK), GLA_RANK ** -0.5),
        'gla_gk_b': nrm(ks[6], (DEPTH, GLA_HEADS * GLA_DK), 0.01),
        'gla_norm_g': 1.0 + nrm(ks[7], (DEPTH, GLA_DV), 0.02),
        'dn_conv_w': nrm(ks[8], (DEPTH, DN_CONV, DN_HEADS * (2 * DN_DK + DN_DV)), DN_CONV ** -0.5),
        'dn_A_log': jnp.log(jax.random.uniform(ks[9], (DEPTH, DN_HEADS), f32, 1.0, 16.0)),
        'dn_dt_bias': dt + jnp.log(-jnp.expm1(-dt)),
        'dn_norm_g': 1.0 + nrm(ks[11], (DEPTH, DN_DV), 0.02),
        'w_branch': nrm(ks[12], (DEPTH, N_BRANCH, BRANCH_WIDTH, D_MODEL), BRANCH_WIDTH ** -0.5),
        'gate_b': nrm(ks[13], (DEPTH, N_BRANCH * D_MODEL), 0.01),
        'w_out': nrm(ks[14], (DEPTH, D_MODEL, D_MODEL), D_MODEL ** -0.5),
        'norm_ffn_g': 1.0 + nrm(ks[15], (DEPTH, D_MODEL), 0.02),
        'w_up': nrm(ks[16], (DEPTH, D_MODEL, 2 * FFN_HIDDEN), D_MODEL ** -0.5),
        'ffn_conv_w': nrm(ks[17], (DEPTH, FFN_CONV, 2 * FFN_HIDDEN), FFN_CONV ** -0.5),
        'ffn_conv_b': nrm(ks[18], (DEPTH, 2 * FFN_HIDDEN), 0.01),
        'w_down': nrm(ks[19], (DEPTH, FFN_HIDDEN, D_MODEL), FFN_HIDDEN ** -0.5),
        'norm_final_g': 1.0 + nrm(ks[20], (D_MODEL,), 0.02),
    }


def reference(x, norm_mix_g, w_in, hg_lower_bounds, hg_norm_g, gla_gk_w2, gla_gk_b,
              gla_norm_g, dn_conv_w, dn_A_log, dn_dt_bias, dn_norm_g, w_branch, gate_b,
              w_out, norm_ffn_g, w_up, ffn_conv_w, ffn_conv_b, w_down, norm_final_g):
    lb_all = jnp.cumsum(jax.nn.softmax(hg_lower_bounds.astype(jnp.float32), axis=0), axis=0)
    lb_all = lb_all - lb_all[:1]
    for l in range(DEPTH):
        h = rms_norm(x, norm_mix_g[l])
        x = x + token_mixer(h, w_in[l], lb_all[l], hg_norm_g[l], gla_gk_w2[l], gla_gk_b[l],
                            gla_norm_g[l], dn_conv_w[l], dn_A_log[l], dn_dt_bias[l],
                            dn_norm_g[l], w_branch[l], gate_b[l], w_out[l])
        h = rms_norm(x, norm_ffn_g[l])
        x = x + conv_ffn(h, w_up[l], ffn_conv_w[l], ffn_conv_b[l], w_down[l])
    return rms_norm(x, norm_final_g)
```

```python
import functools

import jax
import jax.numpy as jnp
import numpy as np
from jax import lax
from jax.experimental import pallas as pl
from jax.experimental.pallas import tpu as pltpu

F32 = jnp.float32
BF16 = jnp.bfloat16

D_MODEL = 1024
CHUNK = 64
NORM_EPS = 1e-6
N_HEADS = 4
HEAD_DIM = 128
BRANCH_WIDTH = N_HEADS * HEAD_DIM
GLA_DK = 64
GLA_RANK = 16
GLA_GATE_NORMALIZER = 16.0
DN_CONV = 4
N_BRANCH = 3
FFN_HIDDEN = 2816
FFN_CONV = 3
LANES = 128
CONV_PAD = 8
N_LEVELS = 6

TILE_MIX = 256
TILE_MERGE = 512
TILE_FFN = 256
VMEM_LIMIT = 56 * 1024 * 1024


def _sigmoid(x):
    return 1.0 / (1.0 + jnp.exp(-x))


def _silu(x):
    return x * _sigmoid(x)


def _softplus(x):
    return jnp.maximum(x, 0.0) + jnp.log1p(jnp.exp(-jnp.abs(x)))


def _log_sigmoid(x):
    return -_softplus(-x)


def _rms_norm(x, g):
    y = x * lax.rsqrt(jnp.mean(x * x, axis=-1, keepdims=True) + NORM_EPS)
    return y * g


def _dot(a, b):
    return jnp.dot(a, b, preferred_element_type=F32)


def _dot_nt(a, b):
    return lax.dot_general(a, b, (((1,), (1,)), ((), ())), preferred_element_type=F32)


def _dot_tn(a, b):
    return lax.dot_general(a, b, (((0,), (0,)), ((), ())), preferred_element_type=F32)


def _split_bf16(x):
    hi = x.astype(BF16)
    lo = (x - hi.astype(F32)).astype(BF16)
    return hi, lo


def _head(h):
    return slice(h * HEAD_DIM, (h + 1) * HEAD_DIM)


def _window_table():
    t = np.arange(CHUNK)
    u = np.arange(CHUNK)
    rows = []
    for lvl in range(N_LEVELS):
        p = CHUNK >> (lvl + 1)
        mid = (t // (2 * p)) * (2 * p) + p - 1
        upper = t > mid
        win = np.where(upper[:, None], (u[None, :] > mid[:, None]) & (u[None, :] <= t[:, None]),
                       (u[None, :] > t[:, None]) & (u[None, :] <= mid[:, None]))
        rows.append(win)
    rows.append(u[None, :] <= t[:, None])
    rows.append(u[None, :] > t[:, None])
    return np.concatenate(rows, axis=0).astype(np.float32)


def _level_masks():
    t = np.arange(CHUNK)[:, None]
    s = np.arange(CHUNK)[None, :]
    out = []
    for lvl in range(N_LEVELS):
        p = CHUNK >> (lvl + 1)
        out.append((t // (2 * p) == s // (2 * p)) & (t % (2 * p) >= p) & (s % (2 * p) < p))
    out.append(t == s)
    return np.stack(out).astype(np.float32)


def _gla_chunks(n_chunks, q_s, k_s, v_s, g_s, o_s, st_ref, win_ref, lmask_ref):
    win = win_ref[...]

    def chunk(c, carry):
        r0 = pl.multiple_of(c * CHUNK, CHUNK)
        rows = pl.ds(r0, CHUNK)
        q = q_s[rows, :]
        k = k_s[rows, :]
        v_bf = v_s[rows, :].astype(BF16)
        g_hi, g_lo = _split_bf16(g_s[rows, :])
        e = jnp.exp(_dot(win, g_hi) + _dot(win, g_lo))
        q_bf = q.astype(BF16)
        k_bf = k.astype(BF16)
        q_lv = [(q * e[l * CHUNK:(l + 1) * CHUNK]).astype(BF16) for l in range(N_LEVELS)]
        k_lv = [(k * e[l * CHUNK:(l + 1) * CHUNK]).astype(BF16) for l in range(N_LEVELS)]
        e_b = e[N_LEVELS * CHUNK:(N_LEVELS + 1) * CHUNK]
        e_rev = e[(N_LEVELS + 1) * CHUNK:(N_LEVELS + 2) * CHUNK]
        e_last = e_b[CHUNK - 1:CHUNK, :]
        q_in = (q * e_b).astype(BF16)
        k_out = (k * e_rev).astype(BF16)
        for h in range(N_HEADS):
            sl = _head(h)
            sc = _dot_nt(q_bf[:, sl], k_bf[:, sl]) * lmask_ref[N_LEVELS]
            for l in range(N_LEVELS):
                sc = sc + _dot_nt(q_lv[l][:, sl], k_lv[l][:, sl]) * lmask_ref[l]
            st = st_ref[h]
            o = _dot_nt(q_in[:, sl], st.astype(BF16)) + _dot(sc.astype(BF16), v_bf[:, sl])
            o_s[rows, sl] = o
            st_ref[h] = st * e_last[:, sl] + _dot_tn(v_bf[:, sl], k_out[:, sl])
        return carry

    lax.fori_loop(0, n_chunks, chunk, 0)


def _gated_head_norm(o, z, g):
    outs = []
    for h in range(N_HEADS):
        oh = o[:, _head(h)]
        inv = lax.rsqrt(jnp.mean(oh * oh, axis=-1, keepdims=True) + NORM_EPS)
        outs.append(oh * inv)
    return jnp.concatenate(outs, axis=-1) * g * _silu(z)


def _reset_state(st_ref):
    @pl.when(pl.program_id(1) == 0)
    def _():
        st_ref[...] = jnp.zeros(st_ref.shape, st_ref.dtype)


def _hgrn_kernel(x_ref, ng_ref, w_ref, lb_ref, hn_ref, win_ref, lmask_ref, o_ref,
                 q_s, k_s, v_s, g_s, o_s, st_ref):
    _reset_state(st_ref)
    w = BRANCH_WIDTH
    h = _rms_norm(x_ref[...], ng_ref[...]).astype(BF16)
    p = _dot(h, w_ref[...])
    z = p[:, w:2 * w]
    lb = lb_ref[0:1, :]
    log_lb = lb_ref[1:2, :]
    log1m_lb = lb_ref[2:3, :]
    c = log1m_lb + _log_sigmoid(z)
    g_s[...] = jnp.maximum(log_lb, c) + jnp.log1p(jnp.exp(-jnp.abs(log_lb - c)))
    k_s[...] = (1.0 - lb) * _sigmoid(-z)
    q_s[...] = _silu(p[:, 0:w]) * (HEAD_DIM ** -0.5)
    v_s[...] = p[:, 2 * w:3 * w]
    _gla_chunks(x_ref.shape[0] // CHUNK, q_s, k_s, v_s, g_s, o_s, st_ref, win_ref, lmask_ref)
    o_ref[...] = _gated_head_norm(o_s[...], p[:, 3 * w:4 * w], hn_ref[...]).astype(o_ref.dtype)


def _gla_kernel(x_ref, ng_ref, w_ref, w2_ref, b2_ref, hn_ref, win_ref, lmask_ref, o_ref,
                q_s, k_s, v_s, g_s, o_s, st_ref):
    _reset_state(st_ref)
    w = BRANCH_WIDTH
    h = _rms_norm(x_ref[...], ng_ref[...]).astype(BF16)
    p = _dot(h, w_ref[...])
    gk = _dot(p[:, 4 * w:4 * w + LANES].astype(BF16), w2_ref[...]) + b2_ref[...]
    g_s[...] = _log_sigmoid(gk) * (1.0 / GLA_GATE_NORMALIZER)
    q_s[...] = p[:, 0:w] * (GLA_DK ** -0.5)
    k_s[...] = p[:, w:2 * w]
    v_s[...] = p[:, 2 * w:3 * w]
    _gla_chunks(x_ref.shape[0] // CHUNK, q_s, k_s, v_s, g_s, o_s, st_ref, win_ref, lmask_ref)
    o_ref[...] = _gated_head_norm(o_s[...], p[:, 3 * w:4 * w], hn_ref[...]).astype(o_ref.dtype)


def _unit_lower_inverse(m):
    row = lax.broadcasted_iota(jnp.int32, (CHUNK, CHUNK), 0)
    col = lax.broadcasted_iota(jnp.int32, (CHUNK, CHUNK), 1)
    eye = (row == col).astype(F32)
    t = eye
    b = 1
    while b < CHUNK:
        off = jnp.where((row // (2 * b) == col // (2 * b)) & (row % (2 * b) >= b) & (col % (2 * b) < b), m, 0.0)
        if b == 1:
            t = eye - off
        else:
            t_bf = t.astype(BF16)
            t = t - _dot(_dot(t_bf, off.astype(BF16)).astype(BF16), t_bf)
        b *= 2
    return t


def _dn_kernel(x_ref, ng_ref, w_ref, cw_ref, ab_ref, hn_ref, tri_ref, o_ref,
               ext_s, q_s, k_s, v_s, g_s, bt_s, o_s, st_ref, hist_s):
    _reset_state(st_ref)

    @pl.when(pl.program_id(1) == 0)
    def _():
        hist_s[...] = jnp.zeros(hist_s.shape, hist_s.dtype)

    tile = x_ref.shape[0]
    w = BRANCH_WIDTH
    h = _rms_norm(x_ref[...], ng_ref[...]).astype(BF16)
    p = _dot(h, w_ref[...])
    ext_s[0:CONV_PAD, :] = hist_s[...]
    ext_s[CONV_PAD:CONV_PAD + tile, :] = p[:, 0:3 * w]
    hist_s[...] = ext_s[tile:tile + CONV_PAD, :]
    conv = cw_ref[0:1, :] * ext_s[CONV_PAD - 3:CONV_PAD - 3 + tile, :]
    for j in range(1, DN_CONV):
        conv = conv + cw_ref[j:j + 1, :] * ext_s[CONV_PAD - 3 + j:CONV_PAD - 3 + j + tile, :]
    qkv = _silu(conv)
    qn, kn = [], []
    for hd in range(N_HEADS):
        qh = qkv[:, _head(hd)]
        kh = qkv[:, w + hd * HEAD_DIM:w + (hd + 1) * HEAD_DIM]
        qn.append(qh * lax.rsqrt(jnp.sum(qh * qh, axis=-1, keepdims=True) + NORM_EPS) * (HEAD_DIM ** -0.5))
        kn.append(kh * lax.rsqrt(jnp.sum(kh * kh, axis=-1, keepdims=True) + NORM_EPS))
    q_s[...] = jnp.concatenate(qn, axis=-1)
    k_s[...] = jnp.concatenate(kn, axis=-1)
    v_s[...] = qkv[:, 2 * w:3 * w]
    ba = p[:, 4 * w:4 * w + LANES]
    bt_s[...] = _sigmoid(ba)
    g_s[...] = -jnp.exp(ab_ref[0:1, :]) * _softplus(ba + ab_ref[1:2, :])

    row = lax.broadcasted_iota(jnp.int32, (CHUNK, CHUNK), 0)
    col = lax.broadcasted_iota(jnp.int32, (CHUNK, CHUNK), 1)
    incl = row >= col
    strict = row > col
    tri = tri_ref[...]

    def chunk(c, carry):
        r0 = pl.multiple_of(c * CHUNK, CHUNK)
        rows = pl.ds(r0, CHUNK)
        q = q_s[rows, :]
        k = k_s[rows, :]
        v = v_s[rows, :]
        beta = bt_s[rows, :]
        g_hi, g_lo = _split_bf16(g_s[rows, :])
        gcum = _dot(tri, g_hi) + _dot(tri, g_lo)
        gcum_t = gcum.T
        g_last = gcum[CHUNK - 1:CHUNK, :]
        e_in = jnp.exp(gcum)
        e_out = jnp.exp(g_last - gcum)
        e_last = jnp.exp(g_last)
        for hd in range(N_HEADS):
            sl = _head(hd)
            ln = N_HEADS + hd
            gc = gcum[:, ln:ln + 1]
            decay = jnp.exp(jnp.where(incl, gc - gcum_t[ln:ln + 1, :], -jnp.inf))
            bc = beta[:, hd:hd + 1]
            kh = k[:, sl]
            kb = kh * bc
            kh_bf = kh.astype(BF16)
            m = jnp.where(strict, _dot_nt(kb.astype(BF16), kh_bf) * decay, 0.0)
            t_inv = _unit_lower_inverse(m).astype(BF16)
            rhs = jnp.concatenate([v[:, sl] * bc, kb * e_in[:, ln:ln + 1]], axis=-1).astype(BF16)
            sol = _dot(t_inv, rhs)
            u = sol[:, 0:HEAD_DIM]
            wmat = sol[:, HEAD_DIM:2 * HEAD_DIM]
            sc = jnp.where(incl, _dot_nt(q[:, sl].astype(BF16), kh_bf) * decay, 0.0)
            q_in = (q[:, sl] * e_in[:, ln:ln + 1]).astype(BF16)
            k_out = (kh * e_out[:, ln:ln + 1]).astype(BF16)
            st = st_ref[hd]
            st_bf = st.astype(BF16)
            v_new = u - _dot(wmat.astype(BF16), st_bf)
            v_new_bf = v_new.astype(BF16)
            o_s[rows, sl] = _dot(q_in, st_bf) + _dot(sc.astype(BF16), v_new_bf)
            st_ref[hd] = st * e_last[:, ln:ln + 1] + _dot_tn(k_out, v_new_bf)
        return carry

    lax.fori_loop(0, tile // CHUNK, chunk, 0)
    o_ref[...] = _gated_head_norm(o_s[...], p[:, 3 * w:4 * w], hn_ref[...]).astype(o_ref.dtype)


def _merge_kernel(x_ref, ng_ref, wg_ref, gb_ref, oa_ref, ob_ref, oc_ref, wb_ref, wo_ref, y_ref):
    x = x_ref[...]
    h = _rms_norm(x, ng_ref[...]).astype(BF16)
    gates = _sigmoid(_dot(h, wg_ref[...]) + gb_ref[...])
    merged = None
    for n, o_n in enumerate((oa_ref, ob_ref, oc_ref)):
        term = gates[:, n * D_MODEL:(n + 1) * D_MODEL] * _dot(o_n[...], wb_ref[n])
        merged = term if merged is None else merged + term
    y_ref[...] = x + _dot(merged.astype(BF16), wo_ref[...])


def _ffn_kernel(x_ref, ng_ref, wu_ref, cw_ref, cb_ref, wd_ref, fg_ref, y_ref, ext_s, hist_s, act_s,
                *, final_norm):
    @pl.when(pl.program_id(1) == 0)
    def _():
        hist_s[...] = jnp.zeros(hist_s.shape, hist_s.dtype)

    tile = x_ref.shape[0]
    x = x_ref[...]
    h = _rms_norm(x, ng_ref[...]).astype(BF16)
    ext_s[0:CONV_PAD, :] = hist_s[...]
    ext_s[CONV_PAD:CONV_PAD + tile, :] = _dot(h, wu_ref[...])
    hist_s[...] = ext_s[tile:tile + CONV_PAD, :]
    blk = 2 * LANES
    for j0 in range(0, FFN_HIDDEN, blk):
        halves = []
        for base in (j0, FFN_HIDDEN + j0):
            cols = slice(base, base + blk)
            u = cb_ref[0:1, cols] + cw_ref[0:1, cols] * ext_s[CONV_PAD - 2:CONV_PAD - 2 + tile, cols]
            for j in range(1, FFN_CONV):
                u = u + cw_ref[j:j + 1, cols] * ext_s[CONV_PAD - 2 + j:CONV_PAD - 2 + j + tile, cols]
            halves.append(u)
        act_s[:, j0:j0 + blk] = (_silu(halves[0]) * halves[1]).astype(BF16)
    y = x + _dot(act_s[...], wd_ref[...])
    if final_norm:
        y = _rms_norm(y, fg_ref[...])
    y_ref[...] = y


def _const_spec(shape):
    nd = len(shape)
    return pl.BlockSpec(shape, lambda b, i, _nd=nd: (0,) * _nd, pipeline_mode=pl.Buffered(1))


def _row_spec(tile, width, steps):
    return pl.BlockSpec((tile, width), lambda b, i, _s=steps: (b * _s + i, 0))


def _params():
    return pltpu.CompilerParams(dimension_semantics=("arbitrary", "arbitrary"), vmem_limit_bytes=VMEM_LIMIT)


def _mixer_call(body, name, x2, batch, consts, scratch):
    rows = x2.shape[0]
    steps = rows // batch // TILE_MIX
    in_specs = [_row_spec(TILE_MIX, D_MODEL, steps)] + [_const_spec(c.shape) for c in consts]
    return pl.pallas_call(
        body,
        grid=(batch, steps),
        in_specs=in_specs,
        out_specs=_row_spec(TILE_MIX, BRANCH_WIDTH, steps),
        out_shape=jax.ShapeDtypeStruct((rows, BRANCH_WIDTH), BF16),
        scratch_shapes=scratch,
        compiler_params=_params(),
        name=name,
    )(x2, *consts)


def _gla_scratch():
    return [pltpu.VMEM((TILE_MIX, BRANCH_WIDTH), F32) for _ in range(5)] + [
        pltpu.VMEM((N_HEADS, HEAD_DIM, HEAD_DIM), F32)]


def _dn_scratch():
    return ([pltpu.VMEM((TILE_MIX + CONV_PAD, 3 * BRANCH_WIDTH), F32)]
            + [pltpu.VMEM((TILE_MIX, BRANCH_WIDTH), F32) for _ in range(3)]
            + [pltpu.VMEM((TILE_MIX, LANES), F32) for _ in range(2)]
            + [pltpu.VMEM((TILE_MIX, BRANCH_WIDTH), F32),
               pltpu.VMEM((N_HEADS, HEAD_DIM, HEAD_DIM), F32),
               pltpu.VMEM((CONV_PAD, 3 * BRANCH_WIDTH), F32)])


def _merge_call(x2, batch, consts_a, branches, consts_b):
    rows = x2.shape[0]
    steps = rows // batch // TILE_MERGE
    in_specs = ([_row_spec(TILE_MERGE, D_MODEL, steps)] + [_const_spec(c.shape) for c in consts_a]
                + [_row_spec(TILE_MERGE, BRANCH_WIDTH, steps) for _ in branches]
                + [_const_spec(c.shape) for c in consts_b])
    return pl.pallas_call(
        _merge_kernel,
        grid=(batch, steps),
        in_specs=in_specs,
        out_specs=_row_spec(TILE_MERGE, D_MODEL, steps),
        out_shape=jax.ShapeDtypeStruct((rows, D_MODEL), F32),
        compiler_params=_params(),
        name="merge",
    )(x2, *consts_a, *branches, *consts_b)


def _ffn_call(x2, batch, consts, final_norm):
    rows = x2.shape[0]
    steps = rows // batch // TILE_FFN
    in_specs = [_row_spec(TILE_FFN, D_MODEL, steps)] + [_const_spec(c.shape) for c in consts]
    return pl.pallas_call(
        functools.partial(_ffn_kernel, final_norm=final_norm),
        grid=(batch, steps),
        in_specs=in_specs,
        out_specs=_row_spec(TILE_FFN, D_MODEL, steps),
        out_shape=jax.ShapeDtypeStruct((rows, D_MODEL), F32),
        scratch_shapes=[pltpu.VMEM((TILE_FFN + CONV_PAD, 2 * FFN_HIDDEN), F32),
                        pltpu.VMEM((CONV_PAD, 2 * FFN_HIDDEN), F32),
                        pltpu.VMEM((TILE_FFN, FFN_HIDDEN), BF16)],
        compiler_params=_params(),
        name="ffn_final" if final_norm else "ffn",
    )(x2, *consts)


def _pad_heads(wcols, dk):
    lead = wcols.shape[:-1]
    w4 = wcols.reshape(lead + (N_HEADS, dk))
    w4 = jnp.pad(w4, [(0, 0)] * len(lead) + [(0, 0), (0, HEAD_DIM - dk)])
    return w4.reshape(lead + (N_HEADS * HEAD_DIM,))


def _pad_cols(wcols, width):
    return jnp.pad(wcols, [(0, 0)] * (wcols.ndim - 1) + [(0, width - wcols.shape[-1])])


def _row(v):
    return v.reshape(1, -1).astype(F32)


def kernel(x, norm_mix_g, w_in, hg_lower_bounds, hg_norm_g, gla_gk_w2, gla_gk_b, gla_norm_g, dn_conv_w,
           dn_A_log, dn_dt_bias, dn_norm_g, w_branch, gate_b, w_out, norm_ffn_g, w_up, ffn_conv_w,
           ffn_conv_b, w_down, norm_final_g):
    batch, seq, d = x.shape
    depth = w_in.shape[0]
    assert d == D_MODEL and seq % max(TILE_MIX, TILE_MERGE, TILE_FFN) == 0
    x2 = x.reshape(batch * seq, d)

    lb_all = jnp.cumsum(jax.nn.softmax(hg_lower_bounds.astype(F32), axis=0), axis=0)
    lb_all = lb_all - lb_all[:1]
    win = jnp.asarray(_window_table(), BF16)
    lmask = jnp.asarray(_level_masks(), F32)
    tri = jnp.asarray(np.tril(np.ones((CHUNK, CHUNK), np.float32)), BF16)

    hw = BRANCH_WIDTH
    c0 = 4 * hw
    c1 = c0 + 2 * N_HEADS * GLA_DK
    c2 = c1 + hw
    c3 = c2 + GLA_RANK
    c4 = c3 + hw
    c5 = c4 + 3 * hw
    c6 = c5 + hw
    c7 = c6 + N_HEADS
    c8 = c7 + N_HEADS

    for l in range(depth):
        wl = w_in[l]
        ng = _row(norm_mix_g[l])
        lb = lb_all[l]
        lb_rows = jnp.stack([lb, jnp.log(lb), jnp.log1p(-lb)]).astype(F32)
        o_hg = _mixer_call(
            _hgrn_kernel, "hgrn2", x2, batch,
            [ng, wl[:, :c0].astype(BF16), lb_rows, _row(jnp.tile(hg_norm_g[l], N_HEADS)), win, lmask],
            _gla_scratch())

        w_gla = jnp.concatenate([
            _pad_heads(wl[:, c0:c0 + N_HEADS * GLA_DK], GLA_DK),
            _pad_heads(wl[:, c0 + N_HEADS * GLA_DK:c1], GLA_DK),
            wl[:, c1:c2], wl[:, c3:c4], _pad_cols(wl[:, c2:c3], LANES)], axis=1).astype(BF16)
        w2 = jnp.pad(_pad_heads(gla_gk_w2[l], GLA_DK), ((0, LANES - GLA_RANK), (0, 0))).astype(BF16)
        b2 = _row(_pad_heads(gla_gk_b[l], GLA_DK))
        o_gla = _mixer_call(
            _gla_kernel, "gla", x2, batch,
            [ng, w_gla, w2, b2, _row(jnp.tile(gla_norm_g[l], N_HEADS)), win, lmask],
            _gla_scratch())

        w_dn = jnp.concatenate([wl[:, c4:c6], _pad_cols(wl[:, c6:c8], LANES)], axis=1).astype(BF16)
        ab = jnp.zeros((2, LANES), F32)
        ab = ab.at[0, N_HEADS:2 * N_HEADS].set(dn_A_log[l].astype(F32))
        ab = ab.at[1, N_HEADS:2 * N_HEADS].set(dn_dt_bias[l].astype(F32))
        o_dn = _mixer_call(
            _dn_kernel, "deltanet", x2, batch,
            [ng, w_dn, dn_conv_w[l].astype(F32), ab, _row(jnp.tile(dn_norm_g[l], N_HEADS)), tri],
            _dn_scratch())

        x2 = _merge_call(
            x2, batch, [ng, wl[:, c8:].astype(BF16), _row(gate_b[l])], [o_hg, o_gla, o_dn],
            [w_branch[l].astype(BF16), w_out[l].astype(BF16)])

        x2 = _ffn_call(
            x2, batch,
            [_row(norm_ffn_g[l]), w_up[l].astype(BF16), ffn_conv_w[l].astype(F32), _row(ffn_conv_b[l]),
             w_down[l].astype(BF16), _row(norm_final_g)],
            final_norm=(l == depth - 1))
    return x2.reshape(batch, seq, d)
```

```python
import functools

import jax
import jax.numpy as jnp
import numpy as np
from jax import lax
from jax.experimental import pallas as pl
from jax.experimental.pallas import tpu as pltpu

F32 = jnp.float32
BF16 = jnp.bfloat16

D_MODEL = 1024
CHUNK = 64
NORM_EPS = 1e-6
N_HEADS = 4
HEAD_DIM = 128
BRANCH_WIDTH = N_HEADS * HEAD_DIM
GLA_DK = 64
GLA_RANK = 16
GLA_GATE_NORMALIZER = 16.0
DN_CONV = 4
N_BRANCH = 3
FFN_HIDDEN = 2816
FFN_CONV = 3
LANES = 128
CONV_PAD = 8
N_LEVELS = 6
N_TABLES = N_LEVELS + 2
LOG2E = 1.4426950408889634

TILE_MERGE = 512
TILE_FFN = 256
VMEM_LIMIT = 56 * 1024 * 1024


def _sigmoid(x):
    return 1.0 / (1.0 + jnp.exp(-x))


def _silu(x):
    return x * _sigmoid(x)


def _softplus(x):
    return jnp.maximum(x, 0.0) + jnp.log1p(jnp.exp(-jnp.abs(x)))


def _log_sigmoid(x):
    return -_softplus(-x)


def _rms_norm(x, g):
    y = x * lax.rsqrt(jnp.mean(x * x, axis=-1, keepdims=True) + NORM_EPS)
    return y * g


def _dot(a, b):
    return jnp.dot(a, b, preferred_element_type=F32)


def _bdot(a, b):
    return lax.dot_general(a, b, (((2,), (1,)), ((0,), (0,))), preferred_element_type=F32)


def _bdot_nt(a, b):
    return lax.dot_general(a, b, (((2,), (2,)), ((0,), (0,))), preferred_element_type=F32)


def _bdot_tn(a, b):
    return lax.dot_general(a, b, (((1,), (1,)), ((0,), (0,))), preferred_element_type=F32)


def _split_bf16(x):
    hi = x.astype(BF16)
    lo = (x - hi.astype(F32)).astype(BF16)
    return hi, lo


def _head(h):
    return slice(h * HEAD_DIM, (h + 1) * HEAD_DIM)


def _to_problems(x2d, nb):
    x3 = x2d.reshape(nb, x2d.shape[0] // nb, x2d.shape[1])
    return jnp.concatenate([x3[:, :, _head(h)] for h in range(N_HEADS)], axis=0)


def _from_problems(xp, nb):
    rows = nb * xp.shape[1]
    return jnp.concatenate([xp[h * nb:(h + 1) * nb].reshape(rows, HEAD_DIM) for h in range(N_HEADS)], axis=-1)


def _window_table():
    t = np.arange(CHUNK)
    u = np.arange(CHUNK)
    rows = []
    for lvl in range(N_LEVELS):
        p = CHUNK >> (lvl + 1)
        mid = (t // (2 * p)) * (2 * p) + p - 1
        upper = t > mid
        win = np.where(upper[:, None], (u[None, :] > mid[:, None]) & (u[None, :] <= t[:, None]),
                       (u[None, :] > t[:, None]) & (u[None, :] <= mid[:, None]))
        rows.append(win)
    rows.append(u[None, :] <= t[:, None])
    rows.append(u[None, :] > t[:, None])
    tab = np.concatenate(rows, axis=0).astype(np.float32)
    return np.concatenate([tab, tab], axis=1)


def _level_masks():
    t = np.arange(CHUNK)[:, None]
    s = np.arange(CHUNK)[None, :]
    out = []
    for lvl in range(N_LEVELS):
        p = CHUNK >> (lvl + 1)
        out.append((t // (2 * p) == s // (2 * p)) & (t % (2 * p) >= p) & (s % (2 * p) < p))
    out.append(t == s)
    return np.stack(out).astype(np.float32)


def _gla_step(q, k, v, g, nb, st_ref, win_ref, lmask_ref, ql_s, kl_s):
    q_bf = q.astype(BF16)
    k_bf = k.astype(BF16)
    g_hi, g_lo = _split_bf16(g * LOG2E)
    win = win_ref[...]
    ql_s[N_LEVELS] = q_bf
    kl_s[N_LEVELS] = k_bf
    last = []
    for b in range(nb):
        r = slice(b * CHUNK, (b + 1) * CHUNK)
        e = jnp.exp2(_dot(win, jnp.concatenate([g_hi[r], g_lo[r]], axis=0)))
        e_bf = e.astype(BF16)
        for l in range(N_LEVELS):
            lv = slice(l * CHUNK, (l + 1) * CHUNK)
            ql_s[l, r, :] = q_bf[r] * e_bf[lv]
            kl_s[l, r, :] = k_bf[r] * e_bf[lv]
        ql_s[N_LEVELS + 1, r, :] = q_bf[r] * e_bf[N_LEVELS * CHUNK:(N_LEVELS + 1) * CHUNK]
        kl_s[N_LEVELS + 1, r, :] = k_bf[r] * e_bf[(N_LEVELS + 1) * CHUNK:(N_LEVELS + 2) * CHUNK]
        last.append(e[(N_LEVELS + 1) * CHUNK - 1:(N_LEVELS + 1) * CHUNK, :])
    e_last = jnp.concatenate(last, axis=0)
    e_last = jnp.concatenate([e_last[:, _head(h)] for h in range(N_HEADS)], axis=0)[:, None, :]

    sc = None
    for l in range(N_LEVELS + 1):
        s = _bdot_nt(_to_problems(ql_s[l], nb), _to_problems(kl_s[l], nb)) * lmask_ref[l][None]
        sc = s if sc is None else sc + s
    v_p = _to_problems(v.astype(BF16), nb)
    st = st_ref[...]
    o = _bdot(sc.astype(BF16), v_p) + _bdot_nt(_to_problems(ql_s[N_LEVELS + 1], nb), st.astype(BF16))
    st_ref[...] = st * e_last + _bdot_tn(v_p, _to_problems(kl_s[N_LEVELS + 1], nb))
    return _from_problems(o, nb)


def _gated_head_norm(o, z, g):
    outs = []
    for h in range(N_HEADS):
        oh = o[:, _head(h)]
        inv = lax.rsqrt(jnp.mean(oh * oh, axis=-1, keepdims=True) + NORM_EPS)
        outs.append(oh * inv)
    return jnp.concatenate(outs, axis=-1) * g * _silu(z)


def _zero_at_start(*refs):
    @pl.when(pl.program_id(0) == 0)
    def _():
        for r in refs:
            r[...] = jnp.zeros(r.shape, r.dtype)


def _normed_tile(x_ref, ng_ref):
    nb, c, d = x_ref.shape
    return _rms_norm(x_ref[...].reshape(nb * c, d), ng_ref[...]).astype(BF16)


def _hgrn_kernel(x_ref, ng_ref, w_ref, lb_ref, hn_ref, win_ref, lmask_ref, o_ref, st_ref, ql_s, kl_s):
    _zero_at_start(st_ref)
    nb = x_ref.shape[0]
    w = BRANCH_WIDTH
    p = _dot(_normed_tile(x_ref, ng_ref), w_ref[...])
    z = p[:, w:2 * w]
    lb = lb_ref[0:1, :]
    log_lb = lb_ref[1:2, :]
    c = lb_ref[2:3, :] + _log_sigmoid(z)
    g = jnp.maximum(log_lb, c) + jnp.log1p(jnp.exp(-jnp.abs(log_lb - c)))
    k = (1.0 - lb) * _sigmoid(-z)
    q = _silu(p[:, 0:w]) * (HEAD_DIM ** -0.5)
    o = _gla_step(q, k, p[:, 2 * w:3 * w], g, nb, st_ref, win_ref, lmask_ref, ql_s, kl_s)
    o = _gated_head_norm(o, p[:, 3 * w:4 * w], hn_ref[...])
    o_ref[...] = o.reshape(o_ref.shape).astype(o_ref.dtype)


def _gla_kernel(x_ref, ng_ref, w_ref, w2_ref, b2_ref, hn_ref, win_ref, lmask_ref, o_ref, st_ref, ql_s, kl_s):
    _zero_at_start(st_ref)
    nb = x_ref.shape[0]
    w = BRANCH_WIDTH
    p = _dot(_normed_tile(x_ref, ng_ref), w_ref[...])
    gk = _dot(p[:, 4 * w:4 * w + LANES].astype(BF16), w2_ref[...]) + b2_ref[...]
    g = _log_sigmoid(gk) * (1.0 / GLA_GATE_NORMALIZER)
    q = p[:, 0:w] * (GLA_DK ** -0.5)
    o = _gla_step(q, p[:, w:2 * w], p[:, 2 * w:3 * w], g, nb, st_ref, win_ref, lmask_ref, ql_s, kl_s)
    o = _gated_head_norm(o, p[:, 3 * w:4 * w], hn_ref[...])
    o_ref[...] = o.reshape(o_ref.shape).astype(o_ref.dtype)


def _unit_lower_inverse(m):
    row = lax.broadcasted_iota(jnp.int32, (CHUNK, CHUNK), 0)
    col = lax.broadcasted_iota(jnp.int32, (CHUNK, CHUNK), 1)
    eye = (row == col).astype(F32)[None]
    t = None
    b = 1
    while b < CHUNK:
        sel = ((row // (2 * b) == col // (2 * b)) & (row % (2 * b) >= b) & (col % (2 * b) < b))[None]
        off = jnp.where(sel, m, 0.0)
        if b == 1:
            t = eye - off
        else:
            t_bf = t.astype(BF16)
            t = t - _bdot(_bdot(t_bf, off.astype(BF16)).astype(BF16), t_bf)
        b *= 2
    return t


def _dn_kernel(x_ref, ng_ref, w_ref, cw_ref, ab_ref, hn_ref, tri_ref, o_ref, st_ref, hist_s, ext_s):
    _zero_at_start(st_ref, hist_s)
    nb = x_ref.shape[0]
    rows = nb * CHUNK
    w = BRANCH_WIDTH
    p = _dot(_normed_tile(x_ref, ng_ref), w_ref[...])
    ext_s[:, 0:CONV_PAD, :] = hist_s[...]
    ext_s[:, CONV_PAD:CONV_PAD + CHUNK, :] = p[:, 0:3 * w].reshape(nb, CHUNK, 3 * w)
    hist_s[...] = ext_s[:, CHUNK:CHUNK + CONV_PAD, :]
    conv = cw_ref[0:1, :][None] * ext_s[:, CONV_PAD - 3:CONV_PAD - 3 + CHUNK, :]
    for j in range(1, DN_CONV):
        conv = conv + cw_ref[j:j + 1, :][None] * ext_s[:, CONV_PAD - 3 + j:CONV_PAD - 3 + j + CHUNK, :]
    qkv = _silu(conv.reshape(rows, 3 * w))
    qn, kn = [], []
    for hd in range(N_HEADS):
        qh = qkv[:, _head(hd)]
        kh = qkv[:, w + hd * HEAD_DIM:w + (hd + 1) * HEAD_DIM]
        qn.append(qh * lax.rsqrt(jnp.sum(qh * qh, axis=-1, keepdims=True) + NORM_EPS) * (HEAD_DIM ** -0.5))
        kn.append(kh * lax.rsqrt(jnp.sum(kh * kh, axis=-1, keepdims=True) + NORM_EPS))
    q_p = jnp.concatenate([x.reshape(nb, CHUNK, HEAD_DIM) for x in qn], axis=0)
    k_p = jnp.concatenate([x.reshape(nb, CHUNK, HEAD_DIM) for x in kn], axis=0)
    v_p = _to_problems(qkv[:, 2 * w:3 * w], nb)

    ba = p[:, 4 * w:4 * w + LANES]
    beta = _sigmoid(ba)
    g_hi, g_lo = _split_bf16(-jnp.exp(ab_ref[0:1, :]) * _softplus(ba + ab_ref[1:2, :]))
    tri = tri_ref[...]
    gcols, grows = [], []
    for b in range(nb):
        r = slice(b * CHUNK, (b + 1) * CHUNK)
        gcum = _dot(tri, jnp.concatenate([g_hi[r], g_lo[r]], axis=0))
        gcols.append(gcum)
        grows.append(gcum.T)
    ln = lambda h: slice(N_HEADS + h, N_HEADS + h + 1)
    gc = jnp.concatenate([gcols[b][None, :, ln(h)] for h in range(N_HEADS) for b in range(nb)], axis=0)
    gr = jnp.concatenate([grows[b][None, ln(h), :] for h in range(N_HEADS) for b in range(nb)], axis=0)
    beta3 = beta.reshape(nb, CHUNK, LANES)
    bc = jnp.concatenate([beta3[:, :, h:h + 1] for h in range(N_HEADS)], axis=0)

    row = lax.broadcasted_iota(jnp.int32, (CHUNK, CHUNK), 0)
    col = lax.broadcasted_iota(jnp.int32, (CHUNK, CHUNK), 1)
    incl = (row >= col)[None]
    strict = (row > col)[None]
    decay = jnp.exp(jnp.where(incl, gc - gr, -jnp.inf))
    g_last = gc[:, CHUNK - 1:CHUNK, :]
    e_in = jnp.exp(gc)
    e_out = jnp.exp(g_last - gc)
    e_last = jnp.exp(g_last)

    kb = k_p * bc
    k_bf = k_p.astype(BF16)
    kq = _bdot_nt(jnp.concatenate([kb, q_p], axis=1).astype(BF16), k_bf)
    m = jnp.where(strict, kq[:, 0:CHUNK] * decay, 0.0)
    sc = jnp.where(incl, kq[:, CHUNK:2 * CHUNK] * decay, 0.0).astype(BF16)
    t_inv = _unit_lower_inverse(m).astype(BF16)
    rhs = jnp.concatenate([v_p * bc, kb * e_in], axis=-1).astype(BF16)
    uw = _bdot(t_inv, rhs).astype(BF16)
    sc_uw = _bdot(sc, uw)
    k_uw = _bdot_tn((k_p * e_out).astype(BF16), uw)
    q_eff = (q_p * e_in - sc_uw[:, :, HEAD_DIM:]).astype(BF16)
    st = st_ref[...]
    st_bf = st.astype(BF16)
    o = _bdot(q_eff, st_bf) + sc_uw[:, :, 0:HEAD_DIM]
    st_ref[...] = st * e_last + k_uw[:, :, 0:HEAD_DIM] - _bdot(k_uw[:, :, HEAD_DIM:].astype(BF16), st_bf)
    o = _gated_head_norm(_from_problems(o, nb), p[:, 3 * w:4 * w], hn_ref[...])
    o_ref[...] = o.reshape(o_ref.shape).astype(o_ref.dtype)


def _merge_kernel(x_ref, ng_ref, wg_ref, gb_ref, oa_ref, ob_ref, oc_ref, wb_ref, wo_ref, y_ref):
    x = x_ref[...]
    h = _rms_norm(x, ng_ref[...]).astype(BF16)
    gates = _sigmoid(_dot(h, wg_ref[...]) + gb_ref[...])
    merged = None
    for n, o_n in enumerate((oa_ref, ob_ref, oc_ref)):
        term = gates[:, n * D_MODEL:(n + 1) * D_MODEL] * _dot(o_n[...], wb_ref[n])
        merged = term if merged is None else merged + term
    y_ref[...] = x + _dot(merged.astype(BF16), wo_ref[...])


def _ffn_kernel(x_ref, ng_ref, wu_ref, cw_ref, cb_ref, wd_ref, fg_ref, y_ref, ext_s, hist_s, act_s,
                *, final_norm):
    @pl.when(pl.program_id(1) == 0)
    def _():
        hist_s[...] = jnp.zeros(hist_s.shape, hist_s.dtype)

    tile = x_ref.shape[0]
    x = x_ref[...]
    h = _rms_norm(x, ng_ref[...]).astype(BF16)
    ext_s[0:CONV_PAD, :] = hist_s[...]
    ext_s[CONV_PAD:CONV_PAD + tile, :] = _dot(h, wu_ref[...])
    hist_s[...] = ext_s[tile:tile + CONV_PAD, :]
    blk = 2 * LANES
    for j0 in range(0, FFN_HIDDEN, blk):
        halves = []
        for base in (j0, FFN_HIDDEN + j0):
            cols = slice(base, base + blk)
            u = cb_ref[0:1, cols] + cw_ref[0:1, cols] * ext_s[CONV_PAD - 2:CONV_PAD - 2 + tile, cols]
            for j in range(1, FFN_CONV):
                u = u + cw_ref[j:j + 1, cols] * ext_s[CONV_PAD - 2 + j:CONV_PAD - 2 + j + tile, cols]
            halves.append(u)
        act_s[:, j0:j0 + blk] = (_silu(halves[0]) * halves[1]).astype(BF16)
    y = x + _dot(act_s[...], wd_ref[...])
    if final_norm:
        y = _rms_norm(y, fg_ref[...])
    y_ref[...] = y


def _const_spec(shape, grid_rank):
    nd = len(shape)
    if grid_rank == 1:
        return pl.BlockSpec(shape, lambda i, _nd=nd: (0,) * _nd, pipeline_mode=pl.Buffered(1))
    return pl.BlockSpec(shape, lambda b, i, _nd=nd: (0,) * _nd, pipeline_mode=pl.Buffered(1))


def _row_spec(tile, width, steps):
    return pl.BlockSpec((tile, width), lambda b, i, _s=steps: (b * _s + i, 0))


def _mixer_call(body, name, x3, consts, scratch):
    batch, seq, _ = x3.shape
    in_specs = ([pl.BlockSpec((batch, CHUNK, D_MODEL), lambda i: (0, i, 0))]
                + [_const_spec(c.shape, 1) for c in consts])
    return pl.pallas_call(
        body,
        grid=(seq // CHUNK,),
        in_specs=in_specs,
        out_specs=pl.BlockSpec((batch, CHUNK, BRANCH_WIDTH), lambda i: (0, i, 0)),
        out_shape=jax.ShapeDtypeStruct((batch, seq, BRANCH_WIDTH), BF16),
        scratch_shapes=scratch,
        compiler_params=pltpu.CompilerParams(dimension_semantics=("arbitrary",), vmem_limit_bytes=VMEM_LIMIT),
        name=name,
    )(x3, *consts)


def _gla_scratch(batch):
    rows = batch * CHUNK
    return [pltpu.VMEM((N_HEADS * batch, HEAD_DIM, HEAD_DIM), F32),
            pltpu.VMEM((N_TABLES, rows, BRANCH_WIDTH), BF16),
            pltpu.VMEM((N_TABLES, rows, BRANCH_WIDTH), BF16)]


def _dn_scratch(batch):
    return [pltpu.VMEM((N_HEADS * batch, HEAD_DIM, HEAD_DIM), F32),
            pltpu.VMEM((batch, CONV_PAD, 3 * BRANCH_WIDTH), F32),
            pltpu.VMEM((batch, CHUNK + CONV_PAD, 3 * BRANCH_WIDTH), F32)]


def _params2():
    return pltpu.CompilerParams(dimension_semantics=("arbitrary", "arbitrary"), vmem_limit_bytes=VMEM_LIMIT)


def _merge_call(x2, batch, consts_a, branches, consts_b):
    rows = x2.shape[0]
    steps = rows // batch // TILE_MERGE
    in_specs = ([_row_spec(TILE_MERGE, D_MODEL, steps)] + [_const_spec(c.shape, 2) for c in consts_a]
                + [_row_spec(TILE_MERGE, BRANCH_WIDTH, steps) for _ in branches]
                + [_const_spec(c.shape, 2) for c in consts_b])
    return pl.pallas_call(
        _merge_kernel,
        grid=(batch, steps),
        in_specs=in_specs,
        out_specs=_row_spec(TILE_MERGE, D_MODEL, steps),
        out_shape=jax.ShapeDtypeStruct((rows, D_MODEL), F32),
        compiler_params=_params2(),
        name="merge",
    )(x2, *consts_a, *branches, *consts_b)


def _ffn_call(x2, batch, consts, final_norm):
    rows = x2.shape[0]
    steps = rows // batch // TILE_FFN
    in_specs = [_row_spec(TILE_FFN, D_MODEL, steps)] + [_const_spec(c.shape, 2) for c in consts]
    return pl.pallas_call(
        functools.partial(_ffn_kernel, final_norm=final_norm),
        grid=(batch, steps),
        in_specs=in_specs,
        out_specs=_row_spec(TILE_FFN, D_MODEL, steps),
        out_shape=jax.ShapeDtypeStruct((rows, D_MODEL), F32),
        scratch_shapes=[pltpu.VMEM((TILE_FFN + CONV_PAD, 2 * FFN_HIDDEN), F32),
                        pltpu.VMEM((CONV_PAD, 2 * FFN_HIDDEN), F32),
                        pltpu.VMEM((TILE_FFN, FFN_HIDDEN), BF16)],
        compiler_params=_params2(),
        name="ffn_final" if final_norm else "ffn",
    )(x2, *consts)


def _pad_heads(wcols, dk):
    lead = wcols.shape[:-1]
    w4 = wcols.reshape(lead + (N_HEADS, dk))
    w4 = jnp.pad(w4, [(0, 0)] * len(lead) + [(0, 0), (0, HEAD_DIM - dk)])
    return w4.reshape(lead + (N_HEADS * HEAD_DIM,))


def _pad_cols(wcols, width):
    return jnp.pad(wcols, [(0, 0)] * (wcols.ndim - 1) + [(0, width - wcols.shape[-1])])


def _row(v):
    return v.reshape(1, -1).astype(F32)


def kernel(x, norm_mix_g, w_in, hg_lower_bounds, hg_norm_g, gla_gk_w2, gla_gk_b, gla_norm_g, dn_conv_w,
           dn_A_log, dn_dt_bias, dn_norm_g, w_branch, gate_b, w_out, norm_ffn_g, w_up, ffn_conv_w,
           ffn_conv_b, w_down, norm_final_g):
    batch, seq, d = x.shape
    depth = w_in.shape[0]
    assert d == D_MODEL and seq % max(CHUNK, TILE_MERGE, TILE_FFN) == 0

    lb_all = jnp.cumsum(jax.nn.softmax(hg_lower_bounds.astype(F32), axis=0), axis=0)
    lb_all = lb_all - lb_all[:1]
    win = jnp.asarray(_window_table(), BF16)
    lmask = jnp.asarray(_level_masks(), F32)
    tri1 = np.tril(np.ones((CHUNK, CHUNK), np.float32))
    tri = jnp.asarray(np.concatenate([tri1, tri1], axis=1), BF16)

    hw = BRANCH_WIDTH
    c0 = 4 * hw
    c1 = c0 + 2 * N_HEADS * GLA_DK
    c2 = c1 + hw
    c3 = c2 + GLA_RANK
    c4 = c3 + hw
    c5 = c4 + 3 * hw
    c6 = c5 + hw
    c7 = c6 + N_HEADS
    c8 = c7 + N_HEADS

    x3 = x
    for l in range(depth):
        wl = w_in[l]
        ng = _row(norm_mix_g[l])
        lb = lb_all[l]
        lb_rows = jnp.stack([lb, jnp.log(lb), jnp.log1p(-lb)]).astype(F32)
        o_hg = _mixer_call(
            _hgrn_kernel, "hgrn2", x3,
            [ng, wl[:, :c0].astype(BF16), lb_rows, _row(jnp.tile(hg_norm_g[l], N_HEADS)), win, lmask],
            _gla_scratch(batch))

        w_gla = jnp.concatenate([
            _pad_heads(wl[:, c0:c0 + N_HEADS * GLA_DK], GLA_DK),
            _pad_heads(wl[:, c0 + N_HEADS * GLA_DK:c1], GLA_DK),
            wl[:, c1:c2], wl[:, c3:c4], _pad_cols(wl[:, c2:c3], LANES)], axis=1).astype(BF16)
        w2 = jnp.pad(_pad_heads(gla_gk_w2[l], GLA_DK), ((0, LANES - GLA_RANK), (0, 0))).astype(BF16)
        b2 = _row(_pad_heads(gla_gk_b[l], GLA_DK))
        o_gla = _mixer_call(
            _gla_kernel, "gla", x3,
            [ng, w_gla, w2, b2, _row(jnp.tile(gla_norm_g[l], N_HEADS)), win, lmask],
            _gla_scratch(batch))

        w_dn = jnp.concatenate([wl[:, c4:c6], _pad_cols(wl[:, c6:c8], LANES)], axis=1).astype(BF16)
        ab = jnp.zeros((2, LANES), F32)
        ab = ab.at[0, N_HEADS:2 * N_HEADS].set(dn_A_log[l].astype(F32))
        ab = ab.at[1, N_HEADS:2 * N_HEADS].set(dn_dt_bias[l].astype(F32))
        o_dn = _mixer_call(
            _dn_kernel, "deltanet", x3,
            [ng, w_dn, dn_conv_w[l].astype(F32), ab, _row(jnp.tile(dn_norm_g[l], N_HEADS)), tri],
            _dn_scratch(batch))

        x2 = _merge_call(
            x3.reshape(batch * seq, d), batch, [ng, wl[:, c8:].astype(BF16), _row(gate_b[l])],
            [o.reshape(batch * seq, hw) for o in (o_hg, o_gla, o_dn)],
            [w_branch[l].astype(BF16), w_out[l].astype(BF16)])

        x2 = _ffn_call(
            x2, batch,
            [_row(norm_ffn_g[l]), w_up[l].astype(BF16), ffn_conv_w[l].astype(F32), _row(ffn_conv_b[l]),
             w_down[l].astype(BF16), _row(norm_final_g)],
            final_norm=(l == depth - 1))
        x3 = x2.reshape(batch, seq, d)
    return x3
```

```python
import functools

import jax
import jax.numpy as jnp
import numpy as np
from jax import lax
from jax.experimental import pallas as pl
from jax.experimental.pallas import tpu as pltpu

F32 = jnp.float32
BF16 = jnp.bfloat16

D_MODEL = 1024
CHUNK = 64
NORM_EPS = 1e-6
N_HEADS = 4
HEAD_DIM = 128
BRANCH_WIDTH = N_HEADS * HEAD_DIM
GLA_DK = 64
GLA_RANK = 16
GLA_GATE_NORMALIZER = 16.0
DN_CONV = 4
N_BRANCH = 3
FFN_HIDDEN = 2816
FFN_CONV = 3
LANES = 128
CONV_PAD = 8
N_LEVELS = 6
LOG2E = 1.4426950408889634
SAFE_LOG2_DECAY = -200.0

TILE_MERGE = 512
TILE_FFN = 256
VMEM_LIMIT = 56 * 1024 * 1024


def _sigmoid(x):
    return 1.0 / (1.0 + jnp.exp(-x))


def _silu(x):
    return x * _sigmoid(x)


def _softplus(x):
    return jnp.maximum(x, 0.0) + jnp.log1p(jnp.exp(-jnp.abs(x)))


def _log_sigmoid(x):
    return -_softplus(-x)


def _rms_norm(x, g):
    y = x * lax.rsqrt(jnp.mean(x * x, axis=-1, keepdims=True) + NORM_EPS)
    return y * g


def _dot(a, b):
    return jnp.dot(a, b, preferred_element_type=F32)


def _bdot(a, b):
    return lax.dot_general(a, b, (((2,), (1,)), ((0,), (0,))), preferred_element_type=F32)


def _bdot_nt(a, b):
    return lax.dot_general(a, b, (((2,), (2,)), ((0,), (0,))), preferred_element_type=F32)


def _bdot_tn(a, b):
    return lax.dot_general(a, b, (((1,), (1,)), ((0,), (0,))), preferred_element_type=F32)


def _split_bf16(x):
    hi = x.astype(BF16)
    lo = (x - hi.astype(F32)).astype(BF16)
    return hi, lo


def _head(h):
    return slice(h * HEAD_DIM, (h + 1) * HEAD_DIM)


def _heads_to_problems(x3):
    return jnp.concatenate([x3[:, :, _head(h)] for h in range(N_HEADS)], axis=0)


def _to_problems(x2d, nb):
    return _heads_to_problems(x2d.reshape(nb, x2d.shape[0] // nb, x2d.shape[1]))


def _from_problems(xp, nb):
    rows = nb * xp.shape[1]
    return jnp.concatenate([xp[h * nb:(h + 1) * nb].reshape(rows, HEAD_DIM) for h in range(N_HEADS)], axis=-1)


def _run_interleaved(*stage_generators):
    live = list(stage_generators)
    while live:
        live = [s for s in live if next(s, True) is None]


def _causal_masks():
    row = lax.broadcasted_iota(jnp.int32, (CHUNK, CHUNK), 0)
    col = lax.broadcasted_iota(jnp.int32, (CHUNK, CHUNK), 1)
    return row, col, (row >= col)[None], (row > col)[None]


def _prefix_suffix_rows():
    t = np.arange(CHUNK)[:, None]
    u = np.arange(CHUNK)[None, :]
    return [u <= t, u > t]


def _edge_table():
    tab = np.concatenate(_prefix_suffix_rows(), axis=0).astype(np.float32)
    return np.concatenate([tab, tab], axis=1)


def _level_table():
    t = np.arange(CHUNK)
    u = np.arange(CHUNK)
    rows = []
    for lvl in range(N_LEVELS):
        p = CHUNK >> (lvl + 1)
        mid = (t // (2 * p)) * (2 * p) + p - 1
        upper = t > mid
        rows.append(np.where(upper[:, None], (u[None, :] > mid[:, None]) & (u[None, :] <= t[:, None]),
                             (u[None, :] > t[:, None]) & (u[None, :] <= mid[:, None])))
    tab = np.concatenate(rows, axis=0).astype(np.float32)
    return np.concatenate([tab, tab], axis=1)


def _level_masks():
    t = np.arange(CHUNK)[:, None]
    s = np.arange(CHUNK)[None, :]
    out = []
    for lvl in range(N_LEVELS):
        p = CHUNK >> (lvl + 1)
        out.append((t // (2 * p) == s // (2 * p)) & (t % (2 * p) >= p) & (s % (2 * p) < p))
    out.append(t == s)
    return np.stack(out).astype(np.float32)


def _gated_head_norm(o, zg, g):
    outs = []
    for h in range(N_HEADS):
        oh = o[:, _head(h)]
        inv = lax.rsqrt(jnp.mean(oh * oh, axis=-1, keepdims=True) + NORM_EPS)
        outs.append(oh * inv)
    return jnp.concatenate(outs, axis=-1) * g * zg


def _zero(*refs):
    for r in refs:
        r[...] = jnp.zeros(r.shape, r.dtype)


def _gla_front(kind, x_ref, ng_ref, w_ref, aux_refs, edge_ref, slot, q_s, k_s, v_s, zg_s, gs_s, tab_s, safe_s, p_s):
    nb, _, d = x_ref.shape
    rows = nb * CHUNK
    w = BRANCH_WIDTH
    h = _rms_norm(x_ref[...].reshape(rows, d), ng_ref[...]).astype(BF16)
    yield
    for c0 in range(0, p_s.shape[1], w):
        c1 = min(c0 + w, p_s.shape[1])
        p_s[:, c0:c1] = _dot(h, w_ref[:, c0:c1])
        yield

    def store_decay(g):
        g_hi, g_lo = _split_bf16(g * LOG2E)
        for b in range(nb):
            r = slice(b * CHUNK, (b + 1) * CHUNK)
            gs_s[slot, b] = jnp.concatenate([g_hi[r], g_lo[r]], axis=0)

    if kind == "hgrn2":
        lb_ref, = aux_refs
        q_s[slot] = (_silu(p_s[:, 0:w]) * (HEAD_DIM ** -0.5)).astype(BF16)
        yield
        z = p_s[:, w:2 * w]
        lb = lb_ref[0:1, :]
        log_lb = lb_ref[1:2, :]
        c = lb_ref[2:3, :] + _log_sigmoid(z)
        store_decay(jnp.maximum(log_lb, c) + jnp.log1p(jnp.exp(-jnp.abs(log_lb - c))))
        k_s[slot] = ((1.0 - lb) * _sigmoid(-z)).astype(BF16)
        yield
    else:
        w2_ref, b2_ref = aux_refs
        q_s[slot] = (p_s[:, 0:w] * (GLA_DK ** -0.5)).astype(BF16)
        k_s[slot] = p_s[:, w:2 * w].astype(BF16)
        yield
        gk = _dot(p_s[:, 4 * w:4 * w + LANES].astype(BF16), w2_ref[...]) + b2_ref[...]
        store_decay(_log_sigmoid(gk) * (1.0 / GLA_GATE_NORMALIZER))
        yield
    v_s[slot] = p_s[:, 2 * w:3 * w].astype(BF16)
    zg_s[slot] = _silu(p_s[:, 3 * w:4 * w])
    yield
    edge = edge_ref[...]
    low_mark = None
    for b in range(nb):
        tab = _dot(edge, gs_s[slot, b])
        tab_s[slot, b] = tab
        total = tab[CHUNK - 1:CHUNK, :]
        low_mark = total if low_mark is None else jnp.minimum(low_mark, total)
        if b % 2 == 1:
            yield
    safe_s[slot] = (jnp.min(low_mark) >= SAFE_LOG2_DECAY).astype(jnp.int32)


def _gla_scores_centered(q, k, tab, nb):
    _, _, incl, _ = _causal_masks()
    pre = tab[:, 0:CHUNK, :]
    mid = 0.5 * pre[:, CHUNK - 1:CHUNK, :]
    rows = nb * CHUNK
    qf = q * jnp.exp2(pre - mid).astype(BF16).reshape(rows, -1)
    kf = k * jnp.exp2(mid - pre).astype(BF16).reshape(rows, -1)
    yield
    s = _bdot_nt(_to_problems(qf, nb), _to_problems(kf, nb))
    yield jnp.where(incl, s, 0.0)


def _gla_scores_levels(q, k, gs, level_ref, lmask_ref, nb):
    sc = _bdot_nt(_to_problems(q, nb), _to_problems(k, nb)) * lmask_ref[N_LEVELS][None]
    for l in range(N_LEVELS):
        wt = level_ref[l * CHUNK:(l + 1) * CHUNK, :]
        e = jnp.exp2(jnp.concatenate([_dot(wt, gs[b]) for b in range(nb)], axis=0)).astype(BF16)
        sc = sc + _bdot_nt(_to_problems(q * e, nb), _to_problems(k * e, nb)) * lmask_ref[l][None]
    return sc


def _gla_back(centered, slot, q_s, k_s, v_s, zg_s, gs_s, tab_s, st_ref, hn_ref, level_ref, lmask_ref, o_ref):
    nb = o_ref.shape[0]
    rows = nb * CHUNK
    q = q_s[slot]
    k = k_s[slot]
    tab = tab_s[slot]
    if centered:
        scores = _gla_scores_centered(q, k, tab, nb)
        yield next(scores)
        sc = next(scores)
    else:
        sc = _gla_scores_levels(q, k, [gs_s[slot, b] for b in range(nb)], level_ref, lmask_ref, nb)
    yield
    pre = tab[:, 0:CHUNK, :]
    e_in = jnp.exp2(pre).astype(BF16).reshape(rows, -1)
    e_out = jnp.exp2(tab[:, CHUNK:2 * CHUNK, :]).astype(BF16).reshape(rows, -1)
    e_last = _heads_to_problems(jnp.exp2(pre[:, CHUNK - 1:CHUNK, :]))
    v_p = _to_problems(v_s[slot], nb)
    st = st_ref[...]
    o = _bdot(sc.astype(BF16), v_p) + _bdot_nt(_to_problems(q * e_in, nb), st.astype(BF16))
    yield
    st_ref[...] = st * e_last + _bdot_tn(v_p, _to_problems(k * e_out, nb))
    yield
    y = _gated_head_norm(_from_problems(o, nb), zg_s[slot], hn_ref[...])
    o_ref[...] = y.reshape(o_ref.shape).astype(o_ref.dtype)


def _gla_kernel(kind, n_aux, x_ref, ng_ref, w_ref, *rest):
    aux_refs = rest[:n_aux]
    (hn_ref, edge_ref, level_ref, lmask_ref, o_ref,
     st_ref, q_s, k_s, v_s, zg_s, gs_s, tab_s, safe_s, p_s) = rest[n_aux:]
    j = pl.program_id(0)
    slot = j % 2
    prev = 1 - slot
    bufs = (q_s, k_s, v_s, zg_s, gs_s, tab_s)

    @pl.when(j == 0)
    def _():
        _zero(st_ref, *bufs)
        safe_s[0] = 1
        safe_s[1] = 1

    def front():
        return _gla_front(kind, x_ref, ng_ref, w_ref, aux_refs, edge_ref, slot, *bufs, safe_s, p_s)

    def back(centered):
        return _gla_back(centered, prev, *bufs, st_ref, hn_ref, level_ref, lmask_ref, o_ref)

    @pl.when(safe_s[prev] == 1)
    def _():
        _run_interleaved(front(), back(True))

    @pl.when(safe_s[prev] != 1)
    def _():
        _run_interleaved(back(False))
        _run_interleaved(front())


def _dn_front(x_ref, ng_ref, w_ref, cw_ref, ab_ref, slot, hist_s, ext_s, q_s, k_s, v_s, zg_s, bg_s):
    nb, _, d = x_ref.shape
    rows = nb * CHUNK
    w = BRANCH_WIDTH
    h = _rms_norm(x_ref[...].reshape(rows, d), ng_ref[...]).astype(BF16)
    yield
    for part, dst in enumerate((q_s, k_s, v_s)):
        cols = slice(part * w, (part + 1) * w)
        ext_s[:, 0:CONV_PAD, cols] = hist_s[:, :, cols]
        ext_s[:, CONV_PAD:CONV_PAD + CHUNK, cols] = _dot(h, w_ref[:, cols]).reshape(nb, CHUNK, w)
        hist_s[:, :, cols] = ext_s[:, CHUNK:CHUNK + CONV_PAD, cols]
        yield
        conv = cw_ref[0:1, cols][None] * ext_s[:, CONV_PAD - 3:CONV_PAD - 3 + CHUNK, cols]
        for tap in range(1, DN_CONV):
            conv = conv + cw_ref[tap:tap + 1, cols][None] * ext_s[:, CONV_PAD - 3 + tap:CONV_PAD - 3 + tap + CHUNK, cols]
        act = _silu(conv.reshape(rows, w))
        if part < 2:
            scale = HEAD_DIM ** -0.5 if part == 0 else 1.0
            heads = []
            for hd in range(N_HEADS):
                xh = act[:, _head(hd)]
                heads.append(xh * lax.rsqrt(jnp.sum(xh * xh, axis=-1, keepdims=True) + NORM_EPS) * scale)
            act = jnp.concatenate(heads, axis=-1)
        dst[slot] = _to_problems(act, nb)
        yield
    zg_s[slot] = _silu(_dot(h, w_ref[:, 3 * w:4 * w]))
    yield
    ba = _dot(h, w_ref[:, 4 * w:4 * w + LANES])
    bg_s[slot, 0] = _sigmoid(ba)
    bg_s[slot, 1] = -jnp.exp(ab_ref[0:1, :]) * _softplus(ba + ab_ref[1:2, :])


def _dn_back(slot, q_s, k_s, v_s, zg_s, bg_s, st_ref, hn_ref, tri_ref, o_ref):
    nb = o_ref.shape[0]
    q_p = q_s[slot]
    k_p = k_s[slot]
    v_p = v_s[slot]
    beta = bg_s[slot, 0]
    g_hi, g_lo = _split_bf16(bg_s[slot, 1])
    tri = tri_ref[...]
    gcols, grows = [], []
    for b in range(nb):
        r = slice(b * CHUNK, (b + 1) * CHUNK)
        gcum = _dot(tri, jnp.concatenate([g_hi[r], g_lo[r]], axis=0))
        gcols.append(gcum)
        grows.append(gcum.T)
    ln = lambda h: slice(N_HEADS + h, N_HEADS + h + 1)
    gc = jnp.concatenate([gcols[b][None, :, ln(h)] for h in range(N_HEADS) for b in range(nb)], axis=0)
    gr = jnp.concatenate([grows[b][None, ln(h), :] for h in range(N_HEADS) for b in range(nb)], axis=0)
    beta3 = beta.reshape(nb, CHUNK, LANES)
    bc = jnp.concatenate([beta3[:, :, h:h + 1] for h in range(N_HEADS)], axis=0)

    row, col, incl, strict = _causal_masks()
    decay = jnp.exp(jnp.where(incl, gc - gr, -jnp.inf))
    g_last = gc[:, CHUNK - 1:CHUNK, :]
    e_in = jnp.exp(gc)
    e_out = jnp.exp(g_last - gc)
    e_last = jnp.exp(g_last)

    kb = k_p * bc
    kq = _bdot_nt(jnp.concatenate([kb, q_p], axis=1).astype(BF16), k_p.astype(BF16))
    yield
    m = jnp.where(strict, kq[:, 0:CHUNK] * decay, 0.0)
    sc = jnp.where(incl, kq[:, CHUNK:2 * CHUNK] * decay, 0.0).astype(BF16)
    eye = (row == col).astype(F32)[None]
    t = None
    blk = 1
    while blk < CHUNK:
        sel = ((row // (2 * blk) == col // (2 * blk)) & (row % (2 * blk) >= blk) & (col % (2 * blk) < blk))[None]
        off = jnp.where(sel, m, 0.0)
        if blk == 1:
            t = eye - off
        else:
            t_bf = t.astype(BF16)
            half = _bdot(t_bf, off.astype(BF16)).astype(BF16)
            yield
            t = t - _bdot(half, t_bf)
            yield
        blk *= 2
    rhs = jnp.concatenate([v_p * bc, kb * e_in], axis=-1).astype(BF16)
    uw = _bdot(t.astype(BF16), rhs).astype(BF16)
    yield
    sc_uw = _bdot(sc, uw)
    k_uw = _bdot_tn((k_p * e_out).astype(BF16), uw)
    yield
    q_eff = (q_p * e_in - sc_uw[:, :, HEAD_DIM:]).astype(BF16)
    st = st_ref[...]
    st_bf = st.astype(BF16)
    o = _bdot(q_eff, st_bf) + sc_uw[:, :, 0:HEAD_DIM]
    st_ref[...] = st * e_last + k_uw[:, :, 0:HEAD_DIM] - _bdot(k_uw[:, :, HEAD_DIM:].astype(BF16), st_bf)
    yield
    y = _gated_head_norm(_from_problems(o, nb), zg_s[slot], hn_ref[...])
    o_ref[...] = y.reshape(o_ref.shape).astype(o_ref.dtype)


def _dn_kernel(x_ref, ng_ref, w_ref, cw_ref, ab_ref, hn_ref, tri_ref, o_ref,
               st_ref, hist_s, ext_s, q_s, k_s, v_s, zg_s, bg_s):
    j = pl.program_id(0)
    slot = j % 2
    bufs = (q_s, k_s, v_s, zg_s, bg_s)

    @pl.when(j == 0)
    def _():
        _zero(st_ref, hist_s, *bufs)

    _run_interleaved(
        _dn_front(x_ref, ng_ref, w_ref, cw_ref, ab_ref, slot, hist_s, ext_s, *bufs),
        _dn_back(1 - slot, *bufs, st_ref, hn_ref, tri_ref, o_ref))


def _merge_kernel(x_ref, ng_ref, wg_ref, gb_ref, oa_ref, ob_ref, oc_ref, wb_ref, wo_ref, y_ref):
    x = x_ref[...]
    h = _rms_norm(x, ng_ref[...]).astype(BF16)
    gates = _sigmoid(_dot(h, wg_ref[...]) + gb_ref[...])
    merged = None
    for n, o_n in enumerate((oa_ref, ob_ref, oc_ref)):
        term = gates[:, n * D_MODEL:(n + 1) * D_MODEL] * _dot(o_n[...], wb_ref[n])
        merged = term if merged is None else merged + term
    y_ref[...] = x + _dot(merged.astype(BF16), wo_ref[...])


def _ffn_kernel(x_ref, ng_ref, wu_ref, cw_ref, cb_ref, wd_ref, fg_ref, y_ref, ext_s, hist_s, act_s,
                *, final_norm):
    @pl.when(pl.program_id(1) == 0)
    def _():
        hist_s[...] = jnp.zeros(hist_s.shape, hist_s.dtype)

    tile = x_ref.shape[0]
    x = x_ref[...]
    h = _rms_norm(x, ng_ref[...]).astype(BF16)
    ext_s[0:CONV_PAD, :] = hist_s[...]
    ext_s[CONV_PAD:CONV_PAD + tile, :] = _dot(h, wu_ref[...])
    hist_s[...] = ext_s[tile:tile + CONV_PAD, :]
    blk = 2 * LANES
    for j0 in range(0, FFN_HIDDEN, blk):
        halves = []
        for base in (j0, FFN_HIDDEN + j0):
            cols = slice(base, base + blk)
            u = cb_ref[0:1, cols] + cw_ref[0:1, cols] * ext_s[CONV_PAD - 2:CONV_PAD - 2 + tile, cols]
            for j in range(1, FFN_CONV):
                u = u + cw_ref[j:j + 1, cols] * ext_s[CONV_PAD - 2 + j:CONV_PAD - 2 + j + tile, cols]
            halves.append(u)
        act_s[:, j0:j0 + blk] = (_silu(halves[0]) * halves[1]).astype(BF16)
    y = x + _dot(act_s[...], wd_ref[...])
    if final_norm:
        y = _rms_norm(y, fg_ref[...])
    y_ref[...] = y


def _const_spec(shape, grid_rank):
    nd = len(shape)
    if grid_rank == 1:
        return pl.BlockSpec(shape, lambda i, _nd=nd: (0,) * _nd, pipeline_mode=pl.Buffered(1))
    return pl.BlockSpec(shape, lambda b, i, _nd=nd: (0,) * _nd, pipeline_mode=pl.Buffered(1))


def _row_spec(tile, width, steps):
    return pl.BlockSpec((tile, width), lambda b, i, _s=steps: (b * _s + i, 0))


def _mixer_call(body, name, x3, consts, scratch):
    batch, seq, _ = x3.shape
    n = seq // CHUNK
    in_specs = ([pl.BlockSpec((batch, CHUNK, D_MODEL), lambda j: (0, jnp.minimum(j, n - 1), 0))]
                + [_const_spec(c.shape, 1) for c in consts])
    return pl.pallas_call(
        body,
        grid=(n + 1,),
        in_specs=in_specs,
        out_specs=pl.BlockSpec((batch, CHUNK, BRANCH_WIDTH), lambda j: (0, jnp.maximum(j - 1, 0), 0)),
        out_shape=jax.ShapeDtypeStruct((batch, seq, BRANCH_WIDTH), BF16),
        scratch_shapes=scratch,
        compiler_params=pltpu.CompilerParams(dimension_semantics=("arbitrary",), vmem_limit_bytes=VMEM_LIMIT),
        name=name,
    )(x3, *consts)


def _gla_scratch(batch, proj_width):
    rows = batch * CHUNK
    w = BRANCH_WIDTH
    return [pltpu.VMEM((N_HEADS * batch, HEAD_DIM, HEAD_DIM), F32),
            pltpu.VMEM((2, rows, w), BF16), pltpu.VMEM((2, rows, w), BF16), pltpu.VMEM((2, rows, w), BF16),
            pltpu.VMEM((2, rows, w), F32),
            pltpu.VMEM((2, batch, 2 * CHUNK, w), BF16),
            pltpu.VMEM((2, batch, 2 * CHUNK, w), F32),
            pltpu.SMEM((2,), jnp.int32),
            pltpu.VMEM((rows, proj_width), F32)]


def _dn_scratch(batch):
    rows = batch * CHUNK
    prob = (2, N_HEADS * batch, CHUNK, HEAD_DIM)
    return [pltpu.VMEM((N_HEADS * batch, HEAD_DIM, HEAD_DIM), F32),
            pltpu.VMEM((batch, CONV_PAD, 3 * BRANCH_WIDTH), F32),
            pltpu.VMEM((batch, CHUNK + CONV_PAD, 3 * BRANCH_WIDTH), F32),
            pltpu.VMEM(prob, F32), pltpu.VMEM(prob, F32), pltpu.VMEM(prob, F32),
            pltpu.VMEM((2, rows, BRANCH_WIDTH), F32),
            pltpu.VMEM((2, 2, rows, LANES), F32)]


def _params2():
    return pltpu.CompilerParams(dimension_semantics=("arbitrary", "arbitrary"), vmem_limit_bytes=VMEM_LIMIT)


def _merge_call(x2, batch, consts_a, branches, consts_b):
    rows = x2.shape[0]
    steps = rows // batch // TILE_MERGE
    in_specs = ([_row_spec(TILE_MERGE, D_MODEL, steps)] + [_const_spec(c.shape, 2) for c in consts_a]
                + [_row_spec(TILE_MERGE, BRANCH_WIDTH, steps) for _ in branches]
                + [_const_spec(c.shape, 2) for c in consts_b])
    return pl.pallas_call(
        _merge_kernel,
        grid=(batch, steps),
        in_specs=in_specs,
        out_specs=_row_spec(TILE_MERGE, D_MODEL, steps),
        out_shape=jax.ShapeDtypeStruct((rows, D_MODEL), F32),
        compiler_params=_params2(),
        name="merge",
    )(x2, *consts_a, *branches, *consts_b)


def _ffn_call(x2, batch, consts, final_norm):
    rows = x2.shape[0]
    steps = rows // batch // TILE_FFN
    in_specs = [_row_spec(TILE_FFN, D_MODEL, steps)] + [_const_spec(c.shape, 2) for c in consts]
    return pl.pallas_call(
        functools.partial(_ffn_kernel, final_norm=final_norm),
        grid=(batch, steps),
        in_specs=in_specs,
        out_specs=_row_spec(TILE_FFN, D_MODEL, steps),
        out_shape=jax.ShapeDtypeStruct((rows, D_MODEL), F32),
        scratch_shapes=[pltpu.VMEM((TILE_FFN + CONV_PAD, 2 * FFN_HIDDEN), F32),
                        pltpu.VMEM((CONV_PAD, 2 * FFN_HIDDEN), F32),
                        pltpu.VMEM((TILE_FFN, FFN_HIDDEN), BF16)],
        compiler_params=_params2(),
        name="ffn_final" if final_norm else "ffn",
    )(x2, *consts)


def _pad_heads(wcols, dk):
    lead = wcols.shape[:-1]
    w4 = wcols.reshape(lead + (N_HEADS, dk))
    w4 = jnp.pad(w4, [(0, 0)] * len(lead) + [(0, 0), (0, HEAD_DIM - dk)])
    return w4.reshape(lead + (N_HEADS * HEAD_DIM,))


def _pad_cols(wcols, width):
    return jnp.pad(wcols, [(0, 0)] * (wcols.ndim - 1) + [(0, width - wcols.shape[-1])])


def _row(v):
    return v.reshape(1, -1).astype(F32)


def kernel(x, norm_mix_g, w_in, hg_lower_bounds, hg_norm_g, gla_gk_w2, gla_gk_b, gla_norm_g, dn_conv_w,
           dn_A_log, dn_dt_bias, dn_norm_g, w_branch, gate_b, w_out, norm_ffn_g, w_up, ffn_conv_w,
           ffn_conv_b, w_down, norm_final_g):
    batch, seq, d = x.shape
    depth = w_in.shape[0]
    assert d == D_MODEL and seq % max(CHUNK, TILE_MERGE, TILE_FFN) == 0

    lb_all = jnp.cumsum(jax.nn.softmax(hg_lower_bounds.astype(F32), axis=0), axis=0)
    lb_all = lb_all - lb_all[:1]
    edge = jnp.asarray(_edge_table(), BF16)
    level = jnp.asarray(_level_table(), BF16)
    lmask = jnp.asarray(_level_masks(), F32)
    tri1 = np.tril(np.ones((CHUNK, CHUNK), np.float32))
    tri = jnp.asarray(np.concatenate([tri1, tri1], axis=1), BF16)

    hw = BRANCH_WIDTH
    c0 = 4 * hw
    c1 = c0 + 2 * N_HEADS * GLA_DK
    c2 = c1 + hw
    c3 = c2 + GLA_RANK
    c4 = c3 + hw
    c5 = c4 + 3 * hw
    c6 = c5 + hw
    c7 = c6 + N_HEADS
    c8 = c7 + N_HEADS

    x3 = x
    for l in range(depth):
        wl = w_in[l]
        ng = _row(norm_mix_g[l])
        lb = lb_all[l]
        lb_rows = jnp.stack([lb, jnp.log(lb), jnp.log1p(-lb)]).astype(F32)
        o_hg = _mixer_call(
            functools.partial(_gla_kernel, "hgrn2", 1), "hgrn2", x3,
            [ng, wl[:, :c0].astype(BF16), lb_rows, _row(jnp.tile(hg_norm_g[l], N_HEADS)), edge, level, lmask],
            _gla_scratch(batch, c0))

        w_gla = jnp.concatenate([
            _pad_heads(wl[:, c0:c0 + N_HEADS * GLA_DK], GLA_DK),
            _pad_heads(wl[:, c0 + N_HEADS * GLA_DK:c1], GLA_DK),
            wl[:, c1:c2], wl[:, c3:c4], _pad_cols(wl[:, c2:c3], LANES)], axis=1).astype(BF16)
        w2 = jnp.pad(_pad_heads(gla_gk_w2[l], GLA_DK), ((0, LANES - GLA_RANK), (0, 0))).astype(BF16)
        b2 = _row(_pad_heads(gla_gk_b[l], GLA_DK))
        o_gla = _mixer_call(
            functools.partial(_gla_kernel, "gla", 2), "gla", x3,
            [ng, w_gla, w2, b2, _row(jnp.tile(gla_norm_g[l], N_HEADS)), edge, level, lmask],
            _gla_scratch(batch, w_gla.shape[1]))

        w_dn = jnp.concatenate([wl[:, c4:c6], _pad_cols(wl[:, c6:c8], LANES)], axis=1).astype(BF16)
        ab = jnp.zeros((2, LANES), F32)
        ab = ab.at[0, N_HEADS:2 * N_HEADS].set(dn_A_log[l].astype(F32))
        ab = ab.at[1, N_HEADS:2 * N_HEADS].set(dn_dt_bias[l].astype(F32))
        o_dn = _mixer_call(
            _dn_kernel, "deltanet", x3,
            [ng, w_dn, dn_conv_w[l].astype(F32), ab, _row(jnp.tile(dn_norm_g[l], N_HEADS)), tri],
            _dn_scratch(batch))

        x2 = _merge_call(
            x3.reshape(batch * seq, d), batch, [ng, wl[:, c8:].astype(BF16), _row(gate_b[l])],
            [o.reshape(batch * seq, hw) for o in (o_hg, o_gla, o_dn)],
            [w_branch[l].astype(BF16), w_out[l].astype(BF16)])

        x2 = _ffn_call(
            x2, batch,
            [_row(norm_ffn_g[l]), w_up[l].astype(BF16), ffn_conv_w[l].astype(F32), _row(ffn_conv_b[l]),
             w_down[l].astype(BF16), _row(norm_final_g)],
            final_norm=(l == depth - 1))
        x3 = x2.reshape(batch, seq, d)
    return x3
```

```python
import functools

import jax
import jax.numpy as jnp
import numpy as np
from jax import lax
from jax.experimental import pallas as pl
from jax.experimental.pallas import tpu as pltpu

F32 = jnp.float32
BF16 = jnp.bfloat16

D_MODEL = 1024
CHUNK = 64
NORM_EPS = 1e-6
N_HEADS = 4
HEAD_DIM = 128
BRANCH_WIDTH = N_HEADS * HEAD_DIM
GLA_DK = 64
GLA_RANK = 16
GLA_GATE_NORMALIZER = 16.0
DN_CONV = 4
N_BRANCH = 3
FFN_HIDDEN = 2816
FFN_CONV = 3
LANES = 128
CONV_PAD = 8
N_LEVELS = 6
LOG2E = 1.4426950408889634
SAFE_LOG2_DECAY = -200.0

TILE_MERGE = 512
TILE_FFN = 512
VMEM_LIMIT = 56 * 1024 * 1024


def _sigmoid(x):
    return 1.0 / (1.0 + jnp.exp(-x))


def _silu(x):
    return x * _sigmoid(x)


def _softplus(x):
    return jnp.maximum(x, 0.0) + jnp.log1p(jnp.exp(-jnp.abs(x)))


def _log_sigmoid(x):
    return -_softplus(-x)


def _rms_norm(x, g):
    y = x * lax.rsqrt(jnp.mean(x * x, axis=-1, keepdims=True) + NORM_EPS)
    return y * g


def _dot(a, b):
    return jnp.dot(a, b, preferred_element_type=F32)


def _bdot(a, b):
    return lax.dot_general(a, b, (((2,), (1,)), ((0,), (0,))), preferred_element_type=F32)


def _bdot_nt(a, b):
    return lax.dot_general(a, b, (((2,), (2,)), ((0,), (0,))), preferred_element_type=F32)


def _bdot_tn(a, b):
    return lax.dot_general(a, b, (((1,), (1,)), ((0,), (0,))), preferred_element_type=F32)


def _split_bf16(x):
    hi = x.astype(BF16)
    lo = (x - hi.astype(F32)).astype(BF16)
    return hi, lo


def _head(h):
    return slice(h * HEAD_DIM, (h + 1) * HEAD_DIM)


def _heads_to_problems(x3):
    return jnp.concatenate([x3[:, :, _head(h)] for h in range(N_HEADS)], axis=0)


def _to_problems(x2d, nb):
    return _heads_to_problems(x2d.reshape(nb, x2d.shape[0] // nb, x2d.shape[1]))


def _from_problems(xp, nb):
    rows = nb * xp.shape[1]
    return jnp.concatenate([xp[h * nb:(h + 1) * nb].reshape(rows, HEAD_DIM) for h in range(N_HEADS)], axis=-1)


def _run_interleaved(*stage_generators):
    live = list(stage_generators)
    while live:
        live = [s for s in live if next(s, True) is None]


def _causal_masks():
    row = lax.broadcasted_iota(jnp.int32, (CHUNK, CHUNK), 0)
    col = lax.broadcasted_iota(jnp.int32, (CHUNK, CHUNK), 1)
    return row, col, (row >= col)[None], (row > col)[None]


def _prefix_suffix_rows():
    t = np.arange(CHUNK)[:, None]
    u = np.arange(CHUNK)[None, :]
    return [u <= t, u > t]


def _edge_table():
    tab = np.concatenate(_prefix_suffix_rows(), axis=0).astype(np.float32)
    return np.concatenate([tab, tab], axis=1)


def _level_table():
    t = np.arange(CHUNK)
    u = np.arange(CHUNK)
    rows = []
    for lvl in range(N_LEVELS):
        p = CHUNK >> (lvl + 1)
        mid = (t // (2 * p)) * (2 * p) + p - 1
        upper = t > mid
        rows.append(np.where(upper[:, None], (u[None, :] > mid[:, None]) & (u[None, :] <= t[:, None]),
                             (u[None, :] > t[:, None]) & (u[None, :] <= mid[:, None])))
    tab = np.concatenate(rows, axis=0).astype(np.float32)
    return np.concatenate([tab, tab], axis=1)


def _level_masks():
    t = np.arange(CHUNK)[:, None]
    s = np.arange(CHUNK)[None, :]
    out = []
    for lvl in range(N_LEVELS):
        p = CHUNK >> (lvl + 1)
        out.append((t // (2 * p) == s // (2 * p)) & (t % (2 * p) >= p) & (s % (2 * p) < p))
    out.append(t == s)
    return np.stack(out).astype(np.float32)


def _gated_head_norm(o, zg, g):
    outs = []
    for h in range(N_HEADS):
        oh = o[:, _head(h)]
        inv = lax.rsqrt(jnp.mean(oh * oh, axis=-1, keepdims=True) + NORM_EPS)
        outs.append(oh * inv)
    return jnp.concatenate(outs, axis=-1) * g * zg


def _zero(*refs):
    for r in refs:
        r[...] = jnp.zeros(r.shape, r.dtype)


def _gla_front(kind, x_ref, ng_ref, w_ref, aux_refs, edge_ref, slot, q_s, k_s, v_s, zg_s, gs_s, tab_s, safe_s, p_s):
    nb, _, d = x_ref.shape
    rows = nb * CHUNK
    w = BRANCH_WIDTH
    h = _rms_norm(x_ref[...].reshape(rows, d), ng_ref[...]).astype(BF16)
    yield

    def project(c0, c1):
        p_s[:, c0:c1] = _dot(h, w_ref[:, c0:c1])

    def store_decay(g):
        g_hi, g_lo = _split_bf16(g * LOG2E)
        for b in range(nb):
            r = slice(b * CHUNK, (b + 1) * CHUNK)
            gs_s[slot, b] = jnp.concatenate([g_hi[r], g_lo[r]], axis=0)

    if kind == "hgrn2":
        lb_ref, = aux_refs
        for c0 in range(0, 4 * w, w):
            project(c0, c0 + w)
            yield
        q_s[slot] = (_silu(p_s[:, 0:w]) * (HEAD_DIM ** -0.5)).astype(BF16)
        yield
        z = p_s[:, w:2 * w]
        lb = lb_ref[0:1, :]
        log_lb = lb_ref[1:2, :]
        c = lb_ref[2:3, :] + _log_sigmoid(z)
        store_decay(jnp.maximum(log_lb, c) + jnp.log1p(jnp.exp(-jnp.abs(log_lb - c))))
        k_s[slot] = ((1.0 - lb) * _sigmoid(-z)).astype(BF16)
        yield
        v_s[slot] = p_s[:, 2 * w:3 * w].astype(BF16)
        zg_s[slot] = _silu(p_s[:, 3 * w:4 * w])
        yield
    else:
        w2_ref, b2_ref = aux_refs
        project(4 * w, 4 * w + LANES)
        yield
        project(0, w)
        yield
        gk = _dot(p_s[:, 4 * w:4 * w + LANES].astype(BF16), w2_ref[...]) + b2_ref[...]
        store_decay(_log_sigmoid(gk) * (1.0 / GLA_GATE_NORMALIZER))
        yield
        project(w, 2 * w)
        yield
        project(2 * w, 3 * w)
        yield
    edge = edge_ref[...]
    low_mark = None
    for b in range(nb):
        tab = _dot(edge, gs_s[slot, b])
        tab_s[slot, b] = tab
        total = tab[CHUNK - 1:CHUNK, :]
        low_mark = total if low_mark is None else jnp.minimum(low_mark, total)
        if b % 4 == 3:
            yield
    safe_s[slot] = (jnp.min(low_mark) >= SAFE_LOG2_DECAY).astype(jnp.int32)
    if kind != "hgrn2":
        project(3 * w, 4 * w)
        yield
        q_s[slot] = (p_s[:, 0:w] * (GLA_DK ** -0.5)).astype(BF16)
        k_s[slot] = p_s[:, w:2 * w].astype(BF16)
        yield
        v_s[slot] = p_s[:, 2 * w:3 * w].astype(BF16)
        zg_s[slot] = _silu(p_s[:, 3 * w:4 * w])


def _gla_scores_centered(q, k, tab, nb):
    _, _, incl, _ = _causal_masks()
    pre = tab[:, 0:CHUNK, :]
    mid = 0.5 * pre[:, CHUNK - 1:CHUNK, :]
    rows = nb * CHUNK
    qf = q * jnp.exp2(pre - mid).astype(BF16).reshape(rows, -1)
    kf = k * jnp.exp2(mid - pre).astype(BF16).reshape(rows, -1)
    yield
    s = _bdot_nt(_to_problems(qf, nb), _to_problems(kf, nb))
    yield jnp.where(incl, s, 0.0)


def _gla_scores_levels(q, k, gs, level_ref, lmask_ref, nb):
    sc = _bdot_nt(_to_problems(q, nb), _to_problems(k, nb)) * lmask_ref[N_LEVELS][None]
    for l in range(N_LEVELS):
        wt = level_ref[l * CHUNK:(l + 1) * CHUNK, :]
        e = jnp.exp2(jnp.concatenate([_dot(wt, gs[b]) for b in range(nb)], axis=0)).astype(BF16)
        sc = sc + _bdot_nt(_to_problems(q * e, nb), _to_problems(k * e, nb)) * lmask_ref[l][None]
    return sc


def _gla_back(centered, slot, q_s, k_s, v_s, zg_s, gs_s, tab_s, st_ref, hn_ref, level_ref, lmask_ref, o_ref):
    nb = o_ref.shape[0]
    rows = nb * CHUNK
    q = q_s[slot]
    k = k_s[slot]
    tab = tab_s[slot]
    if centered:
        scores = _gla_scores_centered(q, k, tab, nb)
        yield next(scores)
        sc = next(scores)
    else:
        sc = _gla_scores_levels(q, k, [gs_s[slot, b] for b in range(nb)], level_ref, lmask_ref, nb)
    yield
    pre = tab[:, 0:CHUNK, :]
    e_in = jnp.exp2(pre).astype(BF16).reshape(rows, -1)
    e_out = jnp.exp2(tab[:, CHUNK:2 * CHUNK, :]).astype(BF16).reshape(rows, -1)
    e_last = _heads_to_problems(jnp.exp2(pre[:, CHUNK - 1:CHUNK, :]))
    v_p = _to_problems(v_s[slot], nb)
    st = st_ref[...]
    o = _bdot(sc.astype(BF16), v_p) + _bdot_nt(_to_problems(q * e_in, nb), st.astype(BF16))
    yield
    st_ref[...] = st * e_last + _bdot_tn(v_p, _to_problems(k * e_out, nb))
    yield
    y = _gated_head_norm(_from_problems(o, nb), zg_s[slot], hn_ref[...])
    o_ref[...] = y.reshape(o_ref.shape).astype(o_ref.dtype)


def _gla_kernel(kind, n_aux, x_ref, ng_ref, w_ref, *rest):
    aux_refs = rest[:n_aux]
    (hn_ref, edge_ref, level_ref, lmask_ref, o_ref,
     st_ref, q_s, k_s, v_s, zg_s, gs_s, tab_s, safe_s, p_s) = rest[n_aux:]
    j = pl.program_id(0)
    slot = j % 2
    prev = 1 - slot
    bufs = (q_s, k_s, v_s, zg_s, gs_s, tab_s)

    @pl.when(j == 0)
    def _():
        _zero(st_ref, *bufs)
        safe_s[0] = 1
        safe_s[1] = 1

    def front():
        return _gla_front(kind, x_ref, ng_ref, w_ref, aux_refs, edge_ref, slot, *bufs, safe_s, p_s)

    def back(centered):
        return _gla_back(centered, prev, *bufs, st_ref, hn_ref, level_ref, lmask_ref, o_ref)

    @pl.when(safe_s[prev] == 1)
    def _():
        _run_interleaved(front(), back(True))

    @pl.when(safe_s[prev] != 1)
    def _():
        _run_interleaved(back(False))
        _run_interleaved(front())


def _dn_front(x_ref, ng_ref, w_ref, cw_ref, ab_ref, slot, hist_s, ext_s, q_s, k_s, v_s, zg_s, bg_s):
    nb, _, d = x_ref.shape
    rows = nb * CHUNK
    w = BRANCH_WIDTH
    h = _rms_norm(x_ref[...].reshape(rows, d), ng_ref[...]).astype(BF16)
    yield
    for part, dst in enumerate((q_s, k_s, v_s)):
        cols = slice(part * w, (part + 1) * w)
        ext_s[:, 0:CONV_PAD, cols] = hist_s[:, :, cols]
        ext_s[:, CONV_PAD:CONV_PAD + CHUNK, cols] = _dot(h, w_ref[:, cols]).reshape(nb, CHUNK, w)
        hist_s[:, :, cols] = ext_s[:, CHUNK:CHUNK + CONV_PAD, cols]
        yield
        ext = ext_s[:, :, cols]
        conv = cw_ref[DN_CONV - 1:DN_CONV, cols][None] * ext[:, CONV_PAD:CONV_PAD + CHUNK]
        for back in range(1, DN_CONV):
            tap = DN_CONV - 1 - back
            shifted = pltpu.roll(ext, back, axis=1)
            conv = conv + cw_ref[tap:tap + 1, cols][None] * shifted[:, CONV_PAD:CONV_PAD + CHUNK]
        act = _silu(conv.reshape(rows, w))
        if part < 2:
            scale = HEAD_DIM ** -0.5 if part == 0 else 1.0
            heads = []
            for hd in range(N_HEADS):
                xh = act[:, _head(hd)]
                heads.append(xh * lax.rsqrt(jnp.sum(xh * xh, axis=-1, keepdims=True) + NORM_EPS) * scale)
            act = jnp.concatenate(heads, axis=-1)
        dst[slot] = _to_problems(act, nb)
        yield
    zg_s[slot] = _silu(_dot(h, w_ref[:, 3 * w:4 * w]))
    yield
    ba = _dot(h, w_ref[:, 4 * w:4 * w + LANES])
    bg_s[slot, 0] = _sigmoid(ba)
    bg_s[slot, 1] = -jnp.exp(ab_ref[0:1, :]) * _softplus(ba + ab_ref[1:2, :])


def _dn_back(slot, q_s, k_s, v_s, zg_s, bg_s, st_ref, hn_ref, tri_ref, o_ref):
    nb = o_ref.shape[0]
    q_p = q_s[slot]
    k_p = k_s[slot]
    v_p = v_s[slot]
    beta = bg_s[slot, 0]
    g_hi, g_lo = _split_bf16(bg_s[slot, 1])
    tri = tri_ref[...]
    gcols, grows = [], []
    for b in range(nb):
        r = slice(b * CHUNK, (b + 1) * CHUNK)
        gcum = _dot(tri, jnp.concatenate([g_hi[r], g_lo[r]], axis=0))
        gcols.append(gcum)
        grows.append(gcum.T)
    ln = lambda h: slice(N_HEADS + h, N_HEADS + h + 1)
    gc = jnp.concatenate([gcols[b][None, :, ln(h)] for h in range(N_HEADS) for b in range(nb)], axis=0)
    gr = jnp.concatenate([grows[b][None, ln(h), :] for h in range(N_HEADS) for b in range(nb)], axis=0)
    beta3 = beta.reshape(nb, CHUNK, LANES)
    bc = jnp.concatenate([beta3[:, :, h:h + 1] for h in range(N_HEADS)], axis=0)

    row, col, incl, strict = _causal_masks()
    decay = jnp.exp(jnp.where(incl, gc - gr, -jnp.inf))
    g_last = gc[:, CHUNK - 1:CHUNK, :]
    e_in = jnp.exp(gc)
    e_out = jnp.exp(g_last - gc)
    e_last = jnp.exp(g_last)

    kb = k_p * bc
    kq = _bdot_nt(jnp.concatenate([kb, q_p], axis=1).astype(BF16), k_p.astype(BF16))
    yield
    m = jnp.where(strict, kq[:, 0:CHUNK] * decay, 0.0)
    sc = jnp.where(incl, kq[:, CHUNK:2 * CHUNK] * decay, 0.0).astype(BF16)
    eye = (row == col).astype(F32)[None]
    t = None
    blk = 1
    while blk < CHUNK:
        sel = ((row // (2 * blk) == col // (2 * blk)) & (row % (2 * blk) >= blk) & (col % (2 * blk) < blk))[None]
        off = jnp.where(sel, m, 0.0)
        if blk == 1:
            t = eye - off
        else:
            t_bf = t.astype(BF16)
            half = _bdot(t_bf, off.astype(BF16)).astype(BF16)
            yield
            t = t - _bdot(half, t_bf)
            yield
        blk *= 2
    rhs = jnp.concatenate([v_p * bc, kb * e_in], axis=-1).astype(BF16)
    uw = _bdot(t.astype(BF16), rhs).astype(BF16)
    yield
    sc_uw = _bdot(sc, uw)
    k_uw = _bdot_tn((k_p * e_out).astype(BF16), uw)
    yield
    q_eff = (q_p * e_in - sc_uw[:, :, HEAD_DIM:]).astype(BF16)
    st = st_ref[...]
    st_bf = st.astype(BF16)
    o = _bdot(q_eff, st_bf) + sc_uw[:, :, 0:HEAD_DIM]
    st_ref[...] = st * e_last + k_uw[:, :, 0:HEAD_DIM] - _bdot(k_uw[:, :, HEAD_DIM:].astype(BF16), st_bf)
    yield
    y = _gated_head_norm(_from_problems(o, nb), zg_s[slot], hn_ref[...])
    o_ref[...] = y.reshape(o_ref.shape).astype(o_ref.dtype)


def _dn_kernel(x_ref, ng_ref, w_ref, cw_ref, ab_ref, hn_ref, tri_ref, o_ref,
               st_ref, hist_s, ext_s, q_s, k_s, v_s, zg_s, bg_s):
    j = pl.program_id(0)
    slot = j % 2
    bufs = (q_s, k_s, v_s, zg_s, bg_s)

    @pl.when(j == 0)
    def _():
        _zero(st_ref, hist_s, *bufs)

    _run_interleaved(
        _dn_front(x_ref, ng_ref, w_ref, cw_ref, ab_ref, slot, hist_s, ext_s, *bufs),
        _dn_back(1 - slot, *bufs, st_ref, hn_ref, tri_ref, o_ref))


def _merge_kernel(x_ref, ng_ref, wg_ref, gb_ref, oa_ref, ob_ref, oc_ref, wb_ref, wo_ref, y_ref):
    x = x_ref[...]
    h = _rms_norm(x, ng_ref[...]).astype(BF16)
    gates = _sigmoid(_dot(h, wg_ref[...]) + gb_ref[...])
    merged = None
    for n, o_n in enumerate((oa_ref, ob_ref, oc_ref)):
        term = gates[:, n * D_MODEL:(n + 1) * D_MODEL] * _dot(o_n[...], wb_ref[n])
        merged = term if merged is None else merged + term
    y_ref[...] = x + _dot(merged.astype(BF16), wo_ref[...])


def _ffn_kernel(x_ref, ng_ref, wu_ref, cw_ref, cb_ref, wd_ref, fg_ref, y_ref, ext_s, hist_s, act_s,
                *, final_norm):
    @pl.when(pl.program_id(1) == 0)
    def _():
        hist_s[...] = jnp.zeros(hist_s.shape, hist_s.dtype)

    tile = x_ref.shape[0]
    x = x_ref[...]
    h = _rms_norm(x, ng_ref[...]).astype(BF16)
    ext_s[0:CONV_PAD, :] = hist_s[...]
    ext_s[CONV_PAD:CONV_PAD + tile, :] = _dot(h, wu_ref[...])
    hist_s[...] = ext_s[tile:tile + CONV_PAD, :]
    blk = 2 * LANES
    for j0 in range(0, FFN_HIDDEN, blk):
        halves = []
        for base in (j0, FFN_HIDDEN + j0):
            cols = slice(base, base + blk)
            u = cb_ref[0:1, cols] + cw_ref[0:1, cols] * ext_s[CONV_PAD - 2:CONV_PAD - 2 + tile, cols]
            for j in range(1, FFN_CONV):
                u = u + cw_ref[j:j + 1, cols] * ext_s[CONV_PAD - 2 + j:CONV_PAD - 2 + j + tile, cols]
            halves.append(u)
        act_s[:, j0:j0 + blk] = (_silu(halves[0]) * halves[1]).astype(BF16)
    y = x + _dot(act_s[...], wd_ref[...])
    if final_norm:
        y = _rms_norm(y, fg_ref[...])
    y_ref[...] = y


def _const_spec(shape, grid_rank):
    nd = len(shape)
    if grid_rank == 1:
        return pl.BlockSpec(shape, lambda i, _nd=nd: (0,) * _nd, pipeline_mode=pl.Buffered(1))
    return pl.BlockSpec(shape, lambda b, i, _nd=nd: (0,) * _nd, pipeline_mode=pl.Buffered(1))


def _row_spec(tile, width, steps):
    return pl.BlockSpec((tile, width), lambda b, i, _s=steps: (b * _s + i, 0))


def _mixer_call(body, name, x3, consts, scratch):
    batch, seq, _ = x3.shape
    n = seq // CHUNK
    in_specs = ([pl.BlockSpec((batch, CHUNK, D_MODEL), lambda j: (0, jnp.minimum(j, n - 1), 0))]
                + [_const_spec(c.shape, 1) for c in consts])
    return pl.pallas_call(
        body,
        grid=(n + 1,),
        in_specs=in_specs,
        out_specs=pl.BlockSpec((batch, CHUNK, BRANCH_WIDTH), lambda j: (0, jnp.maximum(j - 1, 0), 0)),
        out_shape=jax.ShapeDtypeStruct((batch, seq, BRANCH_WIDTH), BF16),
        scratch_shapes=scratch,
        compiler_params=pltpu.CompilerParams(dimension_semantics=("arbitrary",), vmem_limit_bytes=VMEM_LIMIT),
        name=name,
    )(x3, *consts)


def _gla_scratch(batch, proj_width):
    rows = batch * CHUNK
    w = BRANCH_WIDTH
    return [pltpu.VMEM((N_HEADS * batch, HEAD_DIM, HEAD_DIM), F32),
            pltpu.VMEM((2, rows, w), BF16), pltpu.VMEM((2, rows, w), BF16), pltpu.VMEM((2, rows, w), BF16),
            pltpu.VMEM((2, rows, w), F32),
            pltpu.VMEM((2, batch, 2 * CHUNK, w), BF16),
            pltpu.VMEM((2, batch, 2 * CHUNK, w), F32),
            pltpu.SMEM((2,), jnp.int32),
            pltpu.VMEM((rows, proj_width), F32)]


def _dn_scratch(batch):
    rows = batch * CHUNK
    prob = (2, N_HEADS * batch, CHUNK, HEAD_DIM)
    return [pltpu.VMEM((N_HEADS * batch, HEAD_DIM, HEAD_DIM), F32),
            pltpu.VMEM((batch, CONV_PAD, 3 * BRANCH_WIDTH), F32),
            pltpu.VMEM((batch, CHUNK + CONV_PAD, 3 * BRANCH_WIDTH), F32),
            pltpu.VMEM(prob, F32), pltpu.VMEM(prob, F32), pltpu.VMEM(prob, F32),
            pltpu.VMEM((2, rows, BRANCH_WIDTH), F32),
            pltpu.VMEM((2, 2, rows, LANES), F32)]


def _params2():
    return pltpu.CompilerParams(dimension_semantics=("arbitrary", "arbitrary"), vmem_limit_bytes=VMEM_LIMIT)


def _merge_call(x2, batch, consts_a, branches, consts_b):
    rows = x2.shape[0]
    steps = rows // batch // TILE_MERGE
    in_specs = ([_row_spec(TILE_MERGE, D_MODEL, steps)] + [_const_spec(c.shape, 2) for c in consts_a]
                + [_row_spec(TILE_MERGE, BRANCH_WIDTH, steps) for _ in branches]
                + [_const_spec(c.shape, 2) for c in consts_b])
    return pl.pallas_call(
        _merge_kernel,
        grid=(batch, steps),
        in_specs=in_specs,
        out_specs=_row_spec(TILE_MERGE, D_MODEL, steps),
        out_shape=jax.ShapeDtypeStruct((rows, D_MODEL), F32),
        compiler_params=_params2(),
        name="merge",
    )(x2, *consts_a, *branches, *consts_b)


def _ffn_call(x2, batch, consts, final_norm):
    rows = x2.shape[0]
    steps = rows // batch // TILE_FFN
    in_specs = [_row_spec(TILE_FFN, D_MODEL, steps)] + [_const_spec(c.shape, 2) for c in consts]
    return pl.pallas_call(
        functools.partial(_ffn_kernel, final_norm=final_norm),
        grid=(batch, steps),
        in_specs=in_specs,
        out_specs=_row_spec(TILE_FFN, D_MODEL, steps),
        out_shape=jax.ShapeDtypeStruct((rows, D_MODEL), F32),
        scratch_shapes=[pltpu.VMEM((TILE_FFN + CONV_PAD, 2 * FFN_HIDDEN), F32),
                        pltpu.VMEM((CONV_PAD, 2 * FFN_HIDDEN), F32),
                        pltpu.VMEM((TILE_FFN, FFN_HIDDEN), BF16)],
        compiler_params=_params2(),
        name="ffn_final" if final_norm else "ffn",
    )(x2, *consts)


def _pad_heads(wcols, dk):
    lead = wcols.shape[:-1]
    w4 = wcols.reshape(lead + (N_HEADS, dk))
    w4 = jnp.pad(w4, [(0, 0)] * len(lead) + [(0, 0), (0, HEAD_DIM - dk)])
    return w4.reshape(lead + (N_HEADS * HEAD_DIM,))


def _pad_cols(wcols, width):
    return jnp.pad(wcols, [(0, 0)] * (wcols.ndim - 1) + [(0, width - wcols.shape[-1])])


def _row(v):
    return v.reshape(1, -1).astype(F32)


def kernel(x, norm_mix_g, w_in, hg_lower_bounds, hg_norm_g, gla_gk_w2, gla_gk_b, gla_norm_g, dn_conv_w,
           dn_A_log, dn_dt_bias, dn_norm_g, w_branch, gate_b, w_out, norm_ffn_g, w_up, ffn_conv_w,
           ffn_conv_b, w_down, norm_final_g):
    batch, seq, d = x.shape
    depth = w_in.shape[0]
    assert d == D_MODEL and seq % max(CHUNK, TILE_MERGE, TILE_FFN) == 0

    lb_all = jnp.cumsum(jax.nn.softmax(hg_lower_bounds.astype(F32), axis=0), axis=0)
    lb_all = lb_all - lb_all[:1]
    edge = jnp.asarray(_edge_table(), BF16)
    level = jnp.asarray(_level_table(), BF16)
    lmask = jnp.asarray(_level_masks(), F32)
    tri1 = np.tril(np.ones((CHUNK, CHUNK), np.float32))
    tri = jnp.asarray(np.concatenate([tri1, tri1], axis=1), BF16)

    hw = BRANCH_WIDTH
    c0 = 4 * hw
    c1 = c0 + 2 * N_HEADS * GLA_DK
    c2 = c1 + hw
    c3 = c2 + GLA_RANK
    c4 = c3 + hw
    c5 = c4 + 3 * hw
    c6 = c5 + hw
    c7 = c6 + N_HEADS
    c8 = c7 + N_HEADS

    x3 = x
    for l in range(depth):
        wl = w_in[l]
        ng = _row(norm_mix_g[l])
        lb = lb_all[l]
        lb_rows = jnp.stack([lb, jnp.log(lb), jnp.log1p(-lb)]).astype(F32)
        o_hg = _mixer_call(
            functools.partial(_gla_kernel, "hgrn2", 1), "hgrn2", x3,
            [ng, wl[:, :c0].astype(BF16), lb_rows, _row(jnp.tile(hg_norm_g[l], N_HEADS)), edge, level, lmask],
            _gla_scratch(batch, c0))

        w_gla = jnp.concatenate([
            _pad_heads(wl[:, c0:c0 + N_HEADS * GLA_DK], GLA_DK),
            _pad_heads(wl[:, c0 + N_HEADS * GLA_DK:c1], GLA_DK),
            wl[:, c1:c2], wl[:, c3:c4], _pad_cols(wl[:, c2:c3], LANES)], axis=1).astype(BF16)
        w2 = jnp.pad(_pad_heads(gla_gk_w2[l], GLA_DK), ((0, LANES - GLA_RANK), (0, 0))).astype(BF16)
        b2 = _row(_pad_heads(gla_gk_b[l], GLA_DK))
        o_gla = _mixer_call(
            functools.partial(_gla_kernel, "gla", 2), "gla", x3,
            [ng, w_gla, w2, b2, _row(jnp.tile(gla_norm_g[l], N_HEADS)), edge, level, lmask],
            _gla_scratch(batch, w_gla.shape[1]))

        w_dn = jnp.concatenate([wl[:, c4:c6], _pad_cols(wl[:, c6:c8], LANES)], axis=1).astype(BF16)
        ab = jnp.zeros((2, LANES), F32)
        ab = ab.at[0, N_HEADS:2 * N_HEADS].set(dn_A_log[l].astype(F32))
        ab = ab.at[1, N_HEADS:2 * N_HEADS].set(dn_dt_bias[l].astype(F32))
        o_dn = _mixer_call(
            _dn_kernel, "deltanet", x3,
            [ng, w_dn, dn_conv_w[l].astype(F32), ab, _row(jnp.tile(dn_norm_g[l], N_HEADS)), tri],
            _dn_scratch(batch))

        x2 = _merge_call(
            x3.reshape(batch * seq, d), batch, [ng, wl[:, c8:].astype(BF16), _row(gate_b[l])],
            [o.reshape(batch * seq, hw) for o in (o_hg, o_gla, o_dn)],
            [w_branch[l].astype(BF16), w_out[l].astype(BF16)])

        x2 = _ffn_call(
            x2, batch,
            [_row(norm_ffn_g[l]), w_up[l].astype(BF16), ffn_conv_w[l].astype(F32), _row(ffn_conv_b[l]),
             w_down[l].astype(BF16), _row(norm_final_g)],
            final_norm=(l == depth - 1))
        x3 = x2.reshape(batch, seq, d)
    return x3
```

```python
import functools

import jax
import jax.numpy as jnp
import numpy as np
from jax import lax
from jax.experimental import pallas as pl
from jax.experimental.pallas import tpu as pltpu

F32 = jnp.float32
BF16 = jnp.bfloat16

D_MODEL = 1024
CHUNK = 64
NORM_EPS = 1e-6
N_HEADS = 4
HEAD_DIM = 128
BRANCH_WIDTH = N_HEADS * HEAD_DIM
GLA_DK = 64
GLA_RANK = 16
GLA_GATE_NORMALIZER = 16.0
DN_CONV = 4
N_BRANCH = 3
FFN_HIDDEN = 2816
FFN_CONV = 3
LANES = 128
CONV_PAD = 8
N_LEVELS = 6
LOG2E = 1.4426950408889634
SAFE_LOG2_DECAY = -200.0

TILE_MERGE = 512
TILE_FFN = 512
VMEM_LIMIT = 56 * 1024 * 1024


def _sigmoid(x):
    return 1.0 / (1.0 + jnp.exp(-x))


def _silu(x):
    return x * _sigmoid(x)


def _softplus(x):
    return jnp.maximum(x, 0.0) + jnp.log1p(jnp.exp(-jnp.abs(x)))


def _log_sigmoid(x):
    return -_softplus(-x)


def _rms_norm(x, g):
    y = x * lax.rsqrt(jnp.mean(x * x, axis=-1, keepdims=True) + NORM_EPS)
    return y * g


def _dot(a, b):
    return jnp.dot(a, b, preferred_element_type=F32)


def _bdot(a, b):
    return lax.dot_general(a, b, (((2,), (1,)), ((0,), (0,))), preferred_element_type=F32)


def _bdot_nt(a, b):
    return lax.dot_general(a, b, (((2,), (2,)), ((0,), (0,))), preferred_element_type=F32)


def _bdot_tn(a, b):
    return lax.dot_general(a, b, (((1,), (1,)), ((0,), (0,))), preferred_element_type=F32)


def _split_bf16(x):
    hi = x.astype(BF16)
    lo = (x - hi.astype(F32)).astype(BF16)
    return hi, lo


def _head(h):
    return slice(h * HEAD_DIM, (h + 1) * HEAD_DIM)


def _heads_to_problems(x3):
    return jnp.concatenate([x3[:, :, _head(h)] for h in range(N_HEADS)], axis=0)


def _to_problems(x2d, nb):
    return _heads_to_problems(x2d.reshape(nb, x2d.shape[0] // nb, x2d.shape[1]))


def _from_problems(xp, nb):
    rows = nb * xp.shape[1]
    return jnp.concatenate([xp[h * nb:(h + 1) * nb].reshape(rows, HEAD_DIM) for h in range(N_HEADS)], axis=-1)


def _run_interleaved(*stage_generators):
    live = list(stage_generators)
    while live:
        live = [s for s in live if next(s, True) is None]


def _causal_masks():
    row = lax.broadcasted_iota(jnp.int32, (CHUNK, CHUNK), 0)
    col = lax.broadcasted_iota(jnp.int32, (CHUNK, CHUNK), 1)
    return row, col, (row >= col)[None], (row > col)[None]


def _prefix_suffix_rows():
    t = np.arange(CHUNK)[:, None]
    u = np.arange(CHUNK)[None, :]
    return [u <= t, u > t]


def _edge_table():
    tab = np.concatenate(_prefix_suffix_rows(), axis=0).astype(np.float32)
    return np.concatenate([tab, tab], axis=1)


def _level_table():
    t = np.arange(CHUNK)
    u = np.arange(CHUNK)
    rows = []
    for lvl in range(N_LEVELS):
        p = CHUNK >> (lvl + 1)
        mid = (t // (2 * p)) * (2 * p) + p - 1
        upper = t > mid
        rows.append(np.where(upper[:, None], (u[None, :] > mid[:, None]) & (u[None, :] <= t[:, None]),
                             (u[None, :] > t[:, None]) & (u[None, :] <= mid[:, None])))
    tab = np.concatenate(rows, axis=0).astype(np.float32)
    return np.concatenate([tab, tab], axis=1)


def _level_masks():
    t = np.arange(CHUNK)[:, None]
    s = np.arange(CHUNK)[None, :]
    out = []
    for lvl in range(N_LEVELS):
        p = CHUNK >> (lvl + 1)
        out.append((t // (2 * p) == s // (2 * p)) & (t % (2 * p) >= p) & (s % (2 * p) < p))
    out.append(t == s)
    return np.stack(out).astype(np.float32)


def _gated_head_norm(o, zg, g):
    outs = []
    for h in range(N_HEADS):
        oh = o[:, _head(h)]
        inv = lax.rsqrt(jnp.mean(oh * oh, axis=-1, keepdims=True) + NORM_EPS)
        outs.append(oh * inv)
    return jnp.concatenate(outs, axis=-1) * g * zg


def _zero(*refs):
    for r in refs:
        r[...] = jnp.zeros(r.shape, r.dtype)


def _gla_front(kind, x_ref, ng_ref, w_ref, aux_refs, edge_ref, slot, q_s, k_s, v_s, zg_s, gs_s, tab_s, safe_s, p_s):
    nb, _, d = x_ref.shape
    rows = nb * CHUNK
    w = BRANCH_WIDTH
    h = _rms_norm(x_ref[...].reshape(rows, d), ng_ref[...]).astype(BF16)
    yield

    def project(c0, c1):
        p_s[:, c0:c1] = _dot(h, w_ref[:, c0:c1])

    def store_decay(g):
        g_hi, g_lo = _split_bf16(g * LOG2E)
        for b in range(nb):
            r = slice(b * CHUNK, (b + 1) * CHUNK)
            gs_s[slot, b] = jnp.concatenate([g_hi[r], g_lo[r]], axis=0)

    if kind == "hgrn2":
        lb_ref, = aux_refs
        for c0 in range(0, 4 * w, w):
            project(c0, c0 + w)
            yield
        q_s[slot] = (_silu(p_s[:, 0:w]) * (HEAD_DIM ** -0.5)).astype(BF16)
        yield
        z = p_s[:, w:2 * w]
        lb = lb_ref[0:1, :]
        log_lb = lb_ref[1:2, :]
        c = lb_ref[2:3, :] + _log_sigmoid(z)
        store_decay(jnp.maximum(log_lb, c) + jnp.log1p(jnp.exp(-jnp.abs(log_lb - c))))
        k_s[slot] = ((1.0 - lb) * _sigmoid(-z)).astype(BF16)
        yield
        v_s[slot] = p_s[:, 2 * w:3 * w].astype(BF16)
        zg_s[slot] = _silu(p_s[:, 3 * w:4 * w])
        yield
    else:
        w2_ref, b2_ref = aux_refs
        project(4 * w, 4 * w + LANES)
        yield
        project(0, w)
        yield
        gk = _dot(p_s[:, 4 * w:4 * w + LANES].astype(BF16), w2_ref[...]) + b2_ref[...]
        store_decay(_log_sigmoid(gk) * (1.0 / GLA_GATE_NORMALIZER))
        yield
        project(w, 2 * w)
        yield
        project(2 * w, 3 * w)
        yield
    edge = edge_ref[...]
    low_mark = None
    for b in range(nb):
        tab = _dot(edge, gs_s[slot, b])
        tab_s[slot, b] = tab
        total = tab[CHUNK - 1:CHUNK, :]
        low_mark = total if low_mark is None else jnp.minimum(low_mark, total)
        if b % 4 == 3:
            yield
    safe_s[slot] = (jnp.min(low_mark) >= SAFE_LOG2_DECAY).astype(jnp.int32)
    if kind != "hgrn2":
        project(3 * w, 4 * w)
        yield
        q_s[slot] = (p_s[:, 0:w] * (GLA_DK ** -0.5)).astype(BF16)
        k_s[slot] = p_s[:, w:2 * w].astype(BF16)
        yield
        v_s[slot] = p_s[:, 2 * w:3 * w].astype(BF16)
        zg_s[slot] = _silu(p_s[:, 3 * w:4 * w])


def _gla_scores_centered(q, k, tab, nb):
    _, _, incl, _ = _causal_masks()
    pre = tab[:, 0:CHUNK, :]
    mid = 0.5 * pre[:, CHUNK - 1:CHUNK, :]
    rows = nb * CHUNK
    qf = q * jnp.exp2(pre - mid).astype(BF16).reshape(rows, -1)
    kf = k * jnp.exp2(mid - pre).astype(BF16).reshape(rows, -1)
    yield
    s = _bdot_nt(_to_problems(qf, nb), _to_problems(kf, nb))
    yield jnp.where(incl, s, 0.0)


def _gla_scores_levels(q, k, gs, level_ref, lmask_ref, nb):
    sc = _bdot_nt(_to_problems(q, nb), _to_problems(k, nb)) * lmask_ref[N_LEVELS][None]
    for l in range(N_LEVELS):
        wt = level_ref[l * CHUNK:(l + 1) * CHUNK, :]
        e = jnp.exp2(jnp.concatenate([_dot(wt, gs[b]) for b in range(nb)], axis=0)).astype(BF16)
        sc = sc + _bdot_nt(_to_problems(q * e, nb), _to_problems(k * e, nb)) * lmask_ref[l][None]
    return sc


def _gla_back(centered, slot, q_s, k_s, v_s, zg_s, gs_s, tab_s, st_ref, hn_ref, level_ref, lmask_ref, o_ref):
    nb = o_ref.shape[0]
    rows = nb * CHUNK
    q = q_s[slot]
    k = k_s[slot]
    tab = tab_s[slot]
    if centered:
        scores = _gla_scores_centered(q, k, tab, nb)
        yield next(scores)
        sc = next(scores)
    else:
        sc = _gla_scores_levels(q, k, [gs_s[slot, b] for b in range(nb)], level_ref, lmask_ref, nb)
    yield
    pre = tab[:, 0:CHUNK, :]
    e_in = jnp.exp2(pre).astype(BF16).reshape(rows, -1)
    e_out = jnp.exp2(tab[:, CHUNK:2 * CHUNK, :]).astype(BF16).reshape(rows, -1)
    e_last = _heads_to_problems(jnp.exp2(pre[:, CHUNK - 1:CHUNK, :]))
    v_p = _to_problems(v_s[slot], nb)
    st = st_ref[...]
    o = _bdot(sc.astype(BF16), v_p) + _bdot_nt(_to_problems(q * e_in, nb), st.astype(BF16))
    yield
    st_ref[...] = st * e_last + _bdot_tn(v_p, _to_problems(k * e_out, nb))
    yield
    y = _gated_head_norm(_from_problems(o, nb), zg_s[slot], hn_ref[...])
    o_ref[...] = y.reshape(o_ref.shape).astype(o_ref.dtype)


def _gla_kernel(kind, n_aux, x_ref, ng_ref, w_ref, *rest):
    aux_refs = rest[:n_aux]
    (hn_ref, edge_ref, level_ref, lmask_ref, o_ref,
     st_ref, q_s, k_s, v_s, zg_s, gs_s, tab_s, safe_s, p_s) = rest[n_aux:]
    j = pl.program_id(0)
    slot = j % 2
    prev = 1 - slot
    bufs = (q_s, k_s, v_s, zg_s, gs_s, tab_s)

    @pl.when(j == 0)
    def _():
        _zero(st_ref, *bufs)
        safe_s[0] = 1
        safe_s[1] = 1

    def front():
        return _gla_front(kind, x_ref, ng_ref, w_ref, aux_refs, edge_ref, slot, *bufs, safe_s, p_s)

    def back(centered):
        return _gla_back(centered, prev, *bufs, st_ref, hn_ref, level_ref, lmask_ref, o_ref)

    @pl.when(safe_s[prev] == 1)
    def _():
        _run_interleaved(front(), back(True))

    @pl.when(safe_s[prev] != 1)
    def _():
        _run_interleaved(back(False))
        _run_interleaved(front())


def _dn_front(x_ref, ng_ref, w_ref, cw_ref, ab_ref, slot, ext_s, act_s, q_s, k_s, v_s, zg_s, bg_s):
    nb, _, d = x_ref.shape
    rows = nb * CHUNK
    w = BRANCH_WIDTH
    h = _rms_norm(x_ref[...].reshape(rows, d), ng_ref[...]).astype(BF16)
    yield
    lead = (DN_CONV - 1) * nb
    for part, dst in enumerate((q_s, k_s, v_s)):
        p = _dot(h, w_ref[:, part * w:(part + 1) * w])
        for hd in range(N_HEADS):
            slab = part * N_HEADS + hd
            ext_s[slab, 0:lead, :] = ext_s[slab, rows:rows + lead, :]
            for b in range(nb):
                ext_s[slab, pl.ds(lead + b, CHUNK, stride=nb), :] = p[b * CHUNK:(b + 1) * CHUNK, _head(hd)]
        yield
        for hd in range(N_HEADS):
            slab = part * N_HEADS + hd
            lanes = slice(slab * HEAD_DIM, (slab + 1) * HEAD_DIM)
            conv = cw_ref[0:1, lanes] * ext_s[slab, 0:rows, :]
            for tap in range(1, DN_CONV):
                conv = conv + cw_ref[tap:tap + 1, lanes] * ext_s[slab, tap * nb:tap * nb + rows, :]
            act = _silu(conv)
            if part < 2:
                scale = HEAD_DIM ** -0.5 if part == 0 else 1.0
                act = act * lax.rsqrt(jnp.sum(act * act, axis=-1, keepdims=True) + NORM_EPS) * scale
            act_s[hd] = act
            for b in range(nb):
                dst[slot, hd * nb + b] = act_s[hd, pl.ds(b, CHUNK, stride=nb), :]
        yield
    zg_s[slot] = _silu(_dot(h, w_ref[:, 3 * w:4 * w]))
    yield
    ba = _dot(h, w_ref[:, 4 * w:4 * w + LANES])
    bg_s[slot, 0] = _sigmoid(ba)
    bg_s[slot, 1] = -jnp.exp(ab_ref[0:1, :]) * _softplus(ba + ab_ref[1:2, :])


def _dn_back(slot, q_s, k_s, v_s, zg_s, bg_s, st_ref, hn_ref, tri_ref, o_ref):
    nb = o_ref.shape[0]
    q_p = q_s[slot]
    k_p = k_s[slot]
    v_p = v_s[slot]
    beta = bg_s[slot, 0]
    g_hi, g_lo = _split_bf16(bg_s[slot, 1])
    tri = tri_ref[...]
    gcols, grows = [], []
    for b in range(nb):
        r = slice(b * CHUNK, (b + 1) * CHUNK)
        gcum = _dot(tri, jnp.concatenate([g_hi[r], g_lo[r]], axis=0))
        gcols.append(gcum)
        grows.append(gcum.T)
    ln = lambda h: slice(N_HEADS + h, N_HEADS + h + 1)
    gc = jnp.concatenate([gcols[b][None, :, ln(h)] for h in range(N_HEADS) for b in range(nb)], axis=0)
    gr = jnp.concatenate([grows[b][None, ln(h), :] for h in range(N_HEADS) for b in range(nb)], axis=0)
    beta3 = beta.reshape(nb, CHUNK, LANES)
    bc = jnp.concatenate([beta3[:, :, h:h + 1] for h in range(N_HEADS)], axis=0)

    row, col, incl, strict = _causal_masks()
    decay = jnp.exp(jnp.where(incl, gc - gr, -jnp.inf))
    g_last = gc[:, CHUNK - 1:CHUNK, :]
    e_in = jnp.exp(gc)
    e_out = jnp.exp(g_last - gc)
    e_last = jnp.exp(g_last)

    kb = k_p * bc
    kq = _bdot_nt(jnp.concatenate([kb, q_p], axis=1).astype(BF16), k_p.astype(BF16))
    yield
    m = jnp.where(strict, kq[:, 0:CHUNK] * decay, 0.0)
    sc = jnp.where(incl, kq[:, CHUNK:2 * CHUNK] * decay, 0.0).astype(BF16)
    eye = (row == col).astype(F32)[None]
    t = None
    blk = 1
    while blk < CHUNK:
        sel = ((row // (2 * blk) == col // (2 * blk)) & (row % (2 * blk) >= blk) & (col % (2 * blk) < blk))[None]
        off = jnp.where(sel, m, 0.0)
        if blk == 1:
            t = eye - off
        else:
            t_bf = t.astype(BF16)
            half = _bdot(t_bf, off.astype(BF16)).astype(BF16)
            yield
            t = t - _bdot(half, t_bf)
            yield
        blk *= 2
    rhs = jnp.concatenate([v_p * bc, kb * e_in], axis=-1).astype(BF16)
    uw = _bdot(t.astype(BF16), rhs).astype(BF16)
    yield
    sc_uw = _bdot(sc, uw)
    k_uw = _bdot_tn((k_p * e_out).astype(BF16), uw)
    yield
    q_eff = (q_p * e_in - sc_uw[:, :, HEAD_DIM:]).astype(BF16)
    st = st_ref[...]
    st_bf = st.astype(BF16)
    o = _bdot(q_eff, st_bf) + sc_uw[:, :, 0:HEAD_DIM]
    st_ref[...] = st * e_last + k_uw[:, :, 0:HEAD_DIM] - _bdot(k_uw[:, :, HEAD_DIM:].astype(BF16), st_bf)
    yield
    y = _gated_head_norm(_from_problems(o, nb), zg_s[slot], hn_ref[...])
    o_ref[...] = y.reshape(o_ref.shape).astype(o_ref.dtype)


def _dn_kernel(x_ref, ng_ref, w_ref, cw_ref, ab_ref, hn_ref, tri_ref, o_ref,
               st_ref, ext_s, act_s, q_s, k_s, v_s, zg_s, bg_s):
    j = pl.program_id(0)
    slot = j % 2
    bufs = (q_s, k_s, v_s, zg_s, bg_s)

    @pl.when(j == 0)
    def _():
        _zero(st_ref, ext_s, *bufs)

    _run_interleaved(
        _dn_front(x_ref, ng_ref, w_ref, cw_ref, ab_ref, slot, ext_s, act_s, *bufs),
        _dn_back(1 - slot, *bufs, st_ref, hn_ref, tri_ref, o_ref))


def _merge_kernel(x_ref, ng_ref, wg_ref, gb_ref, oa_ref, ob_ref, oc_ref, wb_ref, wo_ref, y_ref):
    x = x_ref[...]
    h = _rms_norm(x, ng_ref[...]).astype(BF16)
    gates = _sigmoid(_dot(h, wg_ref[...]) + gb_ref[...])
    merged = None
    for n, o_n in enumerate((oa_ref, ob_ref, oc_ref)):
        term = gates[:, n * D_MODEL:(n + 1) * D_MODEL] * _dot(o_n[...], wb_ref[n])
        merged = term if merged is None else merged + term
    y_ref[...] = x + _dot(merged.astype(BF16), wo_ref[...])


def _ffn_kernel(x_ref, ng_ref, wu_ref, cw_ref, cb_ref, wd_ref, fg_ref, y_ref, ext_s, hist_s, act_s,
                *, final_norm):
    @pl.when(pl.program_id(1) == 0)
    def _():
        hist_s[...] = jnp.zeros(hist_s.shape, hist_s.dtype)

    tile = x_ref.shape[0]
    x = x_ref[...]
    h = _rms_norm(x, ng_ref[...]).astype(BF16)
    ext_s[0:CONV_PAD, :] = hist_s[...]
    ext_s[CONV_PAD:CONV_PAD + tile, :] = _dot(h, wu_ref[...])
    hist_s[...] = ext_s[tile:tile + CONV_PAD, :]
    blk = 2 * LANES
    split = (FFN_HIDDEN // blk // 2 + 1) * blk
    y = x
    for j0 in range(0, FFN_HIDDEN, blk):
        halves = []
        for base in (j0, FFN_HIDDEN + j0):
            cols = slice(base, base + blk)
            u = cb_ref[0:1, cols] + cw_ref[0:1, cols] * ext_s[CONV_PAD - 2:CONV_PAD - 2 + tile, cols]
            for j in range(1, FFN_CONV):
                u = u + cw_ref[j:j + 1, cols] * ext_s[CONV_PAD - 2 + j:CONV_PAD - 2 + j + tile, cols]
            halves.append(u)
        act_s[:, j0:j0 + blk] = (_silu(halves[0]) * halves[1]).astype(BF16)
        if j0 + blk == split:
            y = y + _dot(act_s[:, 0:split], wd_ref[0:split, :])
    y = y + _dot(act_s[:, split:], wd_ref[split:, :])
    if final_norm:
        y = _rms_norm(y, fg_ref[...])
    y_ref[...] = y


def _const_spec(shape, grid_rank):
    nd = len(shape)
    if grid_rank == 1:
        return pl.BlockSpec(shape, lambda i, _nd=nd: (0,) * _nd, pipeline_mode=pl.Buffered(1))
    return pl.BlockSpec(shape, lambda b, i, _nd=nd: (0,) * _nd, pipeline_mode=pl.Buffered(1))


def _row_spec(tile, width, steps):
    return pl.BlockSpec((tile, width), lambda b, i, _s=steps: (b * _s + i, 0))


def _mixer_call(body, name, x3, consts, scratch):
    batch, seq, _ = x3.shape
    n = seq // CHUNK
    in_specs = ([pl.BlockSpec((batch, CHUNK, D_MODEL), lambda j: (0, jnp.minimum(j, n - 1), 0))]
                + [_const_spec(c.shape, 1) for c in consts])
    return pl.pallas_call(
        body,
        grid=(n + 1,),
        in_specs=in_specs,
        out_specs=pl.BlockSpec((batch, CHUNK, BRANCH_WIDTH), lambda j: (0, jnp.maximum(j - 1, 0), 0)),
        out_shape=jax.ShapeDtypeStruct((batch, seq, BRANCH_WIDTH), BF16),
        scratch_shapes=scratch,
        compiler_params=pltpu.CompilerParams(dimension_semantics=("arbitrary",), vmem_limit_bytes=VMEM_LIMIT),
        name=name,
    )(x3, *consts)


def _gla_scratch(batch, proj_width):
    rows = batch * CHUNK
    w = BRANCH_WIDTH
    return [pltpu.VMEM((N_HEADS * batch, HEAD_DIM, HEAD_DIM), F32),
            pltpu.VMEM((2, rows, w), BF16), pltpu.VMEM((2, rows, w), BF16), pltpu.VMEM((2, rows, w), BF16),
            pltpu.VMEM((2, rows, w), F32),
            pltpu.VMEM((2, batch, 2 * CHUNK, w), BF16),
            pltpu.VMEM((2, batch, 2 * CHUNK, w), F32),
            pltpu.SMEM((2,), jnp.int32),
            pltpu.VMEM((rows, proj_width), F32)]


def _dn_scratch(batch):
    rows = batch * CHUNK
    prob = (2, N_HEADS * batch, CHUNK, HEAD_DIM)
    return [pltpu.VMEM((N_HEADS * batch, HEAD_DIM, HEAD_DIM), F32),
            pltpu.VMEM((3 * N_HEADS, (DN_CONV - 1) * batch + rows, HEAD_DIM), F32),
            pltpu.VMEM((N_HEADS, rows, HEAD_DIM), F32),
            pltpu.VMEM(prob, F32), pltpu.VMEM(prob, F32), pltpu.VMEM(prob, F32),
            pltpu.VMEM((2, rows, BRANCH_WIDTH), F32),
            pltpu.VMEM((2, 2, rows, LANES), F32)]


def _params2():
    return pltpu.CompilerParams(dimension_semantics=("arbitrary", "arbitrary"), vmem_limit_bytes=VMEM_LIMIT)


def _merge_call(x2, batch, consts_a, branches, consts_b):
    rows = x2.shape[0]
    steps = rows // batch // TILE_MERGE
    in_specs = ([_row_spec(TILE_MERGE, D_MODEL, steps)] + [_const_spec(c.shape, 2) for c in consts_a]
                + [_row_spec(TILE_MERGE, BRANCH_WIDTH, steps) for _ in branches]
                + [_const_spec(c.shape, 2) for c in consts_b])
    return pl.pallas_call(
        _merge_kernel,
        grid=(batch, steps),
        in_specs=in_specs,
        out_specs=_row_spec(TILE_MERGE, D_MODEL, steps),
        out_shape=jax.ShapeDtypeStruct((rows, D_MODEL), F32),
        compiler_params=_params2(),
        name="merge",
    )(x2, *consts_a, *branches, *consts_b)


def _ffn_call(x2, batch, consts, final_norm):
    rows = x2.shape[0]
    steps = rows // batch // TILE_FFN
    in_specs = [_row_spec(TILE_FFN, D_MODEL, steps)] + [_const_spec(c.shape, 2) for c in consts]
    return pl.pallas_call(
        functools.partial(_ffn_kernel, final_norm=final_norm),
        grid=(batch, steps),
        in_specs=in_specs,
        out_specs=_row_spec(TILE_FFN, D_MODEL, steps),
        out_shape=jax.ShapeDtypeStruct((rows, D_MODEL), F32),
        scratch_shapes=[pltpu.VMEM((TILE_FFN + CONV_PAD, 2 * FFN_HIDDEN), F32),
                        pltpu.VMEM((CONV_PAD, 2 * FFN_HIDDEN), F32),
                        pltpu.VMEM((TILE_FFN, FFN_HIDDEN), BF16)],
        compiler_params=_params2(),
        name="ffn_final" if final_norm else "ffn",
    )(x2, *consts)


def _pad_heads(wcols, dk):
    lead = wcols.shape[:-1]
    w4 = wcols.reshape(lead + (N_HEADS, dk))
    w4 = jnp.pad(w4, [(0, 0)] * len(lead) + [(0, 0), (0, HEAD_DIM - dk)])
    return w4.reshape(lead + (N_HEADS * HEAD_DIM,))


def _pad_cols(wcols, width):
    return jnp.pad(wcols, [(0, 0)] * (wcols.ndim - 1) + [(0, width - wcols.shape[-1])])


def _row(v):
    return v.reshape(1, -1).astype(F32)


def kernel(x, norm_mix_g, w_in, hg_lower_bounds, hg_norm_g, gla_gk_w2, gla_gk_b, gla_norm_g, dn_conv_w,
           dn_A_log, dn_dt_bias, dn_norm_g, w_branch, gate_b, w_out, norm_ffn_g, w_up, ffn_conv_w,
           ffn_conv_b, w_down, norm_final_g):
    batch, seq, d = x.shape
    depth = w_in.shape[0]
    assert d == D_MODEL and seq % max(CHUNK, TILE_MERGE, TILE_FFN) == 0

    lb_all = jnp.cumsum(jax.nn.softmax(hg_lower_bounds.astype(F32), axis=0), axis=0)
    lb_all = lb_all - lb_all[:1]
    edge = jnp.asarray(_edge_table(), BF16)
    level = jnp.asarray(_level_table(), BF16)
    lmask = jnp.asarray(_level_masks(), F32)
    tri1 = np.tril(np.ones((CHUNK, CHUNK), np.float32))
    tri = jnp.asarray(np.concatenate([tri1, tri1], axis=1), BF16)

    hw = BRANCH_WIDTH
    c0 = 4 * hw
    c1 = c0 + 2 * N_HEADS * GLA_DK
    c2 = c1 + hw
    c3 = c2 + GLA_RANK
    c4 = c3 + hw
    c5 = c4 + 3 * hw
    c6 = c5 + hw
    c7 = c6 + N_HEADS
    c8 = c7 + N_HEADS

    x3 = x
    for l in range(depth):
        wl = w_in[l]
        ng = _row(norm_mix_g[l])
        lb = lb_all[l]
        lb_rows = jnp.stack([lb, jnp.log(lb), jnp.log1p(-lb)]).astype(F32)
        o_hg = _mixer_call(
            functools.partial(_gla_kernel, "hgrn2", 1), "hgrn2", x3,
            [ng, wl[:, :c0].astype(BF16), lb_rows, _row(jnp.tile(hg_norm_g[l], N_HEADS)), edge, level, lmask],
            _gla_scratch(batch, c0))

        w_gla = jnp.concatenate([
            _pad_heads(wl[:, c0:c0 + N_HEADS * GLA_DK], GLA_DK),
            _pad_heads(wl[:, c0 + N_HEADS * GLA_DK:c1], GLA_DK),
            wl[:, c1:c2], wl[:, c3:c4], _pad_cols(wl[:, c2:c3], LANES)], axis=1).astype(BF16)
        w2 = jnp.pad(_pad_heads(gla_gk_w2[l], GLA_DK), ((0, LANES - GLA_RANK), (0, 0))).astype(BF16)
        b2 = _row(_pad_heads(gla_gk_b[l], GLA_DK))
        o_gla = _mixer_call(
            functools.partial(_gla_kernel, "gla", 2), "gla", x3,
            [ng, w_gla, w2, b2, _row(jnp.tile(gla_norm_g[l], N_HEADS)), edge, level, lmask],
            _gla_scratch(batch, w_gla.shape[1]))

        w_dn = jnp.concatenate([wl[:, c4:c6], _pad_cols(wl[:, c6:c8], LANES)], axis=1).astype(BF16)
        ab = jnp.zeros((2, LANES), F32)
        ab = ab.at[0, N_HEADS:2 * N_HEADS].set(dn_A_log[l].astype(F32))
        ab = ab.at[1, N_HEADS:2 * N_HEADS].set(dn_dt_bias[l].astype(F32))
        o_dn = _mixer_call(
            _dn_kernel, "deltanet", x3,
            [ng, w_dn, dn_conv_w[l].astype(F32), ab, _row(jnp.tile(dn_norm_g[l], N_HEADS)), tri],
            _dn_scratch(batch))

        x2 = _merge_call(
            x3.reshape(batch * seq, d), batch, [ng, wl[:, c8:].astype(BF16), _row(gate_b[l])],
            [o.reshape(batch * seq, hw) for o in (o_hg, o_gla, o_dn)],
            [w_branch[l].astype(BF16), w_out[l].astype(BF16)])

        x2 = _ffn_call(
            x2, batch,
            [_row(norm_ffn_g[l]), w_up[l].astype(BF16), ffn_conv_w[l].astype(F32), _row(ffn_conv_b[l]),
             w_down[l].astype(BF16), _row(norm_final_g)],
            final_norm=(l == depth - 1))
        x3 = x2.reshape(batch, seq, d)
    return x3
```

```python
import functools

import jax
import jax.numpy as jnp
import numpy as np
from jax import lax
from jax.experimental import pallas as pl
from jax.experimental.pallas import tpu as pltpu

F32 = jnp.float32
BF16 = jnp.bfloat16

D_MODEL = 1024
CHUNK = 64
NORM_EPS = 1e-6
N_HEADS = 4
HEAD_DIM = 128
BRANCH_WIDTH = N_HEADS * HEAD_DIM
GLA_DK = 64
GLA_RANK = 16
GLA_GATE_NORMALIZER = 16.0
DN_CONV = 4
N_BRANCH = 3
FFN_HIDDEN = 2816
FFN_CONV = 3
LANES = 128
CONV_PAD = 8
N_LEVELS = 6
LOG2E = 1.4426950408889634
SAFE_LOG2_DECAY = -200.0

TILE_MERGE = 512
TILE_FFN = 512
VMEM_LIMIT = 56 * 1024 * 1024


def _sigmoid(x):
    return 1.0 / (1.0 + jnp.exp(-x))


def _silu(x):
    return x * _sigmoid(x)


def _softplus(x):
    return jnp.maximum(x, 0.0) + jnp.log1p(jnp.exp(-jnp.abs(x)))


def _log_sigmoid(x):
    return -_softplus(-x)


def _rms_norm(x, g):
    y = x * lax.rsqrt(jnp.mean(x * x, axis=-1, keepdims=True) + NORM_EPS)
    return y * g


def _dot(a, b):
    return jnp.dot(a, b, preferred_element_type=F32)


def _bdot(a, b):
    return lax.dot_general(a, b, (((2,), (1,)), ((0,), (0,))), preferred_element_type=F32)


def _bdot_nt(a, b):
    return lax.dot_general(a, b, (((2,), (2,)), ((0,), (0,))), preferred_element_type=F32)


def _bdot_tn(a, b):
    return lax.dot_general(a, b, (((1,), (1,)), ((0,), (0,))), preferred_element_type=F32)


def _split_bf16(x):
    hi = x.astype(BF16)
    lo = (x - hi.astype(F32)).astype(BF16)
    return hi, lo


def _head(h):
    return slice(h * HEAD_DIM, (h + 1) * HEAD_DIM)


def _heads_to_problems(x3):
    return jnp.concatenate([x3[:, :, _head(h)] for h in range(N_HEADS)], axis=0)


def _to_problems(x2d, nb):
    return _heads_to_problems(x2d.reshape(nb, x2d.shape[0] // nb, x2d.shape[1]))


def _from_problems(xp, nb):
    rows = nb * xp.shape[1]
    return jnp.concatenate([xp[h * nb:(h + 1) * nb].reshape(rows, HEAD_DIM) for h in range(N_HEADS)], axis=-1)


def _run_interleaved(*stage_generators):
    live = list(stage_generators)
    while live:
        live = [s for s in live if next(s, True) is None]


def _causal_masks():
    row = lax.broadcasted_iota(jnp.int32, (CHUNK, CHUNK), 0)
    col = lax.broadcasted_iota(jnp.int32, (CHUNK, CHUNK), 1)
    return row, col, (row >= col)[None], (row > col)[None]


def _prefix_suffix_rows():
    t = np.arange(CHUNK)[:, None]
    u = np.arange(CHUNK)[None, :]
    return [u <= t, u > t]


def _edge_table():
    tab = np.concatenate(_prefix_suffix_rows(), axis=0).astype(np.float32)
    return np.concatenate([tab, tab], axis=1)


def _level_table():
    t = np.arange(CHUNK)
    u = np.arange(CHUNK)
    rows = []
    for lvl in range(N_LEVELS):
        p = CHUNK >> (lvl + 1)
        mid = (t // (2 * p)) * (2 * p) + p - 1
        upper = t > mid
        rows.append(np.where(upper[:, None], (u[None, :] > mid[:, None]) & (u[None, :] <= t[:, None]),
                             (u[None, :] > t[:, None]) & (u[None, :] <= mid[:, None])))
    tab = np.concatenate(rows, axis=0).astype(np.float32)
    return np.concatenate([tab, tab], axis=1)


def _level_masks():
    t = np.arange(CHUNK)[:, None]
    s = np.arange(CHUNK)[None, :]
    out = []
    for lvl in range(N_LEVELS):
        p = CHUNK >> (lvl + 1)
        out.append((t // (2 * p) == s // (2 * p)) & (t % (2 * p) >= p) & (s % (2 * p) < p))
    out.append(t == s)
    return np.stack(out).astype(np.float32)


def _gated_head_norm(o, zg, g):
    outs = []
    for h in range(N_HEADS):
        oh = o[:, _head(h)]
        inv = lax.rsqrt(jnp.mean(oh * oh, axis=-1, keepdims=True) + NORM_EPS)
        outs.append(oh * inv)
    return jnp.concatenate(outs, axis=-1) * g * zg


def _zero(*refs):
    for r in refs:
        r[...] = jnp.zeros(r.shape, r.dtype)


def _gla_front(kind, x_ref, ng_ref, w_ref, aux_refs, edge_ref, slot, q_s, k_s, v_s, zg_s, gs_s, tab_s, safe_s, p_s):
    nb, _, d = x_ref.shape
    rows = nb * CHUNK
    w = BRANCH_WIDTH
    h = _rms_norm(x_ref[...].reshape(rows, d), ng_ref[...]).astype(BF16)
    yield

    def project(c0, c1):
        p_s[:, c0:c1] = _dot(h, w_ref[:, c0:c1])

    def store_decay(g):
        g_hi, g_lo = _split_bf16(g * LOG2E)
        for b in range(nb):
            r = slice(b * CHUNK, (b + 1) * CHUNK)
            gs_s[slot, b] = jnp.concatenate([g_hi[r], g_lo[r]], axis=0)

    if kind == "hgrn2":
        lb_ref, = aux_refs
        for c0 in range(0, 4 * w, w):
            project(c0, c0 + w)
            yield
        q_s[slot] = (_silu(p_s[:, 0:w]) * (HEAD_DIM ** -0.5)).astype(BF16)
        yield
        z = p_s[:, w:2 * w]
        lb = lb_ref[0:1, :]
        log_lb = lb_ref[1:2, :]
        c = lb_ref[2:3, :] + _log_sigmoid(z)
        store_decay(jnp.maximum(log_lb, c) + jnp.log1p(jnp.exp(-jnp.abs(log_lb - c))))
        k_s[slot] = ((1.0 - lb) * _sigmoid(-z)).astype(BF16)
        yield
        v_s[slot] = p_s[:, 2 * w:3 * w].astype(BF16)
        zg_s[slot] = _silu(p_s[:, 3 * w:4 * w])
        yield
    else:
        w2_ref, b2_ref = aux_refs
        project(4 * w, 4 * w + LANES)
        yield
        project(0, w)
        yield
        gk = _dot(p_s[:, 4 * w:4 * w + LANES].astype(BF16), w2_ref[...]) + b2_ref[...]
        store_decay(_log_sigmoid(gk) * (1.0 / GLA_GATE_NORMALIZER))
        yield
        project(w, 2 * w)
        yield
        project(2 * w, 3 * w)
        yield
    edge = edge_ref[...]
    low_mark = None
    for b in range(nb):
        tab = _dot(edge, gs_s[slot, b])
        tab_s[slot, b] = tab
        total = tab[CHUNK - 1:CHUNK, :]
        low_mark = total if low_mark is None else jnp.minimum(low_mark, total)
        if b % 4 == 3:
            yield
    safe_s[slot] = (jnp.min(low_mark) >= SAFE_LOG2_DECAY).astype(jnp.int32)
    if kind != "hgrn2":
        project(3 * w, 4 * w)
        yield
        q_s[slot] = (p_s[:, 0:w] * (GLA_DK ** -0.5)).astype(BF16)
        k_s[slot] = p_s[:, w:2 * w].astype(BF16)
        yield
        v_s[slot] = p_s[:, 2 * w:3 * w].astype(BF16)
        zg_s[slot] = _silu(p_s[:, 3 * w:4 * w])


def _gla_scores_centered(q, k, tab, nb):
    _, _, incl, _ = _causal_masks()
    pre = tab[:, 0:CHUNK, :]
    mid = 0.5 * pre[:, CHUNK - 1:CHUNK, :]
    rows = nb * CHUNK
    qf = q * jnp.exp2(pre - mid).astype(BF16).reshape(rows, -1)
    kf = k * jnp.exp2(mid - pre).astype(BF16).reshape(rows, -1)
    yield
    s = _bdot_nt(_to_problems(qf, nb), _to_problems(kf, nb))
    yield jnp.where(incl, s, 0.0)


def _gla_scores_levels(q, k, gs, level_ref, lmask_ref, nb):
    sc = _bdot_nt(_to_problems(q, nb), _to_problems(k, nb)) * lmask_ref[N_LEVELS][None]
    for l in range(N_LEVELS):
        wt = level_ref[l * CHUNK:(l + 1) * CHUNK, :]
        e = jnp.exp2(jnp.concatenate([_dot(wt, gs[b]) for b in range(nb)], axis=0)).astype(BF16)
        sc = sc + _bdot_nt(_to_problems(q * e, nb), _to_problems(k * e, nb)) * lmask_ref[l][None]
    return sc


def _gla_back(centered, slot, q_s, k_s, v_s, zg_s, gs_s, tab_s, st_ref, hn_ref, level_ref, lmask_ref, o_ref):
    nb = o_ref.shape[0]
    rows = nb * CHUNK
    q = q_s[slot]
    k = k_s[slot]
    tab = tab_s[slot]
    if centered:
        scores = _gla_scores_centered(q, k, tab, nb)
        yield next(scores)
        sc = next(scores)
    else:
        sc = _gla_scores_levels(q, k, [gs_s[slot, b] for b in range(nb)], level_ref, lmask_ref, nb)
    yield
    pre = tab[:, 0:CHUNK, :]
    e_in = jnp.exp2(pre).astype(BF16).reshape(rows, -1)
    e_out = jnp.exp2(tab[:, CHUNK:2 * CHUNK, :]).astype(BF16).reshape(rows, -1)
    e_last = _heads_to_problems(jnp.exp2(pre[:, CHUNK - 1:CHUNK, :]))
    v_p = _to_problems(v_s[slot], nb)
    st = st_ref[...]
    o = _bdot(sc.astype(BF16), v_p) + _bdot_nt(_to_problems(q * e_in, nb), st.astype(BF16))
    yield
    st_ref[...] = st * e_last + _bdot_tn(v_p, _to_problems(k * e_out, nb))
    yield
    y = _gated_head_norm(_from_problems(o, nb), zg_s[slot], hn_ref[...])
    o_ref[...] = y.reshape(o_ref.shape).astype(o_ref.dtype)


def _gla_kernel(kind, n_aux, x_ref, ng_ref, w_ref, *rest):
    aux_refs = rest[:n_aux]
    (hn_ref, edge_ref, level_ref, lmask_ref, o_ref,
     st_ref, q_s, k_s, v_s, zg_s, gs_s, tab_s, safe_s, p_s) = rest[n_aux:]
    j = pl.program_id(0)
    slot = j % 2
    prev = 1 - slot
    bufs = (q_s, k_s, v_s, zg_s, gs_s, tab_s)

    @pl.when(j == 0)
    def _():
        _zero(st_ref, *bufs)
        safe_s[0] = 1
        safe_s[1] = 1

    def front():
        return _gla_front(kind, x_ref, ng_ref, w_ref, aux_refs, edge_ref, slot, *bufs, safe_s, p_s)

    def back(centered):
        return _gla_back(centered, prev, *bufs, st_ref, hn_ref, level_ref, lmask_ref, o_ref)

    @pl.when(safe_s[prev] == 1)
    def _():
        _run_interleaved(front(), back(True))

    @pl.when(safe_s[prev] != 1)
    def _():
        _run_interleaved(back(False))
        _run_interleaved(front())


def _dn_front(x_ref, ng_ref, w_ref, cw_ref, ab_ref, slot, hist_s, ext_s, q_s, k_s, v_s, zg_s, bg_s):
    nb, _, d = x_ref.shape
    rows = nb * CHUNK
    w = BRANCH_WIDTH
    h = _rms_norm(x_ref[...].reshape(rows, d), ng_ref[...]).astype(BF16)
    yield
    for part, dst in enumerate((q_s, k_s, v_s)):
        cols = slice(part * w, (part + 1) * w)
        ext_s[:, 0:CONV_PAD, cols] = hist_s[:, :, cols]
        ext_s[:, CONV_PAD:CONV_PAD + CHUNK, cols] = _dot(h, w_ref[:, cols]).reshape(nb, CHUNK, w)
        hist_s[:, :, cols] = ext_s[:, CHUNK:CHUNK + CONV_PAD, cols]
        yield
        ext = ext_s[:, :, cols]
        conv = cw_ref[DN_CONV - 1:DN_CONV, cols][None] * ext[:, CONV_PAD:CONV_PAD + CHUNK]
        for back in range(1, DN_CONV):
            tap = DN_CONV - 1 - back
            shifted = pltpu.roll(ext, back, axis=1)
            conv = conv + cw_ref[tap:tap + 1, cols][None] * shifted[:, CONV_PAD:CONV_PAD + CHUNK]
        act = _silu(conv.reshape(rows, w))
        if part < 2:
            scale = HEAD_DIM ** -0.5 if part == 0 else 1.0
            heads = []
            for hd in range(N_HEADS):
                xh = act[:, _head(hd)]
                heads.append(xh * lax.rsqrt(jnp.sum(xh * xh, axis=-1, keepdims=True) + NORM_EPS) * scale)
            act = jnp.concatenate(heads, axis=-1)
        dst[slot] = _to_problems(act, nb)
        yield
    zg_s[slot] = _silu(_dot(h, w_ref[:, 3 * w:4 * w]))
    yield
    ba = _dot(h, w_ref[:, 4 * w:4 * w + LANES])
    bg_s[slot, 0] = _sigmoid(ba)
    bg_s[slot, 1] = -jnp.exp(ab_ref[0:1, :]) * _softplus(ba + ab_ref[1:2, :])


def _dn_back(slot, q_s, k_s, v_s, zg_s, bg_s, st_ref, hn_ref, tri_ref, o_ref):
    nb = o_ref.shape[0]
    q_p = q_s[slot]
    k_p = k_s[slot]
    v_p = v_s[slot]
    beta = bg_s[slot, 0]
    g_hi, g_lo = _split_bf16(bg_s[slot, 1])
    tri = tri_ref[...]
    gcols, grows = [], []
    for b in range(nb):
        r = slice(b * CHUNK, (b + 1) * CHUNK)
        gcum = _dot(tri, jnp.concatenate([g_hi[r], g_lo[r]], axis=0))
        gcols.append(gcum)
        grows.append(gcum.T)
    ln = lambda h: slice(N_HEADS + h, N_HEADS + h + 1)
    gc = jnp.concatenate([gcols[b][None, :, ln(h)] for h in range(N_HEADS) for b in range(nb)], axis=0)
    gr = jnp.concatenate([grows[b][None, ln(h), :] for h in range(N_HEADS) for b in range(nb)], axis=0)
    beta3 = beta.reshape(nb, CHUNK, LANES)
    bc = jnp.concatenate([beta3[:, :, h:h + 1] for h in range(N_HEADS)], axis=0)

    row, col, incl, strict = _causal_masks()
    decay = jnp.exp(jnp.where(incl, gc - gr, -jnp.inf))
    g_last = gc[:, CHUNK - 1:CHUNK, :]
    e_in = jnp.exp(gc)
    e_out = jnp.exp(g_last - gc)
    e_last = jnp.exp(g_last)

    kb = k_p * bc
    kq = _bdot_nt(jnp.concatenate([kb, q_p], axis=1).astype(BF16), k_p.astype(BF16))
    yield
    m = jnp.where(strict, kq[:, 0:CHUNK] * decay, 0.0)
    sc = jnp.where(incl, kq[:, CHUNK:2 * CHUNK] * decay, 0.0).astype(BF16)
    eye = (row == col).astype(F32)[None]
    t = None
    blk = 1
    while blk < CHUNK:
        sel = ((row // (2 * blk) == col // (2 * blk)) & (row % (2 * blk) >= blk) & (col % (2 * blk) < blk))[None]
        off = jnp.where(sel, m, 0.0)
        if blk == 1:
            t = eye - off
        else:
            t_bf = t.astype(BF16)
            half = _bdot(t_bf, off.astype(BF16)).astype(BF16)
            yield
            t = t - _bdot(half, t_bf)
            yield
        blk *= 2
    rhs = jnp.concatenate([v_p * bc, kb * e_in], axis=-1).astype(BF16)
    uw = _bdot(t.astype(BF16), rhs).astype(BF16)
    yield
    sc_uw = _bdot(sc, uw)
    k_uw = _bdot_tn((k_p * e_out).astype(BF16), uw)
    yield
    q_eff = (q_p * e_in - sc_uw[:, :, HEAD_DIM:]).astype(BF16)
    st = st_ref[...]
    st_bf = st.astype(BF16)
    o = _bdot(q_eff, st_bf) + sc_uw[:, :, 0:HEAD_DIM]
    st_ref[...] = st * e_last + k_uw[:, :, 0:HEAD_DIM] - _bdot(k_uw[:, :, HEAD_DIM:].astype(BF16), st_bf)
    yield
    y = _gated_head_norm(_from_problems(o, nb), zg_s[slot], hn_ref[...])
    o_ref[...] = y.reshape(o_ref.shape).astype(o_ref.dtype)


def _dn_kernel(x_ref, ng_ref, w_ref, cw_ref, ab_ref, hn_ref, tri_ref, o_ref,
               st_ref, hist_s, ext_s, q_s, k_s, v_s, zg_s, bg_s):
    j = pl.program_id(0)
    slot = j % 2
    bufs = (q_s, k_s, v_s, zg_s, bg_s)

    @pl.when(j == 0)
    def _():
        _zero(st_ref, hist_s, *bufs)

    _run_interleaved(
        _dn_front(x_ref, ng_ref, w_ref, cw_ref, ab_ref, slot, hist_s, ext_s, *bufs),
        _dn_back(1 - slot, *bufs, st_ref, hn_ref, tri_ref, o_ref))


def _merge_kernel(x_ref, ng_ref, wg_ref, gb_ref, oa_ref, ob_ref, oc_ref, wb_ref, wo_ref, y_ref):
    x = x_ref[...]
    h = _rms_norm(x, ng_ref[...]).astype(BF16)
    gates = _sigmoid(_dot(h, wg_ref[...]) + gb_ref[...])
    merged = None
    for n, o_n in enumerate((oa_ref, ob_ref, oc_ref)):
        term = gates[:, n * D_MODEL:(n + 1) * D_MODEL] * _dot(o_n[...], wb_ref[n])
        merged = term if merged is None else merged + term
    y_ref[...] = x + _dot(merged.astype(BF16), wo_ref[...])


def _ffn_kernel(x_ref, ng_ref, wu_ref, cw_ref, cb_ref, wd_ref, fg_ref, y_ref, ext_s, hist_s, act_s,
                *, final_norm):
    @pl.when(pl.program_id(1) == 0)
    def _():
        hist_s[...] = jnp.zeros(hist_s.shape, hist_s.dtype)

    tile = x_ref.shape[0]
    x = x_ref[...]
    h = _rms_norm(x, ng_ref[...]).astype(BF16)
    ext_s[0:CONV_PAD, :] = hist_s[...]
    ext_s[CONV_PAD:CONV_PAD + tile, :] = _dot(h, wu_ref[...])
    hist_s[...] = ext_s[tile:tile + CONV_PAD, :]
    blk = 2 * LANES
    split = (FFN_HIDDEN // blk // 2 + 1) * blk
    y = x
    for j0 in range(0, FFN_HIDDEN, blk):
        halves = []
        for base in (j0, FFN_HIDDEN + j0):
            cols = slice(base, base + blk)
            u = cb_ref[0:1, cols] + cw_ref[0:1, cols] * ext_s[CONV_PAD - 2:CONV_PAD - 2 + tile, cols]
            for j in range(1, FFN_CONV):
                u = u + cw_ref[j:j + 1, cols] * ext_s[CONV_PAD - 2 + j:CONV_PAD - 2 + j + tile, cols]
            halves.append(u)
        act_s[:, j0:j0 + blk] = (_silu(halves[0]) * halves[1]).astype(BF16)
        if j0 + blk == split:
            y = y + _dot(act_s[:, 0:split], wd_ref[0:split, :])
    y = y + _dot(act_s[:, split:], wd_ref[split:, :])
    if final_norm:
        y = _rms_norm(y, fg_ref[...])
    y_ref[...] = y


def _const_spec(shape, grid_rank):
    nd = len(shape)
    if grid_rank == 1:
        return pl.BlockSpec(shape, lambda i, _nd=nd: (0,) * _nd, pipeline_mode=pl.Buffered(1))
    return pl.BlockSpec(shape, lambda b, i, _nd=nd: (0,) * _nd, pipeline_mode=pl.Buffered(1))


def _row_spec(tile, width, steps):
    return pl.BlockSpec((tile, width), lambda b, i, _s=steps: (b * _s + i, 0))


def _mixer_call(body, name, x3, consts, scratch):
    batch, seq, _ = x3.shape
    n = seq // CHUNK
    in_specs = ([pl.BlockSpec((batch, CHUNK, D_MODEL), lambda j: (0, jnp.minimum(j, n - 1), 0))]
                + [_const_spec(c.shape, 1) for c in consts])
    return pl.pallas_call(
        body,
        grid=(n + 1,),
        in_specs=in_specs,
        out_specs=pl.BlockSpec((batch, CHUNK, BRANCH_WIDTH), lambda j: (0, jnp.maximum(j - 1, 0), 0)),
        out_shape=jax.ShapeDtypeStruct((batch, seq, BRANCH_WIDTH), BF16),
        scratch_shapes=scratch,
        compiler_params=pltpu.CompilerParams(dimension_semantics=("arbitrary",), vmem_limit_bytes=VMEM_LIMIT),
        name=name,
    )(x3, *consts)


def _gla_scratch(batch, proj_width):
    rows = batch * CHUNK
    w = BRANCH_WIDTH
    return [pltpu.VMEM((N_HEADS * batch, HEAD_DIM, HEAD_DIM), F32),
            pltpu.VMEM((2, rows, w), BF16), pltpu.VMEM((2, rows, w), BF16), pltpu.VMEM((2, rows, w), BF16),
            pltpu.VMEM((2, rows, w), F32),
            pltpu.VMEM((2, batch, 2 * CHUNK, w), BF16),
            pltpu.VMEM((2, batch, 2 * CHUNK, w), F32),
            pltpu.SMEM((2,), jnp.int32),
            pltpu.VMEM((rows, proj_width), F32)]


def _dn_scratch(batch):
    rows = batch * CHUNK
    prob = (2, N_HEADS * batch, CHUNK, HEAD_DIM)
    return [pltpu.VMEM((N_HEADS * batch, HEAD_DIM, HEAD_DIM), F32),
            pltpu.VMEM((batch, CONV_PAD, 3 * BRANCH_WIDTH), F32),
            pltpu.VMEM((batch, CHUNK + CONV_PAD, 3 * BRANCH_WIDTH), F32),
            pltpu.VMEM(prob, F32), pltpu.VMEM(prob, F32), pltpu.VMEM(prob, F32),
            pltpu.VMEM((2, rows, BRANCH_WIDTH), F32),
            pltpu.VMEM((2, 2, rows, LANES), F32)]


def _params2():
    return pltpu.CompilerParams(dimension_semantics=("arbitrary", "arbitrary"), vmem_limit_bytes=VMEM_LIMIT)


def _merge_call(x2, batch, consts_a, branches, consts_b):
    rows = x2.shape[0]
    steps = rows // batch // TILE_MERGE
    in_specs = ([_row_spec(TILE_MERGE, D_MODEL, steps)] + [_const_spec(c.shape, 2) for c in consts_a]
                + [_row_spec(TILE_MERGE, BRANCH_WIDTH, steps) for _ in branches]
                + [_const_spec(c.shape, 2) for c in consts_b])
    return pl.pallas_call(
        _merge_kernel,
        grid=(batch, steps),
        in_specs=in_specs,
        out_specs=_row_spec(TILE_MERGE, D_MODEL, steps),
        out_shape=jax.ShapeDtypeStruct((rows, D_MODEL), F32),
        compiler_params=_params2(),
        name="merge",
    )(x2, *consts_a, *branches, *consts_b)


def _ffn_call(x2, batch, consts, final_norm):
    rows = x2.shape[0]
    steps = rows // batch // TILE_FFN
    in_specs = [_row_spec(TILE_FFN, D_MODEL, steps)] + [_const_spec(c.shape, 2) for c in consts]
    return pl.pallas_call(
        functools.partial(_ffn_kernel, final_norm=final_norm),
        grid=(batch, steps),
        in_specs=in_specs,
        out_specs=_row_spec(TILE_FFN, D_MODEL, steps),
        out_shape=jax.ShapeDtypeStruct((rows, D_MODEL), F32),
        scratch_shapes=[pltpu.VMEM((TILE_FFN + CONV_PAD, 2 * FFN_HIDDEN), F32),
                        pltpu.VMEM((CONV_PAD, 2 * FFN_HIDDEN), F32),
                        pltpu.VMEM((TILE_FFN, FFN_HIDDEN), BF16)],
        compiler_params=_params2(),
        name="ffn_final" if final_norm else "ffn",
    )(x2, *consts)


def _pad_heads(wcols, dk):
    lead = wcols.shape[:-1]
    w4 = wcols.reshape(lead + (N_HEADS, dk))
    w4 = jnp.pad(w4, [(0, 0)] * len(lead) + [(0, 0), (0, HEAD_DIM - dk)])
    return w4.reshape(lead + (N_HEADS * HEAD_DIM,))


def _pad_cols(wcols, width):
    return jnp.pad(wcols, [(0, 0)] * (wcols.ndim - 1) + [(0, width - wcols.shape[-1])])


def _row(v):
    return v.reshape(1, -1).astype(F32)


def kernel(x, norm_mix_g, w_in, hg_lower_bounds, hg_norm_g, gla_gk_w2, gla_gk_b, gla_norm_g, dn_conv_w,
           dn_A_log, dn_dt_bias, dn_norm_g, w_branch, gate_b, w_out, norm_ffn_g, w_up, ffn_conv_w,
           ffn_conv_b, w_down, norm_final_g):
    batch, seq, d = x.shape
    depth = w_in.shape[0]
    assert d == D_MODEL and seq % max(CHUNK, TILE_MERGE, TILE_FFN) == 0

    lb_all = jnp.cumsum(jax.nn.softmax(hg_lower_bounds.astype(F32), axis=0), axis=0)
    lb_all = lb_all - lb_all[:1]
    edge = jnp.asarray(_edge_table(), BF16)
    level = jnp.asarray(_level_table(), BF16)
    lmask = jnp.asarray(_level_masks(), F32)
    tri1 = np.tril(np.ones((CHUNK, CHUNK), np.float32))
    tri = jnp.asarray(np.concatenate([tri1, tri1], axis=1), BF16)

    hw = BRANCH_WIDTH
    c0 = 4 * hw
    c1 = c0 + 2 * N_HEADS * GLA_DK
    c2 = c1 + hw
    c3 = c2 + GLA_RANK
    c4 = c3 + hw
    c5 = c4 + 3 * hw
    c6 = c5 + hw
    c7 = c6 + N_HEADS
    c8 = c7 + N_HEADS

    x3 = x
    for l in range(depth):
        wl = w_in[l]
        ng = _row(norm_mix_g[l])
        lb = lb_all[l]
        lb_rows = jnp.stack([lb, jnp.log(lb), jnp.log1p(-lb)]).astype(F32)
        o_hg = _mixer_call(
            functools.partial(_gla_kernel, "hgrn2", 1), "hgrn2", x3,
            [ng, wl[:, :c0].astype(BF16), lb_rows, _row(jnp.tile(hg_norm_g[l], N_HEADS)), edge, level, lmask],
            _gla_scratch(batch, c0))

        w_gla = jnp.concatenate([
            _pad_heads(wl[:, c0:c0 + N_HEADS * GLA_DK], GLA_DK),
            _pad_heads(wl[:, c0 + N_HEADS * GLA_DK:c1], GLA_DK),
            wl[:, c1:c2], wl[:, c3:c4], _pad_cols(wl[:, c2:c3], LANES)], axis=1).astype(BF16)
        w2 = jnp.pad(_pad_heads(gla_gk_w2[l], GLA_DK), ((0, LANES - GLA_RANK), (0, 0))).astype(BF16)
        b2 = _row(_pad_heads(gla_gk_b[l], GLA_DK))
        o_gla = _mixer_call(
            functools.partial(_gla_kernel, "gla", 2), "gla", x3,
            [ng, w_gla, w2, b2, _row(jnp.tile(gla_norm_g[l], N_HEADS)), edge, level, lmask],
            _gla_scratch(batch, w_gla.shape[1]))

        w_dn = jnp.concatenate([wl[:, c4:c6], _pad_cols(wl[:, c6:c8], LANES)], axis=1).astype(BF16)
        ab = jnp.zeros((2, LANES), F32)
        ab = ab.at[0, N_HEADS:2 * N_HEADS].set(dn_A_log[l].astype(F32))
        ab = ab.at[1, N_HEADS:2 * N_HEADS].set(dn_dt_bias[l].astype(F32))
        o_dn = _mixer_call(
            _dn_kernel, "deltanet", x3,
            [ng, w_dn, dn_conv_w[l].astype(F32), ab, _row(jnp.tile(dn_norm_g[l], N_HEADS)), tri],
            _dn_scratch(batch))

        x2 = _merge_call(
            x3.reshape(batch * seq, d), batch, [ng, wl[:, c8:].astype(BF16), _row(gate_b[l])],
            [o.reshape(batch * seq, hw) for o in (o_hg, o_gla, o_dn)],
            [w_branch[l].astype(BF16), w_out[l].astype(BF16)])

        x2 = _ffn_call(
            x2, batch,
            [_row(norm_ffn_g[l]), w_up[l].astype(BF16), ffn_conv_w[l].astype(F32), _row(ffn_conv_b[l]),
             w_down[l].astype(BF16), _row(norm_final_g)],
            final_norm=(l == depth - 1))
        x3 = x2.reshape(batch, seq, d)
    return x3
```

```python
import functools

import jax
import jax.numpy as jnp
import numpy as np
from jax import lax
from jax.experimental import pallas as pl
from jax.experimental.pallas import tpu as pltpu

F32 = jnp.float32
BF16 = jnp.bfloat16

D_MODEL = 1024
CHUNK = 64
NORM_EPS = 1e-6
N_HEADS = 4
HEAD_DIM = 128
BRANCH_WIDTH = N_HEADS * HEAD_DIM
GLA_DK = 64
GLA_RANK = 16
GLA_GATE_NORMALIZER = 16.0
DN_CONV = 4
N_BRANCH = 3
FFN_HIDDEN = 2816
FFN_CONV = 3
LANES = 128
CONV_PAD = 8
N_LEVELS = 6
LOG2E = 1.4426950408889634
SAFE_LOG2_DECAY = -200.0

TILE_MERGE = 512
TILE_FFN = 512
VMEM_LIMIT = 56 * 1024 * 1024


def _sigmoid(x):
    return 1.0 / (1.0 + jnp.exp(-x))


def _silu(x):
    return x * _sigmoid(x)


def _softplus(x):
    return jnp.maximum(x, 0.0) + jnp.log1p(jnp.exp(-jnp.abs(x)))


def _log_sigmoid(x):
    return -_softplus(-x)


def _rms_norm(x, g):
    y = x * lax.rsqrt(jnp.mean(x * x, axis=-1, keepdims=True) + NORM_EPS)
    return y * g


def _dot(a, b):
    return jnp.dot(a, b, preferred_element_type=F32)


def _bdot(a, b):
    return lax.dot_general(a, b, (((2,), (1,)), ((0,), (0,))), preferred_element_type=F32)


def _bdot_nt(a, b):
    return lax.dot_general(a, b, (((2,), (2,)), ((0,), (0,))), preferred_element_type=F32)


def _bdot_tn(a, b):
    return lax.dot_general(a, b, (((1,), (1,)), ((0,), (0,))), preferred_element_type=F32)


def _split_bf16(x):
    hi = x.astype(BF16)
    lo = (x - hi.astype(F32)).astype(BF16)
    return hi, lo


def _head(h):
    return slice(h * HEAD_DIM, (h + 1) * HEAD_DIM)


def _heads_to_problems(x3):
    return jnp.concatenate([x3[:, :, _head(h)] for h in range(N_HEADS)], axis=0)


def _to_problems(x2d, nb):
    return _heads_to_problems(x2d.reshape(nb, x2d.shape[0] // nb, x2d.shape[1]))


def _key_problems(x, nb, query=False):
    x3 = x if x.ndim == 3 else x.reshape(nb, x.shape[0] // nb, x.shape[1])
    if x3.shape[-1] == BRANCH_WIDTH:
        return _heads_to_problems(x3)
    lane = lax.broadcasted_iota(jnp.int32, (1, 1, LANES), 2)
    parts = []
    for h in range(N_HEADS):
        xp = x3[:, :, (h // 2) * LANES:(h // 2 + 1) * LANES]
        if query:
            own = (lane >= LANES // 2) if h % 2 else (lane < LANES // 2)
            xp = jnp.where(own, xp, jnp.zeros_like(xp))
        parts.append(xp)
    return jnp.concatenate(parts, axis=0)


def _from_problems(xp, nb):
    rows = nb * xp.shape[1]
    return jnp.concatenate([xp[h * nb:(h + 1) * nb].reshape(rows, HEAD_DIM) for h in range(N_HEADS)], axis=-1)


def _run_interleaved(*stage_generators):
    live = list(stage_generators)
    while live:
        live = [s for s in live if next(s, True) is None]


def _causal_masks():
    row = lax.broadcasted_iota(jnp.int32, (CHUNK, CHUNK), 0)
    col = lax.broadcasted_iota(jnp.int32, (CHUNK, CHUNK), 1)
    return row, col, (row >= col)[None], (row > col)[None]


def _prefix_suffix_rows():
    t = np.arange(CHUNK)[:, None]
    u = np.arange(CHUNK)[None, :]
    return [u <= t, u > t]


def _edge_table():
    tab = np.concatenate(_prefix_suffix_rows(), axis=0).astype(np.float32)
    return np.concatenate([tab, tab], axis=1)


def _level_table():
    t = np.arange(CHUNK)
    u = np.arange(CHUNK)
    rows = []
    for lvl in range(N_LEVELS):
        p = CHUNK >> (lvl + 1)
        mid = (t // (2 * p)) * (2 * p) + p - 1
        upper = t > mid
        rows.append(np.where(upper[:, None], (u[None, :] > mid[:, None]) & (u[None, :] <= t[:, None]),
                             (u[None, :] > t[:, None]) & (u[None, :] <= mid[:, None])))
    tab = np.concatenate(rows, axis=0).astype(np.float32)
    return np.concatenate([tab, tab], axis=1)


def _level_masks():
    t = np.arange(CHUNK)[:, None]
    s = np.arange(CHUNK)[None, :]
    out = []
    for lvl in range(N_LEVELS):
        p = CHUNK >> (lvl + 1)
        out.append((t // (2 * p) == s // (2 * p)) & (t % (2 * p) >= p) & (s % (2 * p) < p))
    out.append(t == s)
    return np.stack(out).astype(np.float32)


def _gated_head_norm(o, zg, g):
    outs = []
    for h in range(N_HEADS):
        oh = o[:, _head(h)]
        inv = lax.rsqrt(jnp.mean(oh * oh, axis=-1, keepdims=True) + NORM_EPS)
        outs.append(oh * inv)
    return jnp.concatenate(outs, axis=-1) * g * zg


def _zero(*refs):
    for r in refs:
        r[...] = jnp.zeros(r.shape, r.dtype)


def _gla_front(kind, x_ref, ng_ref, w_ref, aux_refs, edge_ref, slot, q_s, k_s, v_s, zg_s, gs_s, tab_s, safe_s, p_s):
    nb, _, d = x_ref.shape
    rows = nb * CHUNK
    w = BRANCH_WIDTH
    h = _rms_norm(x_ref[...].reshape(rows, d), ng_ref[...]).astype(BF16)
    yield

    def project(c0, c1):
        p_s[:, c0:c1] = _dot(h, w_ref[:, c0:c1])

    def store_decay(g):
        g_hi, g_lo = _split_bf16(g * LOG2E)
        for b in range(nb):
            r = slice(b * CHUNK, (b + 1) * CHUNK)
            gs_s[slot, b] = jnp.concatenate([g_hi[r], g_lo[r]], axis=0)

    if kind == "hgrn2":
        lb_ref, = aux_refs
        for c0 in range(0, 4 * w, w):
            project(c0, c0 + w)
            yield
        q_s[slot] = (_silu(p_s[:, 0:w]) * (HEAD_DIM ** -0.5)).astype(BF16)
        yield
        z = p_s[:, w:2 * w]
        lb = lb_ref[0:1, :]
        log_lb = lb_ref[1:2, :]
        c = lb_ref[2:3, :] + _log_sigmoid(z)
        store_decay(jnp.maximum(log_lb, c) + jnp.log1p(jnp.exp(-jnp.abs(log_lb - c))))
        k_s[slot] = ((1.0 - lb) * _sigmoid(-z)).astype(BF16)
        yield
        v_s[slot] = p_s[:, 2 * w:3 * w].astype(BF16)
        zg_s[slot] = _silu(p_s[:, 3 * w:4 * w])
        yield
    else:
        w2_ref, b2_ref = aux_refs
        kw = q_s.shape[-1]
        low = 2 * kw + 2 * w
        project(low, low + LANES)
        yield
        project(0, 2 * kw)
        yield
        gk = _dot(p_s[:, low:low + LANES].astype(BF16), w2_ref[...]) + b2_ref[...]
        store_decay(_log_sigmoid(gk) * (1.0 / GLA_GATE_NORMALIZER))
        yield
        project(2 * kw, 2 * kw + w)
        yield
    edge = edge_ref[...]
    low_mark = None
    for b in range(nb):
        tab = _dot(edge, gs_s[slot, b])
        tab_s[slot, b] = tab
        total = tab[CHUNK - 1:CHUNK, :]
        low_mark = total if low_mark is None else jnp.minimum(low_mark, total)
        if b % 4 == 3:
            yield
    safe_s[slot] = (jnp.min(low_mark) >= SAFE_LOG2_DECAY).astype(jnp.int32)
    if kind != "hgrn2":
        project(2 * kw + w, 2 * kw + 2 * w)
        yield
        q_s[slot] = (p_s[:, 0:kw] * (GLA_DK ** -0.5)).astype(BF16)
        k_s[slot] = p_s[:, kw:2 * kw].astype(BF16)
        yield
        v_s[slot] = p_s[:, 2 * kw:2 * kw + w].astype(BF16)
        zg_s[slot] = _silu(p_s[:, 2 * kw + w:2 * kw + 2 * w])


def _gla_scores_centered(q, k, tab, nb):
    _, _, incl, _ = _causal_masks()
    pre = tab[:, 0:CHUNK, :]
    mid = 0.5 * pre[:, CHUNK - 1:CHUNK, :]
    rows = nb * CHUNK
    qf = q * jnp.exp2(pre - mid).astype(BF16).reshape(rows, -1)
    kf = k * jnp.exp2(mid - pre).astype(BF16).reshape(rows, -1)
    yield
    s = _bdot_nt(_key_problems(qf, nb, query=True), _key_problems(kf, nb))
    yield jnp.where(incl, s, 0.0)


def _gla_scores_levels(q, k, gs, level_ref, lmask_ref, nb):
    sc = _bdot_nt(_key_problems(q, nb, query=True), _key_problems(k, nb)) * lmask_ref[N_LEVELS][None]
    for l in range(N_LEVELS):
        wt = level_ref[l * CHUNK:(l + 1) * CHUNK, :]
        e = jnp.exp2(jnp.concatenate([_dot(wt, gs[b]) for b in range(nb)], axis=0)).astype(BF16)
        sc = sc + _bdot_nt(_key_problems(q * e, nb, query=True), _key_problems(k * e, nb)) * lmask_ref[l][None]
    return sc


def _gla_back(centered, slot, q_s, k_s, v_s, zg_s, gs_s, tab_s, st_ref, hn_ref, level_ref, lmask_ref, o_ref):
    nb = o_ref.shape[0]
    rows = nb * CHUNK
    q = q_s[slot]
    k = k_s[slot]
    tab = tab_s[slot]
    if centered:
        scores = _gla_scores_centered(q, k, tab, nb)
        yield next(scores)
        sc = next(scores)
    else:
        sc = _gla_scores_levels(q, k, [gs_s[slot, b] for b in range(nb)], level_ref, lmask_ref, nb)
    yield
    pre = tab[:, 0:CHUNK, :]
    e_in = jnp.exp2(pre).astype(BF16).reshape(rows, -1)
    e_out = jnp.exp2(tab[:, CHUNK:2 * CHUNK, :]).astype(BF16).reshape(rows, -1)
    e_last = _key_problems(jnp.exp2(pre[:, CHUNK - 1:CHUNK, :]), nb)
    v_p = _to_problems(v_s[slot], nb)
    st = st_ref[...]
    o = _bdot(sc.astype(BF16), v_p) + _bdot_nt(_key_problems(q * e_in, nb, query=True), st.astype(BF16))
    yield
    st_ref[...] = st * e_last + _bdot_tn(v_p, _key_problems(k * e_out, nb))
    yield
    y = _gated_head_norm(_from_problems(o, nb), zg_s[slot], hn_ref[...])
    o_ref[...] = y.reshape(o_ref.shape).astype(o_ref.dtype)


def _gla_kernel(kind, n_aux, x_ref, ng_ref, w_ref, *rest):
    aux_refs = rest[:n_aux]
    (hn_ref, edge_ref, level_ref, lmask_ref, o_ref,
     st_ref, q_s, k_s, v_s, zg_s, gs_s, tab_s, safe_s, p_s) = rest[n_aux:]
    j = pl.program_id(0)
    slot = j % 2
    prev = 1 - slot
    bufs = (q_s, k_s, v_s, zg_s, gs_s, tab_s)

    @pl.when(j == 0)
    def _():
        _zero(st_ref, *bufs)
        safe_s[0] = 1
        safe_s[1] = 1

    def front():
        return _gla_front(kind, x_ref, ng_ref, w_ref, aux_refs, edge_ref, slot, *bufs, safe_s, p_s)

    def back(centered):
        return _gla_back(centered, prev, *bufs, st_ref, hn_ref, level_ref, lmask_ref, o_ref)

    @pl.when(safe_s[prev] == 1)
    def _():
        _run_interleaved(front(), back(True))

    @pl.when(safe_s[prev] != 1)
    def _():
        _run_interleaved(back(False))
        _run_interleaved(front())


def _dn_front(x_ref, ng_ref, w_ref, cw_ref, ab_ref, slot, hist_s, ext_s, q_s, k_s, v_s, zg_s, bg_s):
    nb, _, d = x_ref.shape
    rows = nb * CHUNK
    w = BRANCH_WIDTH
    h = _rms_norm(x_ref[...].reshape(rows, d), ng_ref[...]).astype(BF16)
    yield
    for part, dst in enumerate((q_s, k_s, v_s)):
        cols = slice(part * w, (part + 1) * w)
        ext_s[:, 0:CONV_PAD, cols] = hist_s[:, :, cols]
        ext_s[:, CONV_PAD:CONV_PAD + CHUNK, cols] = _dot(h, w_ref[:, cols]).reshape(nb, CHUNK, w)
        hist_s[:, :, cols] = ext_s[:, CHUNK:CHUNK + CONV_PAD, cols]
        yield
        ext = ext_s[:, :, cols]
        conv = cw_ref[DN_CONV - 1:DN_CONV, cols][None] * ext[:, CONV_PAD:CONV_PAD + CHUNK]
        for back in range(1, DN_CONV):
            tap = DN_CONV - 1 - back
            shifted = pltpu.roll(ext, back, axis=1)
            conv = conv + cw_ref[tap:tap + 1, cols][None] * shifted[:, CONV_PAD:CONV_PAD + CHUNK]
        act = _silu(conv.reshape(rows, w))
        if part < 2:
            scale = HEAD_DIM ** -0.5 if part == 0 else 1.0
            heads = []
            for hd in range(N_HEADS):
                xh = act[:, _head(hd)]
                heads.append(xh * lax.rsqrt(jnp.sum(xh * xh, axis=-1, keepdims=True) + NORM_EPS) * scale)
            act = jnp.concatenate(heads, axis=-1)
        dst[slot] = _to_problems(act, nb)
        yield
    zg_s[slot] = _silu(_dot(h, w_ref[:, 3 * w:4 * w]))
    yield
    ba = _dot(h, w_ref[:, 4 * w:4 * w + LANES])
    bg_s[slot, 0] = _sigmoid(ba)
    bg_s[slot, 1] = -jnp.exp(ab_ref[0:1, :]) * _softplus(ba + ab_ref[1:2, :])


def _dn_back(slot, q_s, k_s, v_s, zg_s, bg_s, st_ref, hn_ref, tri_ref, o_ref):
    nb = o_ref.shape[0]
    q_p = q_s[slot]
    k_p = k_s[slot]
    v_p = v_s[slot]
    beta = bg_s[slot, 0]
    g_hi, g_lo = _split_bf16(bg_s[slot, 1])
    tri = tri_ref[...]
    gcols, grows = [], []
    for b in range(nb):
        r = slice(b * CHUNK, (b + 1) * CHUNK)
        gcum = _dot(tri, jnp.concatenate([g_hi[r], g_lo[r]], axis=0))
        gcols.append(gcum)
        grows.append(gcum.T)
    ln = lambda h: slice(N_HEADS + h, N_HEADS + h + 1)
    gc = jnp.concatenate([gcols[b][None, :, ln(h)] for h in range(N_HEADS) for b in range(nb)], axis=0)
    gr = jnp.concatenate([grows[b][None, ln(h), :] for h in range(N_HEADS) for b in range(nb)], axis=0)
    beta3 = beta.reshape(nb, CHUNK, LANES)
    bc = jnp.concatenate([beta3[:, :, h:h + 1] for h in range(N_HEADS)], axis=0)

    row, col, incl, strict = _causal_masks()
    decay = jnp.exp(jnp.where(incl, gc - gr, -jnp.inf))
    g_last = gc[:, CHUNK - 1:CHUNK, :]
    e_in = jnp.exp(gc)
    e_out = jnp.exp(g_last - gc)
    e_last = jnp.exp(g_last)

    kb = k_p * bc
    kq = _bdot_nt(jnp.concatenate([kb, q_p], axis=1).astype(BF16), k_p.astype(BF16))
    yield
    m = jnp.where(strict, kq[:, 0:CHUNK] * decay, 0.0)
    sc = jnp.where(incl, kq[:, CHUNK:2 * CHUNK] * decay, 0.0).astype(BF16)
    eye = (row == col).astype(F32)[None]
    t = None
    blk = 1
    while blk < CHUNK:
        sel = ((row // (2 * blk) == col // (2 * blk)) & (row % (2 * blk) >= blk) & (col % (2 * blk) < blk))[None]
        off = jnp.where(sel, m, 0.0)
        if blk == 1:
            t = eye - off
        else:
            t_bf = t.astype(BF16)
            half = _bdot(t_bf, off.astype(BF16)).astype(BF16)
            yield
            t = t - _bdot(half, t_bf)
            yield
        blk *= 2
    rhs = jnp.concatenate([v_p * bc, kb * e_in], axis=-1).astype(BF16)
    uw = _bdot(t.astype(BF16), rhs).astype(BF16)
    yield
    sc_uw = _bdot(sc, uw)
    k_uw = _bdot_tn((k_p * e_out).astype(BF16), uw)
    yield
    q_eff = (q_p * e_in - sc_uw[:, :, HEAD_DIM:]).astype(BF16)
    st = st_ref[...]
    st_bf = st.astype(BF16)
    o = _bdot(q_eff, st_bf) + sc_uw[:, :, 0:HEAD_DIM]
    st_ref[...] = st * e_last + k_uw[:, :, 0:HEAD_DIM] - _bdot(k_uw[:, :, HEAD_DIM:].astype(BF16), st_bf)
    yield
    y = _gated_head_norm(_from_problems(o, nb), zg_s[slot], hn_ref[...])
    o_ref[...] = y.reshape(o_ref.shape).astype(o_ref.dtype)


def _dn_kernel(x_ref, ng_ref, w_ref, cw_ref, ab_ref, hn_ref, tri_ref, o_ref,
               st_ref, hist_s, ext_s, q_s, k_s, v_s, zg_s, bg_s):
    j = pl.program_id(0)
    slot = j % 2
    bufs = (q_s, k_s, v_s, zg_s, bg_s)

    @pl.when(j == 0)
    def _():
        _zero(st_ref, hist_s, *bufs)

    _run_interleaved(
        _dn_front(x_ref, ng_ref, w_ref, cw_ref, ab_ref, slot, hist_s, ext_s, *bufs),
        _dn_back(1 - slot, *bufs, st_ref, hn_ref, tri_ref, o_ref))


def _merge_kernel(x_ref, ng_ref, wg_ref, gb_ref, oa_ref, ob_ref, oc_ref, wb_ref, wo_ref, y_ref):
    x = x_ref[...]
    h = _rms_norm(x, ng_ref[...]).astype(BF16)
    gates = _sigmoid(_dot(h, wg_ref[...]) + gb_ref[...])
    merged = None
    for n, o_n in enumerate((oa_ref, ob_ref, oc_ref)):
        term = gates[:, n * D_MODEL:(n + 1) * D_MODEL] * _dot(o_n[...], wb_ref[n])
        merged = term if merged is None else merged + term
    y_ref[...] = x + _dot(merged.astype(BF16), wo_ref[...])


def _ffn_kernel(x_ref, ng_ref, wu_ref, cw_ref, cb_ref, wd_ref, fg_ref, y_ref, ext_s, hist_s, act_s,
                *, final_norm):
    @pl.when(pl.program_id(1) == 0)
    def _():
        hist_s[...] = jnp.zeros(hist_s.shape, hist_s.dtype)

    tile = x_ref.shape[0]
    x = x_ref[...]
    h = _rms_norm(x, ng_ref[...]).astype(BF16)
    ext_s[0:CONV_PAD, :] = hist_s[...]
    ext_s[CONV_PAD:CONV_PAD + tile, :] = _dot(h, wu_ref[...])
    hist_s[...] = ext_s[tile:tile + CONV_PAD, :]
    blk = 2 * LANES
    split = (FFN_HIDDEN // blk // 2 + 1) * blk
    y = x
    for j0 in range(0, FFN_HIDDEN, blk):
        halves = []
        for base in (j0, FFN_HIDDEN + j0):
            cols = slice(base, base + blk)
            u = cb_ref[0:1, cols] + cw_ref[0:1, cols] * ext_s[CONV_PAD - 2:CONV_PAD - 2 + tile, cols]
            for j in range(1, FFN_CONV):
                u = u + cw_ref[j:j + 1, cols] * ext_s[CONV_PAD - 2 + j:CONV_PAD - 2 + j + tile, cols]
            halves.append(u)
        act_s[:, j0:j0 + blk] = (_silu(halves[0]) * halves[1]).astype(BF16)
        if j0 + blk == split:
            y = y + _dot(act_s[:, 0:split], wd_ref[0:split, :])
    y = y + _dot(act_s[:, split:], wd_ref[split:, :])
    if final_norm:
        y = _rms_norm(y, fg_ref[...])
    y_ref[...] = y


def _const_spec(shape, grid_rank):
    nd = len(shape)
    if grid_rank == 1:
        return pl.BlockSpec(shape, lambda i, _nd=nd: (0,) * _nd, pipeline_mode=pl.Buffered(1))
    return pl.BlockSpec(shape, lambda b, i, _nd=nd: (0,) * _nd, pipeline_mode=pl.Buffered(1))


def _row_spec(tile, width, steps):
    return pl.BlockSpec((tile, width), lambda b, i, _s=steps: (b * _s + i, 0))


def _mixer_call(body, name, x3, consts, scratch):
    batch, seq, _ = x3.shape
    n = seq // CHUNK
    in_specs = ([pl.BlockSpec((batch, CHUNK, D_MODEL), lambda j: (0, jnp.minimum(j, n - 1), 0))]
                + [_const_spec(c.shape, 1) for c in consts])
    return pl.pallas_call(
        body,
        grid=(n + 1,),
        in_specs=in_specs,
        out_specs=pl.BlockSpec((batch, CHUNK, BRANCH_WIDTH), lambda j: (0, jnp.maximum(j - 1, 0), 0)),
        out_shape=jax.ShapeDtypeStruct((batch, seq, BRANCH_WIDTH), BF16),
        scratch_shapes=scratch,
        compiler_params=pltpu.CompilerParams(dimension_semantics=("arbitrary",), vmem_limit_bytes=VMEM_LIMIT),
        name=name,
    )(x3, *consts)


def _gla_scratch(batch, proj_width, key_width):
    rows = batch * CHUNK
    w = BRANCH_WIDTH
    return [pltpu.VMEM((N_HEADS * batch, HEAD_DIM, HEAD_DIM), F32),
            pltpu.VMEM((2, rows, key_width), BF16), pltpu.VMEM((2, rows, key_width), BF16),
            pltpu.VMEM((2, rows, w), BF16),
            pltpu.VMEM((2, rows, w), F32),
            pltpu.VMEM((2, batch, 2 * CHUNK, key_width), BF16),
            pltpu.VMEM((2, batch, 2 * CHUNK, key_width), F32),
            pltpu.SMEM((2,), jnp.int32),
            pltpu.VMEM((rows, proj_width), F32)]


def _dn_scratch(batch):
    rows = batch * CHUNK
    prob = (2, N_HEADS * batch, CHUNK, HEAD_DIM)
    return [pltpu.VMEM((N_HEADS * batch, HEAD_DIM, HEAD_DIM), F32),
            pltpu.VMEM((batch, CONV_PAD, 3 * BRANCH_WIDTH), F32),
            pltpu.VMEM((batch, CHUNK + CONV_PAD, 3 * BRANCH_WIDTH), F32),
            pltpu.VMEM(prob, F32), pltpu.VMEM(prob, F32), pltpu.VMEM(prob, F32),
            pltpu.VMEM((2, rows, BRANCH_WIDTH), F32),
            pltpu.VMEM((2, 2, rows, LANES), F32)]


def _params2():
    return pltpu.CompilerParams(dimension_semantics=("arbitrary", "arbitrary"), vmem_limit_bytes=VMEM_LIMIT)


def _merge_call(x2, batch, consts_a, branches, consts_b):
    rows = x2.shape[0]
    steps = rows // batch // TILE_MERGE
    in_specs = ([_row_spec(TILE_MERGE, D_MODEL, steps)] + [_const_spec(c.shape, 2) for c in consts_a]
                + [_row_spec(TILE_MERGE, BRANCH_WIDTH, steps) for _ in branches]
                + [_const_spec(c.shape, 2) for c in consts_b])
    return pl.pallas_call(
        _merge_kernel,
        grid=(batch, steps),
        in_specs=in_specs,
        out_specs=_row_spec(TILE_MERGE, D_MODEL, steps),
        out_shape=jax.ShapeDtypeStruct((rows, D_MODEL), F32),
        compiler_params=_params2(),
        name="merge",
    )(x2, *consts_a, *branches, *consts_b)


def _ffn_call(x2, batch, consts, final_norm):
    rows = x2.shape[0]
    steps = rows // batch // TILE_FFN
    in_specs = [_row_spec(TILE_FFN, D_MODEL, steps)] + [_const_spec(c.shape, 2) for c in consts]
    return pl.pallas_call(
        functools.partial(_ffn_kernel, final_norm=final_norm),
        grid=(batch, steps),
        in_specs=in_specs,
        out_specs=_row_spec(TILE_FFN, D_MODEL, steps),
        out_shape=jax.ShapeDtypeStruct((rows, D_MODEL), F32),
        scratch_shapes=[pltpu.VMEM((TILE_FFN + CONV_PAD, 2 * FFN_HIDDEN), F32),
                        pltpu.VMEM((CONV_PAD, 2 * FFN_HIDDEN), F32),
                        pltpu.VMEM((TILE_FFN, FFN_HIDDEN), BF16)],
        compiler_params=_params2(),
        name="ffn_final" if final_norm else "ffn",
    )(x2, *consts)


def _pad_cols(wcols, width):
    return jnp.pad(wcols, [(0, 0)] * (wcols.ndim - 1) + [(0, width - wcols.shape[-1])])


def _row(v):
    return v.reshape(1, -1).astype(F32)


def kernel(x, norm_mix_g, w_in, hg_lower_bounds, hg_norm_g, gla_gk_w2, gla_gk_b, gla_norm_g, dn_conv_w,
           dn_A_log, dn_dt_bias, dn_norm_g, w_branch, gate_b, w_out, norm_ffn_g, w_up, ffn_conv_w,
           ffn_conv_b, w_down, norm_final_g):
    batch, seq, d = x.shape
    depth = w_in.shape[0]
    assert d == D_MODEL and seq % max(CHUNK, TILE_MERGE, TILE_FFN) == 0

    lb_all = jnp.cumsum(jax.nn.softmax(hg_lower_bounds.astype(F32), axis=0), axis=0)
    lb_all = lb_all - lb_all[:1]
    edge = jnp.asarray(_edge_table(), BF16)
    level = jnp.asarray(_level_table(), BF16)
    lmask = jnp.asarray(_level_masks(), F32)
    tri1 = np.tril(np.ones((CHUNK, CHUNK), np.float32))
    tri = jnp.asarray(np.concatenate([tri1, tri1], axis=1), BF16)

    hw = BRANCH_WIDTH
    c0 = 4 * hw
    c1 = c0 + 2 * N_HEADS * GLA_DK
    c2 = c1 + hw
    c3 = c2 + GLA_RANK
    c4 = c3 + hw
    c5 = c4 + 3 * hw
    c6 = c5 + hw
    c7 = c6 + N_HEADS
    c8 = c7 + N_HEADS

    x3 = x
    for l in range(depth):
        wl = w_in[l]
        ng = _row(norm_mix_g[l])
        lb = lb_all[l]
        lb_rows = jnp.stack([lb, jnp.log(lb), jnp.log1p(-lb)]).astype(F32)
        o_hg = _mixer_call(
            functools.partial(_gla_kernel, "hgrn2", 1), "hgrn2", x3,
            [ng, wl[:, :c0].astype(BF16), lb_rows, _row(jnp.tile(hg_norm_g[l], N_HEADS)), edge, level, lmask],
            _gla_scratch(batch, c0, hw))

        w_gla = jnp.concatenate([wl[:, c0:c2], wl[:, c3:c4], _pad_cols(wl[:, c2:c3], LANES)], axis=1).astype(BF16)
        w2 = jnp.pad(gla_gk_w2[l], ((0, LANES - GLA_RANK), (0, 0))).astype(BF16)
        b2 = _row(gla_gk_b[l])
        o_gla = _mixer_call(
            functools.partial(_gla_kernel, "gla", 2), "gla", x3,
            [ng, w_gla, w2, b2, _row(jnp.tile(gla_norm_g[l], N_HEADS)), edge, level, lmask],
            _gla_scratch(batch, w_gla.shape[1], N_HEADS * GLA_DK))

        w_dn = jnp.concatenate([wl[:, c4:c6], _pad_cols(wl[:, c6:c8], LANES)], axis=1).astype(BF16)
        ab = jnp.zeros((2, LANES), F32)
        ab = ab.at[0, N_HEADS:2 * N_HEADS].set(dn_A_log[l].astype(F32))
        ab = ab.at[1, N_HEADS:2 * N_HEADS].set(dn_dt_bias[l].astype(F32))
        o_dn = _mixer_call(
            _dn_kernel, "deltanet", x3,
            [ng, w_dn, dn_conv_w[l].astype(F32), ab, _row(jnp.tile(dn_norm_g[l], N_HEADS)), tri],
            _dn_scratch(batch))

        x2 = _merge_call(
            x3.reshape(batch * seq, d), batch, [ng, wl[:, c8:].astype(BF16), _row(gate_b[l])],
            [o.reshape(batch * seq, hw) for o in (o_hg, o_gla, o_dn)],
            [w_branch[l].astype(BF16), w_out[l].astype(BF16)])

        x2 = _ffn_call(
            x2, batch,
            [_row(norm_ffn_g[l]), w_up[l].astype(BF16), ffn_conv_w[l].astype(F32), _row(ffn_conv_b[l]),
             w_down[l].astype(BF16), _row(norm_final_g)],
            final_norm=(l == depth - 1))
        x3 = x2.reshape(batch, seq, d)
    return x3
```

```python
import functools

import jax
import jax.numpy as jnp
import numpy as np
from jax import lax
from jax.experimental import pallas as pl
from jax.experimental.pallas import tpu as pltpu

F32 = jnp.float32
BF16 = jnp.bfloat16

D_MODEL = 1024
CHUNK = 64
NORM_EPS = 1e-6
N_HEADS = 4
HEAD_DIM = 128
BRANCH_WIDTH = N_HEADS * HEAD_DIM
GLA_DK = 64
GLA_RANK = 16
GLA_GATE_NORMALIZER = 16.0
DN_LANE0 = GLA_RANK
DN_CONV = 4
N_BRANCH = 3
FFN_HIDDEN = 2816
FFN_CONV = 3
LANES = 128
CONV_PAD = 8
N_LEVELS = 6
LOG2E = 1.4426950408889634
SAFE_LOG2_DECAY = -200.0

TILE_MERGE = 512
TILE_FFN = 512
VMEM_LIMIT = 56 * 1024 * 1024


def _sigmoid(x):
    return 1.0 / (1.0 + jnp.exp(-x))


def _silu(x):
    return x * _sigmoid(x)


def _softplus(x):
    return jnp.maximum(x, 0.0) + jnp.log1p(jnp.exp(-jnp.abs(x)))


def _log_sigmoid(x):
    return -_softplus(-x)


def _rms_norm(x, g):
    y = x * lax.rsqrt(jnp.mean(x * x, axis=-1, keepdims=True) + NORM_EPS)
    return y * g


def _dot(a, b):
    return jnp.dot(a, b, preferred_element_type=F32)


def _bdot(a, b):
    return lax.dot_general(a, b, (((2,), (1,)), ((0,), (0,))), preferred_element_type=F32)


def _bdot_nt(a, b):
    return lax.dot_general(a, b, (((2,), (2,)), ((0,), (0,))), preferred_element_type=F32)


def _bdot_tn(a, b):
    return lax.dot_general(a, b, (((1,), (1,)), ((0,), (0,))), preferred_element_type=F32)


def _split_bf16(x):
    hi = x.astype(BF16)
    lo = (x - hi.astype(F32)).astype(BF16)
    return hi, lo


def _head(h):
    return slice(h * HEAD_DIM, (h + 1) * HEAD_DIM)


def _heads_to_problems(x3):
    return jnp.concatenate([x3[:, :, _head(h)] for h in range(N_HEADS)], axis=0)


def _to_problems(x2d, nb):
    return _heads_to_problems(x2d.reshape(nb, x2d.shape[0] // nb, x2d.shape[1]))


def _key_problems(x, nb, query=False):
    x3 = x if x.ndim == 3 else x.reshape(nb, x.shape[0] // nb, x.shape[1])
    if x3.shape[-1] == BRANCH_WIDTH:
        return _heads_to_problems(x3)
    lane = lax.broadcasted_iota(jnp.int32, (1, 1, LANES), 2)
    parts = []
    for h in range(N_HEADS):
        xp = x3[:, :, (h // 2) * LANES:(h // 2 + 1) * LANES]
        if query:
            own = (lane >= LANES // 2) if h % 2 else (lane < LANES // 2)
            xp = jnp.where(own, xp, jnp.zeros_like(xp))
        parts.append(xp)
    return jnp.concatenate(parts, axis=0)


def _from_problems(xp, nb):
    rows = nb * xp.shape[1]
    return jnp.concatenate([xp[h * nb:(h + 1) * nb].reshape(rows, HEAD_DIM) for h in range(N_HEADS)], axis=-1)


def _run_interleaved(*stage_generators):
    live = list(stage_generators)
    while live:
        live = [s for s in live if next(s, True) is None]


def _causal_masks():
    row = lax.broadcasted_iota(jnp.int32, (CHUNK, CHUNK), 0)
    col = lax.broadcasted_iota(jnp.int32, (CHUNK, CHUNK), 1)
    return row, col, (row >= col)[None], (row > col)[None]


def _prefix_suffix_rows():
    t = np.arange(CHUNK)[:, None]
    u = np.arange(CHUNK)[None, :]
    return [u <= t, u > t]


def _edge_table():
    tab = np.concatenate(_prefix_suffix_rows(), axis=0).astype(np.float32)
    return np.concatenate([tab, tab], axis=1)


def _level_table():
    t = np.arange(CHUNK)
    u = np.arange(CHUNK)
    rows = []
    for lvl in range(N_LEVELS):
        p = CHUNK >> (lvl + 1)
        mid = (t // (2 * p)) * (2 * p) + p - 1
        upper = t > mid
        rows.append(np.where(upper[:, None], (u[None, :] > mid[:, None]) & (u[None, :] <= t[:, None]),
                             (u[None, :] > t[:, None]) & (u[None, :] <= mid[:, None])))
    tab = np.concatenate(rows, axis=0).astype(np.float32)
    return np.concatenate([tab, tab], axis=1)


def _level_masks():
    t = np.arange(CHUNK)[:, None]
    s = np.arange(CHUNK)[None, :]
    out = []
    for lvl in range(N_LEVELS):
        p = CHUNK >> (lvl + 1)
        out.append((t // (2 * p) == s // (2 * p)) & (t % (2 * p) >= p) & (s % (2 * p) < p))
    out.append(t == s)
    return np.stack(out).astype(np.float32)


def _gated_head_norm(o, zg, g):
    outs = []
    for h in range(N_HEADS):
        oh = o[:, _head(h)]
        inv = lax.rsqrt(jnp.mean(oh * oh, axis=-1, keepdims=True) + NORM_EPS)
        outs.append(oh * inv)
    return jnp.concatenate(outs, axis=-1) * g * zg


def _zero(*refs):
    for r in refs:
        r[...] = jnp.zeros(r.shape, r.dtype)


def _gla_front(kind, x_ref, ng_ref, w_ref, aux_refs, edge_ref, slot, q_s, k_s, v_s, zg_s, gs_s, tab_s, safe_s, p_s):
    nb, _, d = x_ref.shape
    rows = nb * CHUNK
    w = BRANCH_WIDTH
    h = _rms_norm(x_ref[...].reshape(rows, d), ng_ref[...]).astype(BF16)
    yield

    def project(c0, c1):
        p_s[:, c0:c1] = _dot(h, w_ref[:, c0:c1])

    def store_decay(g):
        g_hi, g_lo = _split_bf16(g * LOG2E)
        for b in range(nb):
            r = slice(b * CHUNK, (b + 1) * CHUNK)
            gs_s[slot, b] = jnp.concatenate([g_hi[r], g_lo[r]], axis=0)

    if kind == "hgrn2":
        lb_ref, = aux_refs
        for c0 in range(0, 4 * w, w):
            project(c0, c0 + w)
            yield
        q_s[slot] = (_silu(p_s[:, 0:w]) * (HEAD_DIM ** -0.5)).astype(BF16)
        yield
        z = p_s[:, w:2 * w]
        lb = lb_ref[0:1, :]
        log_lb = lb_ref[1:2, :]
        c = lb_ref[2:3, :] + _log_sigmoid(z)
        store_decay(jnp.maximum(log_lb, c) + jnp.log1p(jnp.exp(-jnp.abs(log_lb - c))))
        k_s[slot] = ((1.0 - lb) * _sigmoid(-z)).astype(BF16)
        yield
        v_s[slot] = p_s[:, 2 * w:3 * w].astype(BF16)
        zg_s[slot] = _silu(p_s[:, 3 * w:4 * w])
        yield
    else:
        w2_ref, b2_ref, side_ref = aux_refs
        kw = q_s.shape[-1]
        low = 2 * kw + 2 * w
        project(low, low + LANES)
        side_ref[...] = p_s[:, low:low + LANES].reshape(side_ref.shape)
        yield
        project(0, 2 * kw)
        yield
        gk = _dot(p_s[:, low:low + LANES].astype(BF16), w2_ref[...]) + b2_ref[...]
        store_decay(_log_sigmoid(gk) * (1.0 / GLA_GATE_NORMALIZER))
        yield
        project(2 * kw, 2 * kw + w)
        yield
    edge = edge_ref[...]
    low_mark = None
    for b in range(nb):
        tab = _dot(edge, gs_s[slot, b])
        tab_s[slot, b] = tab
        total = tab[CHUNK - 1:CHUNK, :]
        low_mark = total if low_mark is None else jnp.minimum(low_mark, total)
        if b % 4 == 3:
            yield
    safe_s[slot] = (jnp.min(low_mark) >= SAFE_LOG2_DECAY).astype(jnp.int32)
    if kind != "hgrn2":
        project(2 * kw + w, 2 * kw + 2 * w)
        yield
        q_s[slot] = (p_s[:, 0:kw] * (GLA_DK ** -0.5)).astype(BF16)
        k_s[slot] = p_s[:, kw:2 * kw].astype(BF16)
        yield
        v_s[slot] = p_s[:, 2 * kw:2 * kw + w].astype(BF16)
        zg_s[slot] = _silu(p_s[:, 2 * kw + w:2 * kw + 2 * w])


def _gla_scores_centered(q, k, tab, nb):
    _, _, incl, _ = _causal_masks()
    pre = tab[:, 0:CHUNK, :]
    mid = 0.5 * pre[:, CHUNK - 1:CHUNK, :]
    rows = nb * CHUNK
    qf = q * jnp.exp2(pre - mid).astype(BF16).reshape(rows, -1)
    kf = k * jnp.exp2(mid - pre).astype(BF16).reshape(rows, -1)
    yield
    s = _bdot_nt(_key_problems(qf, nb, query=True), _key_problems(kf, nb))
    yield jnp.where(incl, s, 0.0)


def _gla_scores_levels(q, k, gs, level_ref, lmask_ref, nb):
    sc = _bdot_nt(_key_problems(q, nb, query=True), _key_problems(k, nb)) * lmask_ref[N_LEVELS][None]
    for l in range(N_LEVELS):
        wt = level_ref[l * CHUNK:(l + 1) * CHUNK, :]
        e = jnp.exp2(jnp.concatenate([_dot(wt, gs[b]) for b in range(nb)], axis=0)).astype(BF16)
        sc = sc + _bdot_nt(_key_problems(q * e, nb, query=True), _key_problems(k * e, nb)) * lmask_ref[l][None]
    return sc


def _gla_back(centered, slot, q_s, k_s, v_s, zg_s, gs_s, tab_s, st_ref, hn_ref, level_ref, lmask_ref, o_ref):
    nb = o_ref.shape[0]
    rows = nb * CHUNK
    q = q_s[slot]
    k = k_s[slot]
    tab = tab_s[slot]
    if centered:
        scores = _gla_scores_centered(q, k, tab, nb)
        yield next(scores)
        sc = next(scores)
    else:
        sc = _gla_scores_levels(q, k, [gs_s[slot, b] for b in range(nb)], level_ref, lmask_ref, nb)
    yield
    pre = tab[:, 0:CHUNK, :]
    e_in = jnp.exp2(pre).astype(BF16).reshape(rows, -1)
    e_out = jnp.exp2(tab[:, CHUNK:2 * CHUNK, :]).astype(BF16).reshape(rows, -1)
    e_last = _key_problems(jnp.exp2(pre[:, CHUNK - 1:CHUNK, :]), nb)
    v_p = _to_problems(v_s[slot], nb)
    st = st_ref[...]
    o = _bdot(sc.astype(BF16), v_p) + _bdot_nt(_key_problems(q * e_in, nb, query=True), st.astype(BF16))
    yield
    st_ref[...] = st * e_last + _bdot_tn(v_p, _key_problems(k * e_out, nb))
    yield
    y = _gated_head_norm(_from_problems(o, nb), zg_s[slot], hn_ref[...])
    o_ref[...] = y.reshape(o_ref.shape).astype(o_ref.dtype)


def _gla_kernel(kind, n_aux, x_ref, ng_ref, w_ref, *rest):
    aux_refs, rest = rest[:n_aux], rest[n_aux:]
    (hn_ref, edge_ref, level_ref, lmask_ref, o_ref), rest = rest[:5], rest[5:]
    if kind != "hgrn2":
        aux_refs, rest = aux_refs + rest[:1], rest[1:]
    st_ref, q_s, k_s, v_s, zg_s, gs_s, tab_s, safe_s, p_s = rest
    j = pl.program_id(0)
    slot = j % 2
    prev = 1 - slot
    bufs = (q_s, k_s, v_s, zg_s, gs_s, tab_s)

    @pl.when(j == 0)
    def _():
        _zero(st_ref, *bufs)
        safe_s[0] = 1
        safe_s[1] = 1

    def front():
        return _gla_front(kind, x_ref, ng_ref, w_ref, aux_refs, edge_ref, slot, *bufs, safe_s, p_s)

    def back(centered):
        return _gla_back(centered, prev, *bufs, st_ref, hn_ref, level_ref, lmask_ref, o_ref)

    @pl.when(safe_s[prev] == 1)
    def _():
        _run_interleaved(front(), back(True))

    @pl.when(safe_s[prev] != 1)
    def _():
        _run_interleaved(back(False))
        _run_interleaved(front())


def _dn_front(x_ref, side_ref, ng_ref, w_ref, cw_ref, ab_ref, slot, hist_s, ext_s, q_s, k_s, v_s, zg_s, bg_s):
    nb, _, d = x_ref.shape
    rows = nb * CHUNK
    w = BRANCH_WIDTH
    h = _rms_norm(x_ref[...].reshape(rows, d), ng_ref[...]).astype(BF16)
    yield
    for part, dst in enumerate((q_s, k_s, v_s)):
        cols = slice(part * w, (part + 1) * w)
        ext_s[:, 0:CONV_PAD, cols] = hist_s[:, :, cols]
        ext_s[:, CONV_PAD:CONV_PAD + CHUNK, cols] = _dot(h, w_ref[:, cols]).reshape(nb, CHUNK, w)
        hist_s[:, :, cols] = ext_s[:, CHUNK:CHUNK + CONV_PAD, cols]
        yield
        ext = ext_s[:, :, cols]
        conv = cw_ref[DN_CONV - 1:DN_CONV, cols][None] * ext[:, CONV_PAD:CONV_PAD + CHUNK]
        for back in range(1, DN_CONV):
            tap = DN_CONV - 1 - back
            shifted = pltpu.roll(ext, back, axis=1)
            conv = conv + cw_ref[tap:tap + 1, cols][None] * shifted[:, CONV_PAD:CONV_PAD + CHUNK]
        act = _silu(conv.reshape(rows, w))
        if part < 2:
            scale = HEAD_DIM ** -0.5 if part == 0 else 1.0
            heads = []
            for hd in range(N_HEADS):
                xh = act[:, _head(hd)]
                heads.append(xh * lax.rsqrt(jnp.sum(xh * xh, axis=-1, keepdims=True) + NORM_EPS) * scale)
            act = jnp.concatenate(heads, axis=-1)
        dst[slot] = _to_problems(act, nb)
        yield
    zg_s[slot] = _silu(_dot(h, w_ref[:, 3 * w:4 * w]))
    yield
    ba = side_ref[...].reshape(rows, LANES)
    bg_s[slot, 0] = _sigmoid(ba)
    bg_s[slot, 1] = -jnp.exp(ab_ref[0:1, :]) * _softplus(ba + ab_ref[1:2, :])


def _dn_back(slot, q_s, k_s, v_s, zg_s, bg_s, st_ref, hn_ref, tri_ref, o_ref):
    nb = o_ref.shape[0]
    q_p = q_s[slot]
    k_p = k_s[slot]
    v_p = v_s[slot]
    beta = bg_s[slot, 0]
    g_hi, g_lo = _split_bf16(bg_s[slot, 1])
    tri = tri_ref[...]
    gcols, grows = [], []
    for b in range(nb):
        r = slice(b * CHUNK, (b + 1) * CHUNK)
        gcum = _dot(tri, jnp.concatenate([g_hi[r], g_lo[r]], axis=0))
        gcols.append(gcum)
        grows.append(gcum.T)
    ln = lambda h: slice(DN_LANE0 + N_HEADS + h, DN_LANE0 + N_HEADS + h + 1)
    gc = jnp.concatenate([gcols[b][None, :, ln(h)] for h in range(N_HEADS) for b in range(nb)], axis=0)
    gr = jnp.concatenate([grows[b][None, ln(h), :] for h in range(N_HEADS) for b in range(nb)], axis=0)
    beta3 = beta.reshape(nb, CHUNK, LANES)
    bc = jnp.concatenate([beta3[:, :, DN_LANE0 + h:DN_LANE0 + h + 1] for h in range(N_HEADS)], axis=0)

    row, col, incl, strict = _causal_masks()
    decay = jnp.exp(jnp.where(incl, gc - gr, -jnp.inf))
    g_last = gc[:, CHUNK - 1:CHUNK, :]
    e_in = jnp.exp(gc)
    e_out = jnp.exp(g_last - gc)
    e_last = jnp.exp(g_last)

    kb = k_p * bc
    kq = _bdot_nt(jnp.concatenate([kb, q_p], axis=1).astype(BF16), k_p.astype(BF16))
    yield
    m = jnp.where(strict, kq[:, 0:CHUNK] * decay, 0.0)
    sc = jnp.where(incl, kq[:, CHUNK:2 * CHUNK] * decay, 0.0).astype(BF16)
    eye = (row == col).astype(F32)[None]
    t = None
    blk = 1
    while blk < CHUNK:
        sel = ((row // (2 * blk) == col // (2 * blk)) & (row % (2 * blk) >= blk) & (col % (2 * blk) < blk))[None]
        off = jnp.where(sel, m, 0.0)
        if blk == 1:
            t = eye - off
        else:
            t_bf = t.astype(BF16)
            half = _bdot(t_bf, off.astype(BF16)).astype(BF16)
            yield
            t = t - _bdot(half, t_bf)
            yield
        blk *= 2
    rhs = jnp.concatenate([v_p * bc, kb * e_in], axis=-1).astype(BF16)
    uw = _bdot(t.astype(BF16), rhs).astype(BF16)
    yield
    sc_uw = _bdot(sc, uw)
    k_uw = _bdot_tn((k_p * e_out).astype(BF16), uw)
    yield
    q_eff = (q_p * e_in - sc_uw[:, :, HEAD_DIM:]).astype(BF16)
    st = st_ref[...]
    st_bf = st.astype(BF16)
    o = _bdot(q_eff, st_bf) + sc_uw[:, :, 0:HEAD_DIM]
    st_ref[...] = st * e_last + k_uw[:, :, 0:HEAD_DIM] - _bdot(k_uw[:, :, HEAD_DIM:].astype(BF16), st_bf)
    yield
    y = _gated_head_norm(_from_problems(o, nb), zg_s[slot], hn_ref[...])
    o_ref[...] = y.reshape(o_ref.shape).astype(o_ref.dtype)


def _dn_kernel(x_ref, side_ref, ng_ref, w_ref, cw_ref, ab_ref, hn_ref, tri_ref, o_ref,
               st_ref, hist_s, ext_s, q_s, k_s, v_s, zg_s, bg_s):
    j = pl.program_id(0)
    slot = j % 2
    bufs = (q_s, k_s, v_s, zg_s, bg_s)

    @pl.when(j == 0)
    def _():
        _zero(st_ref, hist_s, *bufs)

    _run_interleaved(
        _dn_front(x_ref, side_ref, ng_ref, w_ref, cw_ref, ab_ref, slot, hist_s, ext_s, *bufs),
        _dn_back(1 - slot, *bufs, st_ref, hn_ref, tri_ref, o_ref))


def _merge_kernel(x_ref, ng_ref, wg_ref, gb_ref, oa_ref, ob_ref, oc_ref, wb_ref, wo_ref, y_ref):
    x = x_ref[...]
    h = _rms_norm(x, ng_ref[...]).astype(BF16)
    gates = _sigmoid(_dot(h, wg_ref[...]) + gb_ref[...])
    merged = None
    for n, o_n in enumerate((oa_ref, ob_ref, oc_ref)):
        term = gates[:, n * D_MODEL:(n + 1) * D_MODEL] * _dot(o_n[...], wb_ref[n])
        merged = term if merged is None else merged + term
    y_ref[...] = x + _dot(merged.astype(BF16), wo_ref[...])


def _ffn_kernel(x_ref, ng_ref, wu_ref, cw_ref, cb_ref, wd_ref, fg_ref, y_ref, ext_s, hist_s, act_s,
                *, final_norm):
    @pl.when(pl.program_id(1) == 0)
    def _():
        hist_s[...] = jnp.zeros(hist_s.shape, hist_s.dtype)

    tile = x_ref.shape[0]
    x = x_ref[...]
    h = _rms_norm(x, ng_ref[...]).astype(BF16)
    ext_s[0:CONV_PAD, :] = hist_s[...]
    ext_s[CONV_PAD:CONV_PAD + tile, :] = _dot(h, wu_ref[...])
    hist_s[...] = ext_s[tile:tile + CONV_PAD, :]
    blk = 2 * LANES
    split = (FFN_HIDDEN // blk // 2 + 1) * blk
    y = x
    for j0 in range(0, FFN_HIDDEN, blk):
        halves = []
        for base in (j0, FFN_HIDDEN + j0):
            cols = slice(base, base + blk)
            u = cb_ref[0:1, cols] + cw_ref[0:1, cols] * ext_s[CONV_PAD - 2:CONV_PAD - 2 + tile, cols]
            for j in range(1, FFN_CONV):
                u = u + cw_ref[j:j + 1, cols] * ext_s[CONV_PAD - 2 + j:CONV_PAD - 2 + j + tile, cols]
            halves.append(u)
        act_s[:, j0:j0 + blk] = (_silu(halves[0]) * halves[1]).astype(BF16)
        if j0 + blk == split:
            y = y + _dot(act_s[:, 0:split], wd_ref[0:split, :])
    y = y + _dot(act_s[:, split:], wd_ref[split:, :])
    if final_norm:
        y = _rms_norm(y, fg_ref[...])
    y_ref[...] = y


def _const_spec(shape, grid_rank):
    nd = len(shape)
    if grid_rank == 1:
        return pl.BlockSpec(shape, lambda i, _nd=nd: (0,) * _nd, pipeline_mode=pl.Buffered(1))
    return pl.BlockSpec(shape, lambda b, i, _nd=nd: (0,) * _nd, pipeline_mode=pl.Buffered(1))


def _row_spec(tile, width, steps):
    return pl.BlockSpec((tile, width), lambda b, i, _s=steps: (b * _s + i, 0))


def _mixer_call(body, name, x3, consts, scratch, side_in=None, side_out=False):
    batch, seq, _ = x3.shape
    n = seq // CHUNK
    front = lambda j: (0, jnp.minimum(j, n - 1), 0)
    back = lambda j: (0, jnp.maximum(j - 1, 0), 0)
    side_spec = pl.BlockSpec((batch, CHUNK, LANES), front)
    streamed = [x3] + ([side_in] if side_in is not None else [])
    in_specs = ([pl.BlockSpec((batch, CHUNK, D_MODEL), front)] + [side_spec] * (len(streamed) - 1)
                + [_const_spec(c.shape, 1) for c in consts])
    out_specs = [pl.BlockSpec((batch, CHUNK, BRANCH_WIDTH), back)]
    out_shape = [jax.ShapeDtypeStruct((batch, seq, BRANCH_WIDTH), BF16)]
    if side_out:
        out_specs.append(side_spec)
        out_shape.append(jax.ShapeDtypeStruct((batch, seq, LANES), F32))
    outs = pl.pallas_call(
        body,
        grid=(n + 1,),
        in_specs=in_specs,
        out_specs=out_specs,
        out_shape=out_shape,
        scratch_shapes=scratch,
        compiler_params=pltpu.CompilerParams(dimension_semantics=("arbitrary",), vmem_limit_bytes=VMEM_LIMIT),
        name=name,
    )(*streamed, *consts)
    return outs if side_out else outs[0]


def _gla_scratch(batch, proj_width, key_width):
    rows = batch * CHUNK
    w = BRANCH_WIDTH
    return [pltpu.VMEM((N_HEADS * batch, HEAD_DIM, HEAD_DIM), F32),
            pltpu.VMEM((2, rows, key_width), BF16), pltpu.VMEM((2, rows, key_width), BF16),
            pltpu.VMEM((2, rows, w), BF16),
            pltpu.VMEM((2, rows, w), F32),
            pltpu.VMEM((2, batch, 2 * CHUNK, key_width), BF16),
            pltpu.VMEM((2, batch, 2 * CHUNK, key_width), F32),
            pltpu.SMEM((2,), jnp.int32),
            pltpu.VMEM((rows, proj_width), F32)]


def _dn_scratch(batch):
    rows = batch * CHUNK
    prob = (2, N_HEADS * batch, CHUNK, HEAD_DIM)
    return [pltpu.VMEM((N_HEADS * batch, HEAD_DIM, HEAD_DIM), F32),
            pltpu.VMEM((batch, CONV_PAD, 3 * BRANCH_WIDTH), F32),
            pltpu.VMEM((batch, CHUNK + CONV_PAD, 3 * BRANCH_WIDTH), F32),
            pltpu.VMEM(prob, F32), pltpu.VMEM(prob, F32), pltpu.VMEM(prob, F32),
            pltpu.VMEM((2, rows, BRANCH_WIDTH), F32),
            pltpu.VMEM((2, 2, rows, LANES), F32)]


def _params2():
    return pltpu.CompilerParams(dimension_semantics=("arbitrary", "arbitrary"), vmem_limit_bytes=VMEM_LIMIT)


def _merge_call(x2, batch, consts_a, branches, consts_b):
    rows = x2.shape[0]
    steps = rows // batch // TILE_MERGE
    in_specs = ([_row_spec(TILE_MERGE, D_MODEL, steps)] + [_const_spec(c.shape, 2) for c in consts_a]
                + [_row_spec(TILE_MERGE, BRANCH_WIDTH, steps) for _ in branches]
                + [_const_spec(c.shape, 2) for c in consts_b])
    return pl.pallas_call(
        _merge_kernel,
        grid=(batch, steps),
        in_specs=in_specs,
        out_specs=_row_spec(TILE_MERGE, D_MODEL, steps),
        out_shape=jax.ShapeDtypeStruct((rows, D_MODEL), F32),
        compiler_params=_params2(),
        name="merge",
    )(x2, *consts_a, *branches, *consts_b)


def _ffn_call(x2, batch, consts, final_norm):
    rows = x2.shape[0]
    steps = rows // batch // TILE_FFN
    in_specs = [_row_spec(TILE_FFN, D_MODEL, steps)] + [_const_spec(c.shape, 2) for c in consts]
    return pl.pallas_call(
        functools.partial(_ffn_kernel, final_norm=final_norm),
        grid=(batch, steps),
        in_specs=in_specs,
        out_specs=_row_spec(TILE_FFN, D_MODEL, steps),
        out_shape=jax.ShapeDtypeStruct((rows, D_MODEL), F32),
        scratch_shapes=[pltpu.VMEM((TILE_FFN + CONV_PAD, 2 * FFN_HIDDEN), F32),
                        pltpu.VMEM((CONV_PAD, 2 * FFN_HIDDEN), F32),
                        pltpu.VMEM((TILE_FFN, FFN_HIDDEN), BF16)],
        compiler_params=_params2(),
        name="ffn_final" if final_norm else "ffn",
    )(x2, *consts)


def _pad_cols(wcols, width):
    return jnp.pad(wcols, [(0, 0)] * (wcols.ndim - 1) + [(0, width - wcols.shape[-1])])


def _row(v):
    return v.reshape(1, -1).astype(F32)


def kernel(x, norm_mix_g, w_in, hg_lower_bounds, hg_norm_g, gla_gk_w2, gla_gk_b, gla_norm_g, dn_conv_w,
           dn_A_log, dn_dt_bias, dn_norm_g, w_branch, gate_b, w_out, norm_ffn_g, w_up, ffn_conv_w,
           ffn_conv_b, w_down, norm_final_g):
    batch, seq, d = x.shape
    depth = w_in.shape[0]
    assert d == D_MODEL and seq % max(CHUNK, TILE_MERGE, TILE_FFN) == 0

    lb_all = jnp.cumsum(jax.nn.softmax(hg_lower_bounds.astype(F32), axis=0), axis=0)
    lb_all = lb_all - lb_all[:1]
    edge = jnp.asarray(_edge_table(), BF16)
    level = jnp.asarray(_level_table(), BF16)
    lmask = jnp.asarray(_level_masks(), F32)
    tri1 = np.tril(np.ones((CHUNK, CHUNK), np.float32))
    tri = jnp.asarray(np.concatenate([tri1, tri1], axis=1), BF16)

    hw = BRANCH_WIDTH
    c0 = 4 * hw
    c1 = c0 + 2 * N_HEADS * GLA_DK
    c2 = c1 + hw
    c3 = c2 + GLA_RANK
    c4 = c3 + hw
    c5 = c4 + 3 * hw
    c6 = c5 + hw
    c7 = c6 + N_HEADS
    c8 = c7 + N_HEADS

    x3 = x
    for l in range(depth):
        wl = w_in[l]
        ng = _row(norm_mix_g[l])
        lb = lb_all[l]
        lb_rows = jnp.stack([lb, jnp.log(lb), jnp.log1p(-lb)]).astype(F32)
        o_hg = _mixer_call(
            functools.partial(_gla_kernel, "hgrn2", 1), "hgrn2", x3,
            [ng, wl[:, :c0].astype(BF16), lb_rows, _row(jnp.tile(hg_norm_g[l], N_HEADS)), edge, level, lmask],
            _gla_scratch(batch, c0, hw))

        gate_cols = _pad_cols(jnp.concatenate([wl[:, c2:c3], wl[:, c6:c8]], axis=1), LANES)
        w_gla = jnp.concatenate([wl[:, c0:c2], wl[:, c3:c4], gate_cols], axis=1).astype(BF16)
        w2 = jnp.pad(gla_gk_w2[l], ((0, LANES - GLA_RANK), (0, 0))).astype(BF16)
        b2 = _row(gla_gk_b[l])
        o_gla, dn_gates = _mixer_call(
            functools.partial(_gla_kernel, "gla", 2), "gla", x3,
            [ng, w_gla, w2, b2, _row(jnp.tile(gla_norm_g[l], N_HEADS)), edge, level, lmask],
            _gla_scratch(batch, w_gla.shape[1], N_HEADS * GLA_DK), side_out=True)

        ab = jnp.zeros((2, LANES), F32)
        ab = ab.at[0, DN_LANE0 + N_HEADS:DN_LANE0 + 2 * N_HEADS].set(dn_A_log[l].astype(F32))
        ab = ab.at[1, DN_LANE0 + N_HEADS:DN_LANE0 + 2 * N_HEADS].set(dn_dt_bias[l].astype(F32))
        o_dn = _mixer_call(
            _dn_kernel, "deltanet", x3,
            [ng, wl[:, c4:c6].astype(BF16), dn_conv_w[l].astype(F32), ab, _row(jnp.tile(dn_norm_g[l], N_HEADS)), tri],
            _dn_scratch(batch), side_in=dn_gates)

        x2 = _merge_call(
            x3.reshape(batch * seq, d), batch, [ng, wl[:, c8:].astype(BF16), _row(gate_b[l])],
            [o.reshape(batch * seq, hw) for o in (o_hg, o_gla, o_dn)],
            [w_branch[l].astype(BF16), w_out[l].astype(BF16)])

        x2 = _ffn_call(
            x2, batch,
            [_row(norm_ffn_g[l]), w_up[l].astype(BF16), ffn_conv_w[l].astype(F32), _row(ffn_conv_b[l]),
             w_down[l].astype(BF16), _row(norm_final_g)],
            final_norm=(l == depth - 1))
        x3 = x2.reshape(batch, seq, d)
    return x3
```

```python
import functools

import jax
import jax.numpy as jnp
import numpy as np
from jax import lax
from jax.experimental import pallas as pl
from jax.experimental.pallas import tpu as pltpu

F32 = jnp.float32
BF16 = jnp.bfloat16

D_MODEL = 1024
CHUNK = 64
NORM_EPS = 1e-6
N_HEADS = 4
HEAD_DIM = 128
BRANCH_WIDTH = N_HEADS * HEAD_DIM
GLA_DK = 64
GLA_RANK = 16
GLA_GATE_NORMALIZER = 16.0
DN_LANE0 = GLA_RANK
DN_CONV = 4
N_BRANCH = 3
FFN_HIDDEN = 2816
FFN_CONV = 3
LANES = 128
CONV_PAD = 8
N_LEVELS = 6
LOG2E = 1.4426950408889634
SAFE_LOG2_DECAY = -200.0

TILE_MERGE = 512
TILE_FFN = 512
VMEM_LIMIT = 56 * 1024 * 1024


def _sigmoid(x):
    return 1.0 / (1.0 + jnp.exp(-x))


def _silu(x):
    return x * _sigmoid(x)


def _softplus(x):
    return jnp.maximum(x, 0.0) + jnp.log1p(jnp.exp(-jnp.abs(x)))


def _log_sigmoid(x):
    return -_softplus(-x)


def _rms_norm(x, g):
    y = x * lax.rsqrt(jnp.mean(x * x, axis=-1, keepdims=True) + NORM_EPS)
    return y * g


def _dot(a, b):
    return jnp.dot(a, b, preferred_element_type=F32)


def _bdot(a, b):
    return lax.dot_general(a, b, (((2,), (1,)), ((0,), (0,))), preferred_element_type=F32)


def _bdot_nt(a, b):
    return lax.dot_general(a, b, (((2,), (2,)), ((0,), (0,))), preferred_element_type=F32)


def _bdot_tn(a, b):
    return lax.dot_general(a, b, (((1,), (1,)), ((0,), (0,))), preferred_element_type=F32)


def _split_bf16(x):
    hi = x.astype(BF16)
    lo = (x - hi.astype(F32)).astype(BF16)
    return hi, lo


def _head(h):
    return slice(h * HEAD_DIM, (h + 1) * HEAD_DIM)


def _heads_to_problems(x3):
    return jnp.concatenate([x3[:, :, _head(h)] for h in range(N_HEADS)], axis=0)


def _to_problems(x2d, nb):
    return _heads_to_problems(x2d.reshape(nb, x2d.shape[0] // nb, x2d.shape[1]))


def _key_problems(x, nb, query=False):
    x3 = x if x.ndim == 3 else x.reshape(nb, x.shape[0] // nb, x.shape[1])
    if x3.shape[-1] == BRANCH_WIDTH:
        return _heads_to_problems(x3)
    lane = lax.broadcasted_iota(jnp.int32, (1, 1, LANES), 2)
    parts = []
    for h in range(N_HEADS):
        xp = x3[:, :, (h // 2) * LANES:(h // 2 + 1) * LANES]
        if query:
            own = (lane >= LANES // 2) if h % 2 else (lane < LANES // 2)
            xp = jnp.where(own, xp, jnp.zeros_like(xp))
        parts.append(xp)
    return jnp.concatenate(parts, axis=0)


def _from_problems(xp, nb):
    rows = nb * xp.shape[1]
    return jnp.concatenate([xp[h * nb:(h + 1) * nb].reshape(rows, HEAD_DIM) for h in range(N_HEADS)], axis=-1)


def _run_interleaved(*stage_generators):
    live = list(stage_generators)
    while live:
        live = [s for s in live if next(s, True) is None]


def _causal_masks():
    row = lax.broadcasted_iota(jnp.int32, (CHUNK, CHUNK), 0)
    col = lax.broadcasted_iota(jnp.int32, (CHUNK, CHUNK), 1)
    return row, col, (row >= col)[None], (row > col)[None]


def _prefix_suffix_rows():
    t = np.arange(CHUNK)[:, None]
    u = np.arange(CHUNK)[None, :]
    return [u <= t, u > t]


def _edge_table():
    tab = np.concatenate(_prefix_suffix_rows(), axis=0).astype(np.float32)
    return np.concatenate([tab, tab], axis=1)


def _level_table():
    t = np.arange(CHUNK)
    u = np.arange(CHUNK)
    rows = []
    for lvl in range(N_LEVELS):
        p = CHUNK >> (lvl + 1)
        mid = (t // (2 * p)) * (2 * p) + p - 1
        upper = t > mid
        rows.append(np.where(upper[:, None], (u[None, :] > mid[:, None]) & (u[None, :] <= t[:, None]),
                             (u[None, :] > t[:, None]) & (u[None, :] <= mid[:, None])))
    tab = np.concatenate(rows, axis=0).astype(np.float32)
    return np.concatenate([tab, tab], axis=1)


def _level_masks():
    t = np.arange(CHUNK)[:, None]
    s = np.arange(CHUNK)[None, :]
    out = []
    for lvl in range(N_LEVELS):
        p = CHUNK >> (lvl + 1)
        out.append((t // (2 * p) == s // (2 * p)) & (t % (2 * p) >= p) & (s % (2 * p) < p))
    out.append(t == s)
    return np.stack(out).astype(np.float32)


def _gated_head_norm(o, zg, g):
    outs = []
    for h in range(N_HEADS):
        oh = o[:, _head(h)]
        inv = lax.rsqrt(jnp.mean(oh * oh, axis=-1, keepdims=True) + NORM_EPS)
        outs.append(oh * inv)
    return jnp.concatenate(outs, axis=-1) * g * zg


def _zero(*refs):
    for r in refs:
        r[...] = jnp.zeros(r.shape, r.dtype)


def _gla_front(kind, x_ref, ng_ref, w_ref, aux_refs, edge_ref, slot, q_s, k_s, v_s, zg_s, gs_s, tab_s, safe_s, p_s):
    nb, _, d = x_ref.shape
    rows = nb * CHUNK
    w = BRANCH_WIDTH
    h = _rms_norm(x_ref[...].reshape(rows, d), ng_ref[...]).astype(BF16)
    yield

    def project(c0, c1):
        p_s[:, c0:c1] = _dot(h, w_ref[:, c0:c1])

    def store_decay(g):
        g_hi, g_lo = _split_bf16(g * LOG2E)
        for b in range(nb):
            r = slice(b * CHUNK, (b + 1) * CHUNK)
            gs_s[slot, b] = jnp.concatenate([g_hi[r], g_lo[r]], axis=0)

    if kind == "hgrn2":
        lb_ref, = aux_refs
        for c0 in range(0, 4 * w, w):
            project(c0, c0 + w)
            yield
        q_s[slot] = (_silu(p_s[:, 0:w]) * (HEAD_DIM ** -0.5)).astype(BF16)
        yield
        z = p_s[:, w:2 * w]
        lb = lb_ref[0:1, :]
        log_lb = lb_ref[1:2, :]
        c = lb_ref[2:3, :] + _log_sigmoid(z)
        store_decay(jnp.maximum(log_lb, c) + jnp.log1p(jnp.exp(-jnp.abs(log_lb - c))))
        k_s[slot] = ((1.0 - lb) * _sigmoid(-z)).astype(BF16)
        yield
        v_s[slot] = p_s[:, 2 * w:3 * w].astype(BF16)
        zg_s[slot] = _silu(p_s[:, 3 * w:4 * w])
        yield
    else:
        w2_ref, b2_ref, side_ref = aux_refs
        kw = q_s.shape[-1]
        low = 2 * kw + 2 * w
        project(low, low + LANES)
        side_ref[...] = p_s[:, low:low + LANES].reshape(side_ref.shape)
        yield
        project(0, 2 * kw)
        yield
        gk = _dot(p_s[:, low:low + LANES].astype(BF16), w2_ref[...]) + b2_ref[...]
        store_decay(_log_sigmoid(gk) * (1.0 / GLA_GATE_NORMALIZER))
        yield
        project(2 * kw, 2 * kw + w)
        yield
    edge = edge_ref[...]
    low_mark = None
    for b in range(nb):
        tab = _dot(edge, gs_s[slot, b])
        tab_s[slot, b] = tab
        total = tab[CHUNK - 1:CHUNK, :]
        low_mark = total if low_mark is None else jnp.minimum(low_mark, total)
        if b % 4 == 3:
            yield
    safe_s[slot] = (jnp.min(low_mark) >= SAFE_LOG2_DECAY).astype(jnp.int32)
    if kind != "hgrn2":
        project(2 * kw + w, 2 * kw + 2 * w)
        yield
        q_s[slot] = (p_s[:, 0:kw] * (GLA_DK ** -0.5)).astype(BF16)
        k_s[slot] = p_s[:, kw:2 * kw].astype(BF16)
        yield
        v_s[slot] = p_s[:, 2 * kw:2 * kw + w].astype(BF16)
        zg_s[slot] = _silu(p_s[:, 2 * kw + w:2 * kw + 2 * w])


def _gla_scores_centered(q, k, tab, nb):
    _, _, incl, _ = _causal_masks()
    pre = tab[:, 0:CHUNK, :]
    mid = 0.5 * pre[:, CHUNK - 1:CHUNK, :]
    rows = nb * CHUNK
    qf = q * jnp.exp2(pre - mid).astype(BF16).reshape(rows, -1)
    kf = k * jnp.exp2(mid - pre).astype(BF16).reshape(rows, -1)
    yield
    s = _bdot_nt(_key_problems(qf, nb, query=True), _key_problems(kf, nb))
    yield jnp.where(incl, s, 0.0)


def _gla_scores_levels(q, k, gs, level_ref, lmask_ref, nb):
    sc = _bdot_nt(_key_problems(q, nb, query=True), _key_problems(k, nb)) * lmask_ref[N_LEVELS][None]
    for l in range(N_LEVELS):
        wt = level_ref[l * CHUNK:(l + 1) * CHUNK, :]
        e = jnp.exp2(jnp.concatenate([_dot(wt, gs[b]) for b in range(nb)], axis=0)).astype(BF16)
        sc = sc + _bdot_nt(_key_problems(q * e, nb, query=True), _key_problems(k * e, nb)) * lmask_ref[l][None]
    return sc


def _gla_back(centered, slot, q_s, k_s, v_s, zg_s, gs_s, tab_s, st_ref, hn_ref, level_ref, lmask_ref, o_ref):
    nb = o_ref.shape[0]
    rows = nb * CHUNK
    q = q_s[slot]
    k = k_s[slot]
    tab = tab_s[slot]
    if centered:
        scores = _gla_scores_centered(q, k, tab, nb)
        yield next(scores)
        sc = next(scores)
    else:
        sc = _gla_scores_levels(q, k, [gs_s[slot, b] for b in range(nb)], level_ref, lmask_ref, nb)
    yield
    pre = tab[:, 0:CHUNK, :]
    e_in = jnp.exp2(pre).astype(BF16).reshape(rows, -1)
    e_out = jnp.exp2(tab[:, CHUNK:2 * CHUNK, :]).astype(BF16).reshape(rows, -1)
    e_last = _key_problems(jnp.exp2(pre[:, CHUNK - 1:CHUNK, :]), nb)
    v_p = _to_problems(v_s[slot], nb)
    st = st_ref[...]
    o = _bdot(sc.astype(BF16), v_p) + _bdot_nt(_key_problems(q * e_in, nb, query=True), st.astype(BF16))
    yield
    st_ref[...] = st * e_last + _bdot_tn(v_p, _key_problems(k * e_out, nb))
    yield
    y = _gated_head_norm(_from_problems(o, nb), zg_s[slot], hn_ref[...])
    o_ref[...] = y.reshape(o_ref.shape).astype(o_ref.dtype)


def _gla_kernel(kind, n_aux, x_ref, ng_ref, w_ref, *rest):
    aux_refs, rest = rest[:n_aux], rest[n_aux:]
    (hn_ref, edge_ref, level_ref, lmask_ref, o_ref), rest = rest[:5], rest[5:]
    if kind != "hgrn2":
        aux_refs, rest = aux_refs + rest[:1], rest[1:]
    st_ref, q_s, k_s, v_s, zg_s, gs_s, tab_s, safe_s, p_s = rest
    j = pl.program_id(0)
    slot = j % 2
    prev = 1 - slot
    bufs = (q_s, k_s, v_s, zg_s, gs_s, tab_s)

    @pl.when(j == 0)
    def _():
        _zero(st_ref, *bufs)
        safe_s[0] = 1
        safe_s[1] = 1

    def front():
        return _gla_front(kind, x_ref, ng_ref, w_ref, aux_refs, edge_ref, slot, *bufs, safe_s, p_s)

    def back(centered):
        return _gla_back(centered, prev, *bufs, st_ref, hn_ref, level_ref, lmask_ref, o_ref)

    @pl.when(safe_s[prev] == 1)
    def _():
        _run_interleaved(front(), back(True))

    @pl.when(safe_s[prev] != 1)
    def _():
        _run_interleaved(back(False))
        _run_interleaved(front())


def _dn_front(x_ref, side_ref, ng_ref, w_ref, cw_ref, ab_ref, slot, hist_s, ext_s, q_s, k_s, v_s, zg_s, bg_s):
    nb, _, d = x_ref.shape
    rows = nb * CHUNK
    w = BRANCH_WIDTH
    h = _rms_norm(x_ref[...].reshape(rows, d), ng_ref[...]).astype(BF16)
    yield
    for part, dst in enumerate((q_s, k_s, v_s)):
        cols = slice(part * w, (part + 1) * w)
        ext_s[:, 0:CONV_PAD, cols] = hist_s[:, :, cols]
        ext_s[:, CONV_PAD:CONV_PAD + CHUNK, cols] = _dot(h, w_ref[:, cols]).reshape(nb, CHUNK, w)
        hist_s[:, :, cols] = ext_s[:, CHUNK:CHUNK + CONV_PAD, cols]
        yield
        ext = ext_s[:, :, cols]
        conv = cw_ref[DN_CONV - 1:DN_CONV, cols][None] * ext[:, CONV_PAD:CONV_PAD + CHUNK]
        for back in range(1, DN_CONV):
            tap = DN_CONV - 1 - back
            shifted = pltpu.roll(ext, back, axis=1)
            conv = conv + cw_ref[tap:tap + 1, cols][None] * shifted[:, CONV_PAD:CONV_PAD + CHUNK]
        act = _silu(conv.reshape(rows, w))
        if part < 2:
            scale = HEAD_DIM ** -0.5 if part == 0 else 1.0
            heads = []
            for hd in range(N_HEADS):
                xh = act[:, _head(hd)]
                heads.append(xh * lax.rsqrt(jnp.sum(xh * xh, axis=-1, keepdims=True) + NORM_EPS) * scale)
            act = jnp.concatenate(heads, axis=-1)
        dst[slot] = _to_problems(act, nb)
        yield
    zg_s[slot] = _silu(_dot(h, w_ref[:, 3 * w:4 * w]))
    yield
    ba = side_ref[...].reshape(rows, LANES)
    bg_s[slot, 0] = _sigmoid(ba)
    bg_s[slot, 1] = -jnp.exp(ab_ref[0:1, :]) * _softplus(ba + ab_ref[1:2, :])


def _dn_back(slot, q_s, k_s, v_s, zg_s, bg_s, st_ref, hn_ref, tri_ref, o_ref):
    nb = o_ref.shape[0]
    q_p = q_s[slot]
    k_p = k_s[slot]
    v_p = v_s[slot]
    beta = bg_s[slot, 0]
    g_hi, g_lo = _split_bf16(bg_s[slot, 1])
    tri = tri_ref[...]
    gcols, grows = [], []
    for b in range(nb):
        r = slice(b * CHUNK, (b + 1) * CHUNK)
        gcum = _dot(tri, jnp.concatenate([g_hi[r], g_lo[r]], axis=0))
        gcols.append(gcum)
        grows.append(gcum.T)
    ln = lambda h: slice(DN_LANE0 + N_HEADS + h, DN_LANE0 + N_HEADS + h + 1)
    gc = jnp.concatenate([gcols[b][None, :, ln(h)] for h in range(N_HEADS) for b in range(nb)], axis=0)
    gr = jnp.concatenate([grows[b][None, ln(h), :] for h in range(N_HEADS) for b in range(nb)], axis=0)
    beta3 = beta.reshape(nb, CHUNK, LANES)
    bc = jnp.concatenate([beta3[:, :, DN_LANE0 + h:DN_LANE0 + h + 1] for h in range(N_HEADS)], axis=0)

    row, col, incl, strict = _causal_masks()
    decay = jnp.exp(jnp.where(incl, gc - gr, -jnp.inf))
    g_last = gc[:, CHUNK - 1:CHUNK, :]
    e_in = jnp.exp(gc)
    e_out = jnp.exp(g_last - gc)
    e_last = jnp.exp(g_last)

    kb = k_p * bc
    kq = _bdot_nt(jnp.concatenate([kb, q_p], axis=1).astype(BF16), k_p.astype(BF16))
    yield
    m = jnp.where(strict, kq[:, 0:CHUNK] * decay, 0.0)
    sc = jnp.where(incl, kq[:, CHUNK:2 * CHUNK] * decay, 0.0).astype(BF16)
    eye = (row == col).astype(F32)[None]
    t = None
    blk = 1
    while blk < CHUNK:
        sel = ((row // (2 * blk) == col // (2 * blk)) & (row % (2 * blk) >= blk) & (col % (2 * blk) < blk))[None]
        off = jnp.where(sel, m, 0.0)
        if blk == 1:
            t = eye - off
        else:
            t_bf = t.astype(BF16)
            half = _bdot(t_bf, off.astype(BF16)).astype(BF16)
            yield
            t = t - _bdot(half, t_bf)
            yield
        blk *= 2
    rhs = jnp.concatenate([v_p * bc, kb * e_in], axis=-1).astype(BF16)
    uw = _bdot(t.astype(BF16), rhs).astype(BF16)
    yield
    sc_uw = _bdot(sc, uw)
    k_uw = _bdot_tn((k_p * e_out).astype(BF16), uw)
    yield
    q_eff = (q_p * e_in - sc_uw[:, :, HEAD_DIM:]).astype(BF16)
    st = st_ref[...]
    st_bf = st.astype(BF16)
    o = _bdot(q_eff, st_bf) + sc_uw[:, :, 0:HEAD_DIM]
    st_ref[...] = st * e_last + k_uw[:, :, 0:HEAD_DIM] - _bdot(k_uw[:, :, HEAD_DIM:].astype(BF16), st_bf)
    yield
    y = _gated_head_norm(_from_problems(o, nb), zg_s[slot], hn_ref[...])
    o_ref[...] = y.reshape(o_ref.shape).astype(o_ref.dtype)


def _dn_kernel(x_ref, side_ref, ng_ref, w_ref, cw_ref, ab_ref, hn_ref, tri_ref, o_ref,
               st_ref, hist_s, ext_s, q_s, k_s, v_s, zg_s, bg_s):
    j = pl.program_id(0)
    slot = j % 2
    bufs = (q_s, k_s, v_s, zg_s, bg_s)

    @pl.when(j == 0)
    def _():
        _zero(st_ref, hist_s, *bufs)

    _run_interleaved(
        _dn_front(x_ref, side_ref, ng_ref, w_ref, cw_ref, ab_ref, slot, hist_s, ext_s, *bufs),
        _dn_back(1 - slot, *bufs, st_ref, hn_ref, tri_ref, o_ref))


def _merge_kernel(x_ref, ng_ref, wg_ref, gb_ref, oa_ref, ob_ref, oc_ref, wb_ref, wo_ref, y_ref):
    x = x_ref[...]
    h = _rms_norm(x, ng_ref[...]).astype(BF16)
    gates = _sigmoid(_dot(h, wg_ref[...]) + gb_ref[...])
    merged = None
    for n, o_n in enumerate((oa_ref, ob_ref, oc_ref)):
        term = gates[:, n * D_MODEL:(n + 1) * D_MODEL] * _dot(o_n[...], wb_ref[n])
        merged = term if merged is None else merged + term
    y_ref[...] = x + _dot(merged.astype(BF16), wo_ref[...])


def _ffn_kernel(x_ref, ng_ref, wu_ref, cw_ref, cb_ref, wd_ref, fg_ref, y_ref, ext_s, hist_s, act_s,
                *, final_norm):
    @pl.when(pl.program_id(1) == 0)
    def _():
        hist_s[...] = jnp.zeros(hist_s.shape, hist_s.dtype)

    tile = x_ref.shape[0]
    x = x_ref[...]
    h = _rms_norm(x, ng_ref[...]).astype(BF16)
    ext_s[0:CONV_PAD, :] = hist_s[...]
    ext_s[CONV_PAD:CONV_PAD + tile, :] = _dot(h, wu_ref[...])
    hist_s[...] = ext_s[tile:tile + CONV_PAD, :]
    blk = 2 * LANES
    split = (FFN_HIDDEN // blk // 2 + 1) * blk
    y = x
    for j0 in range(0, FFN_HIDDEN, blk):
        halves = []
        for base in (j0, FFN_HIDDEN + j0):
            cols = slice(base, base + blk)
            u = cb_ref[0:1, cols] + cw_ref[0:1, cols] * ext_s[CONV_PAD - 2:CONV_PAD - 2 + tile, cols]
            for j in range(1, FFN_CONV):
                u = u + cw_ref[j:j + 1, cols] * ext_s[CONV_PAD - 2 + j:CONV_PAD - 2 + j + tile, cols]
            halves.append(u)
        act_s[:, j0:j0 + blk] = (_silu(halves[0]) * halves[1]).astype(BF16)
        if j0 + blk == split:
            y = y + _dot(act_s[:, 0:split], wd_ref[0:split, :])
    y = y + _dot(act_s[:, split:], wd_ref[split:, :])
    if final_norm:
        y = _rms_norm(y, fg_ref[...])
    y_ref[...] = y


def _const_spec(shape, grid_rank):
    nd = len(shape)
    if grid_rank == 1:
        return pl.BlockSpec(shape, lambda i, _nd=nd: (0,) * _nd, pipeline_mode=pl.Buffered(1))
    return pl.BlockSpec(shape, lambda b, i, _nd=nd: (0,) * _nd, pipeline_mode=pl.Buffered(1))


def _row_spec(tile, width, steps):
    return pl.BlockSpec((tile, width), lambda b, i, _s=steps: (b * _s + i, 0))


def _mixer_call(body, name, x3, consts, scratch, side_in=None, side_out=False):
    batch, seq, _ = x3.shape
    n = seq // CHUNK
    front = lambda j: (0, jnp.minimum(j, n - 1), 0)
    back = lambda j: (0, jnp.maximum(j - 1, 0), 0)
    side_spec = pl.BlockSpec((batch, CHUNK, LANES), front)
    streamed = [x3] + ([side_in] if side_in is not None else [])
    in_specs = ([pl.BlockSpec((batch, CHUNK, D_MODEL), front)] + [side_spec] * (len(streamed) - 1)
                + [_const_spec(c.shape, 1) for c in consts])
    out_specs = [pl.BlockSpec((batch, CHUNK, BRANCH_WIDTH), back)]
    out_shape = [jax.ShapeDtypeStruct((batch, seq, BRANCH_WIDTH), BF16)]
    if side_out:
        out_specs.append(side_spec)
        out_shape.append(jax.ShapeDtypeStruct((batch, seq, LANES), F32))
    outs = pl.pallas_call(
        body,
        grid=(n + 1,),
        in_specs=in_specs,
        out_specs=out_specs,
        out_shape=out_shape,
        scratch_shapes=scratch,
        compiler_params=pltpu.CompilerParams(dimension_semantics=("arbitrary",), vmem_limit_bytes=VMEM_LIMIT),
        name=name,
    )(*streamed, *consts)
    return outs if side_out else outs[0]


def _gla_scratch(batch, proj_width, key_width):
    rows = batch * CHUNK
    w = BRANCH_WIDTH
    return [pltpu.VMEM((N_HEADS * batch, HEAD_DIM, HEAD_DIM), F32),
            pltpu.VMEM((2, rows, key_width), BF16), pltpu.VMEM((2, rows, key_width), BF16),
            pltpu.VMEM((2, rows, w), BF16),
            pltpu.VMEM((2, rows, w), F32),
            pltpu.VMEM((2, batch, 2 * CHUNK, key_width), BF16),
            pltpu.VMEM((2, batch, 2 * CHUNK, key_width), F32),
            pltpu.SMEM((2,), jnp.int32),
            pltpu.VMEM((rows, proj_width), F32)]


def _dn_scratch(batch):
    rows = batch * CHUNK
    prob = (2, N_HEADS * batch, CHUNK, HEAD_DIM)
    return [pltpu.VMEM((N_HEADS * batch, HEAD_DIM, HEAD_DIM), F32),
            pltpu.VMEM((batch, CONV_PAD, 3 * BRANCH_WIDTH), F32),
            pltpu.VMEM((batch, CHUNK + CONV_PAD, 3 * BRANCH_WIDTH), F32),
            pltpu.VMEM(prob, F32), pltpu.VMEM(prob, F32), pltpu.VMEM(prob, F32),
            pltpu.VMEM((2, rows, BRANCH_WIDTH), F32),
            pltpu.VMEM((2, 2, rows, LANES), F32)]


def _params2():
    return pltpu.CompilerParams(dimension_semantics=("arbitrary", "arbitrary"), vmem_limit_bytes=VMEM_LIMIT)


def _merge_call(x2, batch, consts_a, branches, consts_b):
    rows = x2.shape[0]
    steps = rows // batch // TILE_MERGE
    in_specs = ([_row_spec(TILE_MERGE, D_MODEL, steps)] + [_const_spec(c.shape, 2) for c in consts_a]
                + [_row_spec(TILE_MERGE, BRANCH_WIDTH, steps) for _ in branches]
                + [_const_spec(c.shape, 2) for c in consts_b])
    return pl.pallas_call(
        _merge_kernel,
        grid=(batch, steps),
        in_specs=in_specs,
        out_specs=_row_spec(TILE_MERGE, D_MODEL, steps),
        out_shape=jax.ShapeDtypeStruct((rows, D_MODEL), F32),
        compiler_params=_params2(),
        name="merge",
    )(x2, *consts_a, *branches, *consts_b)


def _ffn_call(x2, batch, consts, final_norm):
    rows = x2.shape[0]
    steps = rows // batch // TILE_FFN
    in_specs = [_row_spec(TILE_FFN, D_MODEL, steps)] + [_const_spec(c.shape, 2) for c in consts]
    return pl.pallas_call(
        functools.partial(_ffn_kernel, final_norm=final_norm),
        grid=(batch, steps),
        in_specs=in_specs,
        out_specs=_row_spec(TILE_FFN, D_MODEL, steps),
        out_shape=jax.ShapeDtypeStruct((rows, D_MODEL), F32),
        scratch_shapes=[pltpu.VMEM((TILE_FFN + CONV_PAD, 2 * FFN_HIDDEN), F32),
                        pltpu.VMEM((CONV_PAD, 2 * FFN_HIDDEN), F32),
                        pltpu.VMEM((TILE_FFN, FFN_HIDDEN), BF16)],
        compiler_params=_params2(),
        name="ffn_final" if final_norm else "ffn",
    )(x2, *consts)


def _pad_cols(wcols, width):
    return jnp.pad(wcols, [(0, 0)] * (wcols.ndim - 1) + [(0, width - wcols.shape[-1])])


def _row(v):
    return v.reshape(1, -1).astype(F32)


def kernel(x, norm_mix_g, w_in, hg_lower_bounds, hg_norm_g, gla_gk_w2, gla_gk_b, gla_norm_g, dn_conv_w,
           dn_A_log, dn_dt_bias, dn_norm_g, w_branch, gate_b, w_out, norm_ffn_g, w_up, ffn_conv_w,
           ffn_conv_b, w_down, norm_final_g):
    batch, seq, d = x.shape
    depth = w_in.shape[0]
    assert d == D_MODEL and seq % max(CHUNK, TILE_MERGE, TILE_FFN) == 0

    lb_all = jnp.cumsum(jax.nn.softmax(hg_lower_bounds.astype(F32), axis=0), axis=0)
    lb_all = lb_all - lb_all[:1]
    edge = jnp.asarray(_edge_table(), BF16)
    level = jnp.asarray(_level_table(), BF16)
    lmask = jnp.asarray(_level_masks(), F32)
    tri1 = np.tril(np.ones((CHUNK, CHUNK), np.float32))
    tri = jnp.asarray(np.concatenate([tri1, tri1], axis=1), BF16)

    hw = BRANCH_WIDTH
    c0 = 4 * hw
    c1 = c0 + 2 * N_HEADS * GLA_DK
    c2 = c1 + hw
    c3 = c2 + GLA_RANK
    c4 = c3 + hw
    c5 = c4 + 3 * hw
    c6 = c5 + hw
    c7 = c6 + N_HEADS
    c8 = c7 + N_HEADS

    x3 = x
    for l in range(depth):
        wl = w_in[l].astype(BF16)
        ng = _row(norm_mix_g[l])
        lb = lb_all[l]
        lb_rows = jnp.stack([lb, jnp.log(lb), jnp.log1p(-lb)]).astype(F32)
        o_hg = _mixer_call(
            functools.partial(_gla_kernel, "hgrn2", 1), "hgrn2", x3,
            [ng, wl[:, :c0], lb_rows, _row(jnp.tile(hg_norm_g[l], N_HEADS)), edge, level, lmask],
            _gla_scratch(batch, c0, hw))

        gate_cols = _pad_cols(jnp.concatenate([wl[:, c2:c3], wl[:, c6:c8]], axis=1), LANES)
        w_gla = jnp.concatenate([wl[:, c0:c2], wl[:, c3:c4], gate_cols], axis=1)
        w2 = jnp.pad(gla_gk_w2[l], ((0, LANES - GLA_RANK), (0, 0))).astype(BF16)
        b2 = _row(gla_gk_b[l])
        o_gla, dn_gates = _mixer_call(
            functools.partial(_gla_kernel, "gla", 2), "gla", x3,
            [ng, w_gla, w2, b2, _row(jnp.tile(gla_norm_g[l], N_HEADS)), edge, level, lmask],
            _gla_scratch(batch, w_gla.shape[1], N_HEADS * GLA_DK), side_out=True)

        ab = jnp.zeros((2, LANES), F32)
        ab = ab.at[0, DN_LANE0 + N_HEADS:DN_LANE0 + 2 * N_HEADS].set(dn_A_log[l].astype(F32))
        ab = ab.at[1, DN_LANE0 + N_HEADS:DN_LANE0 + 2 * N_HEADS].set(dn_dt_bias[l].astype(F32))
        o_dn = _mixer_call(
            _dn_kernel, "deltanet", x3,
            [ng, wl[:, c4:c6], dn_conv_w[l].astype(F32), ab, _row(jnp.tile(dn_norm_g[l], N_HEADS)), tri],
            _dn_scratch(batch), side_in=dn_gates)

        x2 = _merge_call(
            x3.reshape(batch * seq, d), batch, [ng, wl[:, c8:], _row(gate_b[l])],
            [o.reshape(batch * seq, hw) for o in (o_hg, o_gla, o_dn)],
            [w_branch[l].astype(BF16), w_out[l].astype(BF16)])

        x2 = _ffn_call(
            x2, batch,
            [_row(norm_ffn_g[l]), w_up[l].astype(BF16), ffn_conv_w[l].astype(F32), _row(ffn_conv_b[l]),
             w_down[l].astype(BF16), _row(norm_final_g)],
            final_norm=(l == depth - 1))
        x3 = x2.reshape(batch, seq, d)
    return x3
```

```python
import functools

import jax
import jax.numpy as jnp
import numpy as np
from jax import lax
from jax.experimental import pallas as pl
from jax.experimental.pallas import tpu as pltpu

F32 = jnp.float32
BF16 = jnp.bfloat16

D_MODEL = 1024
CHUNK = 64
NORM_EPS = 1e-6
N_HEADS = 4
HEAD_DIM = 128
BRANCH_WIDTH = N_HEADS * HEAD_DIM
GLA_DK = 64
GLA_RANK = 16
GLA_GATE_NORMALIZER = 16.0
DN_LANE0 = GLA_RANK
DN_CONV = 4
N_BRANCH = 3
FFN_HIDDEN = 2816
FFN_CONV = 3
LANES = 128
CONV_PAD = 8
N_LEVELS = 6
LOG2E = 1.4426950408889634
SAFE_LOG2_DECAY = -200.0

TILE_MERGE = 512
TILE_FFN = 512
VMEM_LIMIT = 56 * 1024 * 1024


def _sigmoid(x):
    return 1.0 / (1.0 + jnp.exp(-x))


def _silu(x):
    return x * _sigmoid(x)


def _softplus(x):
    return jnp.maximum(x, 0.0) + jnp.log1p(jnp.exp(-jnp.abs(x)))


def _log_sigmoid(x):
    return -_softplus(-x)


def _rms_norm(x, g):
    y = x * lax.rsqrt(jnp.mean(x * x, axis=-1, keepdims=True) + NORM_EPS)
    return y * g


def _dot(a, b):
    return jnp.dot(a, b, preferred_element_type=F32)


def _bdot(a, b):
    return lax.dot_general(a, b, (((2,), (1,)), ((0,), (0,))), preferred_element_type=F32)


def _bdot_nt(a, b):
    return lax.dot_general(a, b, (((2,), (2,)), ((0,), (0,))), preferred_element_type=F32)


def _bdot_tn(a, b):
    return lax.dot_general(a, b, (((1,), (1,)), ((0,), (0,))), preferred_element_type=F32)


def _split_bf16(x):
    hi = x.astype(BF16)
    lo = (x - hi.astype(F32)).astype(BF16)
    return hi, lo


def _head(h):
    return slice(h * HEAD_DIM, (h + 1) * HEAD_DIM)


def _heads_to_problems(x3):
    return jnp.concatenate([x3[:, :, _head(h)] for h in range(N_HEADS)], axis=0)


def _to_problems(x2d, nb):
    return _heads_to_problems(x2d.reshape(nb, x2d.shape[0] // nb, x2d.shape[1]))


def _key_problems(x, nb, query=False):
    x3 = x if x.ndim == 3 else x.reshape(nb, x.shape[0] // nb, x.shape[1])
    if x3.shape[-1] == BRANCH_WIDTH:
        return _heads_to_problems(x3)
    lane = lax.broadcasted_iota(jnp.int32, (1, 1, LANES), 2)
    parts = []
    for h in range(N_HEADS):
        xp = x3[:, :, (h // 2) * LANES:(h // 2 + 1) * LANES]
        if query:
            own = (lane >= LANES // 2) if h % 2 else (lane < LANES // 2)
            xp = jnp.where(own, xp, jnp.zeros_like(xp))
        parts.append(xp)
    return jnp.concatenate(parts, axis=0)


def _from_problems(xp, nb):
    rows = nb * xp.shape[1]
    return jnp.concatenate([xp[h * nb:(h + 1) * nb].reshape(rows, HEAD_DIM) for h in range(N_HEADS)], axis=-1)


def _run_interleaved(*stage_generators):
    live = list(stage_generators)
    while live:
        live = [s for s in live if next(s, True) is None]


def _causal_masks():
    row = lax.broadcasted_iota(jnp.int32, (CHUNK, CHUNK), 0)
    col = lax.broadcasted_iota(jnp.int32, (CHUNK, CHUNK), 1)
    return row, col, (row >= col)[None], (row > col)[None]


def _prefix_suffix_rows():
    t = np.arange(CHUNK)[:, None]
    u = np.arange(CHUNK)[None, :]
    return [u <= t, u > t]


def _edge_table():
    tab = np.concatenate(_prefix_suffix_rows(), axis=0).astype(np.float32)
    return np.concatenate([tab, tab], axis=1)


def _level_table():
    t = np.arange(CHUNK)
    u = np.arange(CHUNK)
    rows = []
    for lvl in range(N_LEVELS):
        p = CHUNK >> (lvl + 1)
        mid = (t // (2 * p)) * (2 * p) + p - 1
        upper = t > mid
        rows.append(np.where(upper[:, None], (u[None, :] > mid[:, None]) & (u[None, :] <= t[:, None]),
                             (u[None, :] > t[:, None]) & (u[None, :] <= mid[:, None])))
    tab = np.concatenate(rows, axis=0).astype(np.float32)
    return np.concatenate([tab, tab], axis=1)


def _level_masks():
    t = np.arange(CHUNK)[:, None]
    s = np.arange(CHUNK)[None, :]
    out = []
    for lvl in range(N_LEVELS):
        p = CHUNK >> (lvl + 1)
        out.append((t // (2 * p) == s // (2 * p)) & (t % (2 * p) >= p) & (s % (2 * p) < p))
    out.append(t == s)
    return np.stack(out).astype(np.float32)


def _gated_head_norm(o, zg, g):
    outs = []
    for h in range(N_HEADS):
        oh = o[:, _head(h)]
        inv = lax.rsqrt(jnp.mean(oh * oh, axis=-1, keepdims=True) + NORM_EPS)
        outs.append(oh * inv)
    return jnp.concatenate(outs, axis=-1) * g * zg


def _zero(*refs):
    for r in refs:
        r[...] = jnp.zeros(r.shape, r.dtype)


def _gla_front(kind, x_ref, ng_ref, w_ref, aux_refs, edge_ref, slot, q_s, k_s, v_s, zg_s, gs_s, tab_s, safe_s, p_s):
    nb, _, d = x_ref.shape
    rows = nb * CHUNK
    w = BRANCH_WIDTH
    h = _rms_norm(x_ref[...].reshape(rows, d), ng_ref[...]).astype(BF16)
    yield

    def project(c0, c1):
        p_s[:, c0:c1] = _dot(h, w_ref[:, c0:c1])

    def store_decay(g):
        g_hi, g_lo = _split_bf16(g * LOG2E)
        for b in range(nb):
            r = slice(b * CHUNK, (b + 1) * CHUNK)
            gs_s[slot, b] = jnp.concatenate([g_hi[r], g_lo[r]], axis=0)

    if kind == "hgrn2":
        lb_ref, = aux_refs
        for c0 in range(0, 4 * w, w):
            project(c0, c0 + w)
            yield
        q_s[slot] = (_silu(p_s[:, 0:w]) * (HEAD_DIM ** -0.5)).astype(BF16)
        yield
        z = p_s[:, w:2 * w]
        lb = lb_ref[0:1, :]
        log_lb = lb_ref[1:2, :]
        c = lb_ref[2:3, :] + _log_sigmoid(z)
        store_decay(jnp.maximum(log_lb, c) + jnp.log1p(jnp.exp(-jnp.abs(log_lb - c))))
        k_s[slot] = ((1.0 - lb) * _sigmoid(-z)).astype(BF16)
        yield
        v_s[slot] = p_s[:, 2 * w:3 * w].astype(BF16)
        zg_s[slot] = _silu(p_s[:, 3 * w:4 * w])
        yield
    else:
        w2_ref, b2_ref, side_ref = aux_refs
        kw = q_s.shape[-1]
        low = 2 * kw + 2 * w
        project(low, low + LANES)
        side_ref[...] = p_s[:, low:low + LANES].reshape(side_ref.shape)
        yield
        project(0, 2 * kw)
        yield
        gk = _dot(p_s[:, low:low + LANES].astype(BF16), w2_ref[...]) + b2_ref[...]
        store_decay(_log_sigmoid(gk) * (1.0 / GLA_GATE_NORMALIZER))
        yield
        project(2 * kw, 2 * kw + w)
        yield
    edge = edge_ref[...]
    low_mark = None
    for b in range(nb):
        tab = _dot(edge, gs_s[slot, b])
        tab_s[slot, b] = tab
        total = tab[CHUNK - 1:CHUNK, :]
        low_mark = total if low_mark is None else jnp.minimum(low_mark, total)
        if b % 4 == 3:
            yield
    safe_s[slot] = (jnp.min(low_mark) >= SAFE_LOG2_DECAY).astype(jnp.int32)
    if kind != "hgrn2":
        project(2 * kw + w, 2 * kw + 2 * w)
        yield
        q_s[slot] = (p_s[:, 0:kw] * (GLA_DK ** -0.5)).astype(BF16)
        k_s[slot] = p_s[:, kw:2 * kw].astype(BF16)
        yield
        v_s[slot] = p_s[:, 2 * kw:2 * kw + w].astype(BF16)
        zg_s[slot] = _silu(p_s[:, 2 * kw + w:2 * kw + 2 * w])


def _gla_scores_centered(q, k, tab, nb):
    _, _, incl, _ = _causal_masks()
    pre = tab[:, 0:CHUNK, :]
    mid = 0.5 * pre[:, CHUNK - 1:CHUNK, :]
    rows = nb * CHUNK
    qf = q * jnp.exp2(pre - mid).astype(BF16).reshape(rows, -1)
    kf = k * jnp.exp2(mid - pre).astype(BF16).reshape(rows, -1)
    yield
    s = _bdot_nt(_key_problems(qf, nb, query=True), _key_problems(kf, nb))
    yield jnp.where(incl, s, 0.0)


def _gla_scores_levels(q, k, gs, level_ref, lmask_ref, nb):
    sc = _bdot_nt(_key_problems(q, nb, query=True), _key_problems(k, nb)) * lmask_ref[N_LEVELS][None]
    for l in range(N_LEVELS):
        wt = level_ref[l * CHUNK:(l + 1) * CHUNK, :]
        e = jnp.exp2(jnp.concatenate([_dot(wt, gs[b]) for b in range(nb)], axis=0)).astype(BF16)
        sc = sc + _bdot_nt(_key_problems(q * e, nb, query=True), _key_problems(k * e, nb)) * lmask_ref[l][None]
    return sc


def _gla_back(centered, slot, q_s, k_s, v_s, zg_s, gs_s, tab_s, st_ref, hn_ref, level_ref, lmask_ref, o_ref):
    nb = o_ref.shape[0]
    rows = nb * CHUNK
    q = q_s[slot]
    k = k_s[slot]
    tab = tab_s[slot]
    if centered:
        scores = _gla_scores_centered(q, k, tab, nb)
        yield next(scores)
        sc = next(scores)
    else:
        sc = _gla_scores_levels(q, k, [gs_s[slot, b] for b in range(nb)], level_ref, lmask_ref, nb)
    yield
    pre = tab[:, 0:CHUNK, :]
    e_in = jnp.exp2(pre).astype(BF16).reshape(rows, -1)
    e_out = jnp.exp2(tab[:, CHUNK:2 * CHUNK, :]).astype(BF16).reshape(rows, -1)
    e_last = _key_problems(jnp.exp2(pre[:, CHUNK - 1:CHUNK, :]), nb)
    v_p = _to_problems(v_s[slot], nb)
    st = st_ref[...]
    o = _bdot(sc.astype(BF16), v_p) + _bdot_nt(_key_problems(q * e_in, nb, query=True), st.astype(BF16))
    yield
    st_ref[...] = st * e_last + _bdot_tn(v_p, _key_problems(k * e_out, nb))
    yield
    y = _gated_head_norm(_from_problems(o, nb), zg_s[slot], hn_ref[...])
    o_ref[...] = y.reshape(o_ref.shape).astype(o_ref.dtype)


def _gla_kernel(kind, n_aux, x_ref, ng_ref, w_ref, *rest):
    aux_refs, rest = rest[:n_aux], rest[n_aux:]
    (hn_ref, edge_ref, level_ref, lmask_ref, o_ref), rest = rest[:5], rest[5:]
    if kind != "hgrn2":
        aux_refs, rest = aux_refs + rest[:1], rest[1:]
    st_ref, q_s, k_s, v_s, zg_s, gs_s, tab_s, safe_s, p_s = rest
    j = pl.program_id(0)
    slot = j % 2
    prev = 1 - slot
    bufs = (q_s, k_s, v_s, zg_s, gs_s, tab_s)

    @pl.when(j == 0)
    def _():
        _zero(st_ref, *bufs)
        safe_s[0] = 1
        safe_s[1] = 1

    def front():
        return _gla_front(kind, x_ref, ng_ref, w_ref, aux_refs, edge_ref, slot, *bufs, safe_s, p_s)

    def back(centered):
        return _gla_back(centered, prev, *bufs, st_ref, hn_ref, level_ref, lmask_ref, o_ref)

    @pl.when(safe_s[prev] == 1)
    def _():
        if kind == "hgrn2":
            _run_interleaved(front(), back(True))
        else:
            _run_interleaved(front())
            _run_interleaved(back(True))

    @pl.when(safe_s[prev] != 1)
    def _():
        _run_interleaved(back(False))
        _run_interleaved(front())


def _dn_front(x_ref, side_ref, ng_ref, w_ref, cw_ref, ab_ref, slot, hist_s, ext_s, q_s, k_s, v_s, zg_s, bg_s):
    nb, _, d = x_ref.shape
    rows = nb * CHUNK
    w = BRANCH_WIDTH
    h = _rms_norm(x_ref[...].reshape(rows, d), ng_ref[...]).astype(BF16)
    yield
    for part, dst in enumerate((q_s, k_s, v_s)):
        cols = slice(part * w, (part + 1) * w)
        ext_s[:, 0:CONV_PAD, cols] = hist_s[:, :, cols]
        ext_s[:, CONV_PAD:CONV_PAD + CHUNK, cols] = _dot(h, w_ref[:, cols]).reshape(nb, CHUNK, w)
        hist_s[:, :, cols] = ext_s[:, CHUNK:CHUNK + CONV_PAD, cols]
        yield
        ext = ext_s[:, :, cols]
        conv = cw_ref[DN_CONV - 1:DN_CONV, cols][None] * ext[:, CONV_PAD:CONV_PAD + CHUNK]
        for back in range(1, DN_CONV):
            tap = DN_CONV - 1 - back
            shifted = pltpu.roll(ext, back, axis=1)
            conv = conv + cw_ref[tap:tap + 1, cols][None] * shifted[:, CONV_PAD:CONV_PAD + CHUNK]
        act = _silu(conv.reshape(rows, w))
        if part < 2:
            scale = HEAD_DIM ** -0.5 if part == 0 else 1.0
            heads = []
            for hd in range(N_HEADS):
                xh = act[:, _head(hd)]
                heads.append(xh * lax.rsqrt(jnp.sum(xh * xh, axis=-1, keepdims=True) + NORM_EPS) * scale)
            act = jnp.concatenate(heads, axis=-1)
        dst[slot] = _to_problems(act, nb)
        yield
    zg_s[slot] = _silu(_dot(h, w_ref[:, 3 * w:4 * w]))
    yield
    ba = side_ref[...].reshape(rows, LANES)
    bg_s[slot, 0] = _sigmoid(ba)
    bg_s[slot, 1] = -jnp.exp(ab_ref[0:1, :]) * _softplus(ba + ab_ref[1:2, :])


def _dn_back(slot, q_s, k_s, v_s, zg_s, bg_s, st_ref, hn_ref, tri_ref, o_ref):
    nb = o_ref.shape[0]
    q_p = q_s[slot]
    k_p = k_s[slot]
    v_p = v_s[slot]
    beta = bg_s[slot, 0]
    g_hi, g_lo = _split_bf16(bg_s[slot, 1])
    tri = tri_ref[...]
    gcols, grows = [], []
    for b in range(nb):
        r = slice(b * CHUNK, (b + 1) * CHUNK)
        gcum = _dot(tri, jnp.concatenate([g_hi[r], g_lo[r]], axis=0))
        gcols.append(gcum)
        grows.append(gcum.T)
    ln = lambda h: slice(DN_LANE0 + N_HEADS + h, DN_LANE0 + N_HEADS + h + 1)
    gc = jnp.concatenate([gcols[b][None, :, ln(h)] for h in range(N_HEADS) for b in range(nb)], axis=0)
    gr = jnp.concatenate([grows[b][None, ln(h), :] for h in range(N_HEADS) for b in range(nb)], axis=0)
    beta3 = beta.reshape(nb, CHUNK, LANES)
    bc = jnp.concatenate([beta3[:, :, DN_LANE0 + h:DN_LANE0 + h + 1] for h in range(N_HEADS)], axis=0)

    row, col, incl, strict = _causal_masks()
    decay = jnp.exp(jnp.where(incl, gc - gr, -jnp.inf))
    g_last = gc[:, CHUNK - 1:CHUNK, :]
    e_in = jnp.exp(gc)
    e_out = jnp.exp(g_last - gc)
    e_last = jnp.exp(g_last)

    kb = k_p * bc
    kq = _bdot_nt(jnp.concatenate([kb, q_p], axis=1).astype(BF16), k_p.astype(BF16))
    yield
    m = jnp.where(strict, kq[:, 0:CHUNK] * decay, 0.0)
    sc = jnp.where(incl, kq[:, CHUNK:2 * CHUNK] * decay, 0.0).astype(BF16)
    eye = (row == col).astype(F32)[None]
    t = None
    blk = 1
    while blk < CHUNK:
        sel = ((row // (2 * blk) == col // (2 * blk)) & (row % (2 * blk) >= blk) & (col % (2 * blk) < blk))[None]
        off = jnp.where(sel, m, 0.0)
        if blk == 1:
            t = eye - off
        else:
            t_bf = t.astype(BF16)
            half = _bdot(t_bf, off.astype(BF16)).astype(BF16)
            yield
            t = t - _bdot(half, t_bf)
            yield
        blk *= 2
    rhs = jnp.concatenate([v_p * bc, kb * e_in], axis=-1).astype(BF16)
    uw = _bdot(t.astype(BF16), rhs).astype(BF16)
    yield
    sc_uw = _bdot(sc, uw)
    k_uw = _bdot_tn((k_p * e_out).astype(BF16), uw)
    yield
    q_eff = (q_p * e_in - sc_uw[:, :, HEAD_DIM:]).astype(BF16)
    st = st_ref[...]
    st_bf = st.astype(BF16)
    o = _bdot(q_eff, st_bf) + sc_uw[:, :, 0:HEAD_DIM]
    st_ref[...] = st * e_last + k_uw[:, :, 0:HEAD_DIM] - _bdot(k_uw[:, :, HEAD_DIM:].astype(BF16), st_bf)
    yield
    y = _gated_head_norm(_from_problems(o, nb), zg_s[slot], hn_ref[...])
    o_ref[...] = y.reshape(o_ref.shape).astype(o_ref.dtype)


def _dn_kernel(x_ref, side_ref, ng_ref, w_ref, cw_ref, ab_ref, hn_ref, tri_ref, o_ref,
               st_ref, hist_s, ext_s, q_s, k_s, v_s, zg_s, bg_s):
    j = pl.program_id(0)
    slot = j % 2
    bufs = (q_s, k_s, v_s, zg_s, bg_s)

    @pl.when(j == 0)
    def _():
        _zero(st_ref, hist_s, *bufs)

    _run_interleaved(_dn_front(x_ref, side_ref, ng_ref, w_ref, cw_ref, ab_ref, slot, hist_s, ext_s, *bufs))
    _run_interleaved(_dn_back(1 - slot, *bufs, st_ref, hn_ref, tri_ref, o_ref))


def _merge_kernel(x_ref, ng_ref, wg_ref, gb_ref, oa_ref, ob_ref, oc_ref, wb_ref, wo_ref, y_ref):
    x = x_ref[...]
    h = _rms_norm(x, ng_ref[...]).astype(BF16)
    gates = _sigmoid(_dot(h, wg_ref[...]) + gb_ref[...])
    merged = None
    for n, o_n in enumerate((oa_ref, ob_ref, oc_ref)):
        term = gates[:, n * D_MODEL:(n + 1) * D_MODEL] * _dot(o_n[...], wb_ref[n])
        merged = term if merged is None else merged + term
    y_ref[...] = x + _dot(merged.astype(BF16), wo_ref[...])


def _ffn_kernel(x_ref, ng_ref, wu_ref, cw_ref, cb_ref, wd_ref, fg_ref, y_ref, ext_s, hist_s, act_s,
                *, final_norm):
    @pl.when(pl.program_id(1) == 0)
    def _():
        hist_s[...] = jnp.zeros(hist_s.shape, hist_s.dtype)

    tile = x_ref.shape[0]
    x = x_ref[...]
    h = _rms_norm(x, ng_ref[...]).astype(BF16)
    ext_s[0:CONV_PAD, :] = hist_s[...]
    ext_s[CONV_PAD:CONV_PAD + tile, :] = _dot(h, wu_ref[...])
    hist_s[...] = ext_s[tile:tile + CONV_PAD, :]
    blk = 2 * LANES
    split = (FFN_HIDDEN // blk // 2 + 1) * blk
    y = x
    for j0 in range(0, FFN_HIDDEN, blk):
        halves = []
        for base in (j0, FFN_HIDDEN + j0):
            cols = slice(base, base + blk)
            u = cb_ref[0:1, cols] + cw_ref[0:1, cols] * ext_s[CONV_PAD - 2:CONV_PAD - 2 + tile, cols]
            for j in range(1, FFN_CONV):
                u = u + cw_ref[j:j + 1, cols] * ext_s[CONV_PAD - 2 + j:CONV_PAD - 2 + j + tile, cols]
            halves.append(u)
        act_s[:, j0:j0 + blk] = (_silu(halves[0]) * halves[1]).astype(BF16)
        if j0 + blk == split:
            y = y + _dot(act_s[:, 0:split], wd_ref[0:split, :])
    y = y + _dot(act_s[:, split:], wd_ref[split:, :])
    if final_norm:
        y = _rms_norm(y, fg_ref[...])
    y_ref[...] = y


def _const_spec(shape, grid_rank):
    nd = len(shape)
    if grid_rank == 1:
        return pl.BlockSpec(shape, lambda i, _nd=nd: (0,) * _nd, pipeline_mode=pl.Buffered(1))
    return pl.BlockSpec(shape, lambda b, i, _nd=nd: (0,) * _nd, pipeline_mode=pl.Buffered(1))


def _row_spec(tile, width, steps):
    return pl.BlockSpec((tile, width), lambda b, i, _s=steps: (b * _s + i, 0))


def _mixer_call(body, name, x3, consts, scratch, side_in=None, side_out=False):
    batch, seq, _ = x3.shape
    n = seq // CHUNK
    front = lambda j: (0, jnp.minimum(j, n - 1), 0)
    back = lambda j: (0, jnp.maximum(j - 1, 0), 0)
    side_spec = pl.BlockSpec((batch, CHUNK, LANES), front)
    streamed = [x3] + ([side_in] if side_in is not None else [])
    in_specs = ([pl.BlockSpec((batch, CHUNK, D_MODEL), front)] + [side_spec] * (len(streamed) - 1)
                + [_const_spec(c.shape, 1) for c in consts])
    out_specs = [pl.BlockSpec((batch, CHUNK, BRANCH_WIDTH), back)]
    out_shape = [jax.ShapeDtypeStruct((batch, seq, BRANCH_WIDTH), BF16)]
    if side_out:
        out_specs.append(side_spec)
        out_shape.append(jax.ShapeDtypeStruct((batch, seq, LANES), F32))
    outs = pl.pallas_call(
        body,
        grid=(n + 1,),
        in_specs=in_specs,
        out_specs=out_specs,
        out_shape=out_shape,
        scratch_shapes=scratch,
        compiler_params=pltpu.CompilerParams(dimension_semantics=("arbitrary",), vmem_limit_bytes=VMEM_LIMIT),
        name=name,
    )(*streamed, *consts)
    return outs if side_out else outs[0]


def _gla_scratch(batch, proj_width, key_width):
    rows = batch * CHUNK
    w = BRANCH_WIDTH
    return [pltpu.VMEM((N_HEADS * batch, HEAD_DIM, HEAD_DIM), F32),
            pltpu.VMEM((2, rows, key_width), BF16), pltpu.VMEM((2, rows, key_width), BF16),
            pltpu.VMEM((2, rows, w), BF16),
            pltpu.VMEM((2, rows, w), F32),
            pltpu.VMEM((2, batch, 2 * CHUNK, key_width), BF16),
            pltpu.VMEM((2, batch, 2 * CHUNK, key_width), F32),
            pltpu.SMEM((2,), jnp.int32),
            pltpu.VMEM((rows, proj_width), F32)]


def _dn_scratch(batch):
    rows = batch * CHUNK
    prob = (2, N_HEADS * batch, CHUNK, HEAD_DIM)
    return [pltpu.VMEM((N_HEADS * batch, HEAD_DIM, HEAD_DIM), F32),
            pltpu.VMEM((batch, CONV_PAD, 3 * BRANCH_WIDTH), F32),
            pltpu.VMEM((batch, CHUNK + CONV_PAD, 3 * BRANCH_WIDTH), F32),
            pltpu.VMEM(prob, F32), pltpu.VMEM(prob, F32), pltpu.VMEM(prob, F32),
            pltpu.VMEM((2, rows, BRANCH_WIDTH), F32),
            pltpu.VMEM((2, 2, rows, LANES), F32)]


def _params2():
    return pltpu.CompilerParams(dimension_semantics=("arbitrary", "arbitrary"), vmem_limit_bytes=VMEM_LIMIT)


def _merge_call(x2, batch, consts_a, branches, consts_b):
    rows = x2.shape[0]
    steps = rows // batch // TILE_MERGE
    in_specs = ([_row_spec(TILE_MERGE, D_MODEL, steps)] + [_const_spec(c.shape, 2) for c in consts_a]
                + [_row_spec(TILE_MERGE, BRANCH_WIDTH, steps) for _ in branches]
                + [_const_spec(c.shape, 2) for c in consts_b])
    return pl.pallas_call(
        _merge_kernel,
        grid=(batch, steps),
        in_specs=in_specs,
        out_specs=_row_spec(TILE_MERGE, D_MODEL, steps),
        out_shape=jax.ShapeDtypeStruct((rows, D_MODEL), F32),
        compiler_params=_params2(),
        name="merge",
    )(x2, *consts_a, *branches, *consts_b)


def _ffn_call(x2, batch, consts, final_norm):
    rows = x2.shape[0]
    steps = rows // batch // TILE_FFN
    in_specs = [_row_spec(TILE_FFN, D_MODEL, steps)] + [_const_spec(c.shape, 2) for c in consts]
    return pl.pallas_call(
        functools.partial(_ffn_kernel, final_norm=final_norm),
        grid=(batch, steps),
        in_specs=in_specs,
        out_specs=_row_spec(TILE_FFN, D_MODEL, steps),
        out_shape=jax.ShapeDtypeStruct((rows, D_MODEL), F32),
        scratch_shapes=[pltpu.VMEM((TILE_FFN + CONV_PAD, 2 * FFN_HIDDEN), F32),
                        pltpu.VMEM((CONV_PAD, 2 * FFN_HIDDEN), F32),
                        pltpu.VMEM((TILE_FFN, FFN_HIDDEN), BF16)],
        compiler_params=_params2(),
        name="ffn_final" if final_norm else "ffn",
    )(x2, *consts)


def _pad_cols(wcols, width):
    return jnp.pad(wcols, [(0, 0)] * (wcols.ndim - 1) + [(0, width - wcols.shape[-1])])


def _row(v):
    return v.reshape(1, -1).astype(F32)


def kernel(x, norm_mix_g, w_in, hg_lower_bounds, hg_norm_g, gla_gk_w2, gla_gk_b, gla_norm_g, dn_conv_w,
           dn_A_log, dn_dt_bias, dn_norm_g, w_branch, gate_b, w_out, norm_ffn_g, w_up, ffn_conv_w,
           ffn_conv_b, w_down, norm_final_g):
    batch, seq, d = x.shape
    depth = w_in.shape[0]
    assert d == D_MODEL and seq % max(CHUNK, TILE_MERGE, TILE_FFN) == 0

    lb_all = jnp.cumsum(jax.nn.softmax(hg_lower_bounds.astype(F32), axis=0), axis=0)
    lb_all = lb_all - lb_all[:1]
    edge = jnp.asarray(_edge_table(), BF16)
    level = jnp.asarray(_level_table(), BF16)
    lmask = jnp.asarray(_level_masks(), F32)
    tri1 = np.tril(np.ones((CHUNK, CHUNK), np.float32))
    tri = jnp.asarray(np.concatenate([tri1, tri1], axis=1), BF16)

    hw = BRANCH_WIDTH
    c0 = 4 * hw
    c1 = c0 + 2 * N_HEADS * GLA_DK
    c2 = c1 + hw
    c3 = c2 + GLA_RANK
    c4 = c3 + hw
    c5 = c4 + 3 * hw
    c6 = c5 + hw
    c7 = c6 + N_HEADS
    c8 = c7 + N_HEADS

    x3 = x
    for l in range(depth):
        wl = w_in[l]
        ng = _row(norm_mix_g[l])
        lb = lb_all[l]
        lb_rows = jnp.stack([lb, jnp.log(lb), jnp.log1p(-lb)]).astype(F32)
        o_hg = _mixer_call(
            functools.partial(_gla_kernel, "hgrn2", 1), "hgrn2", x3,
            [ng, wl[:, :c0].astype(BF16), lb_rows, _row(jnp.tile(hg_norm_g[l], N_HEADS)), edge, level, lmask],
            _gla_scratch(batch, c0, hw))

        gate_cols = _pad_cols(jnp.concatenate([wl[:, c2:c3], wl[:, c6:c8]], axis=1), LANES)
        w_gla = jnp.concatenate([wl[:, c0:c2], wl[:, c3:c4], gate_cols], axis=1).astype(BF16)
        w2 = jnp.pad(gla_gk_w2[l], ((0, LANES - GLA_RANK), (0, 0))).astype(BF16)
        b2 = _row(gla_gk_b[l])
        o_gla, dn_gates = _mixer_call(
            functools.partial(_gla_kernel, "gla", 2), "gla", x3,
            [ng, w_gla, w2, b2, _row(jnp.tile(gla_norm_g[l], N_HEADS)), edge, level, lmask],
            _gla_scratch(batch, w_gla.shape[1], N_HEADS * GLA_DK), side_out=True)

        ab = jnp.zeros((2, LANES), F32)
        ab = ab.at[0, DN_LANE0 + N_HEADS:DN_LANE0 + 2 * N_HEADS].set(dn_A_log[l].astype(F32))
        ab = ab.at[1, DN_LANE0 + N_HEADS:DN_LANE0 + 2 * N_HEADS].set(dn_dt_bias[l].astype(F32))
        o_dn = _mixer_call(
            _dn_kernel, "deltanet", x3,
            [ng, wl[:, c4:c6].astype(BF16), dn_conv_w[l].astype(F32), ab, _row(jnp.tile(dn_norm_g[l], N_HEADS)), tri],
            _dn_scratch(batch), side_in=dn_gates)

        x2 = _merge_call(
            x3.reshape(batch * seq, d), batch, [ng, wl[:, c8:].astype(BF16), _row(gate_b[l])],
            [o.reshape(batch * seq, hw) for o in (o_hg, o_gla, o_dn)],
            [w_branch[l].astype(BF16), w_out[l].astype(BF16)])

        x2 = _ffn_call(
            x2, batch,
            [_row(norm_ffn_g[l]), w_up[l].astype(BF16), ffn_conv_w[l].astype(F32), _row(ffn_conv_b[l]),
             w_down[l].astype(BF16), _row(norm_final_g)],
            final_norm=(l == depth - 1))
        x3 = x2.reshape(batch, seq, d)
    return x3
```

```python
import functools

import jax
import jax.numpy as jnp
import numpy as np
from jax import lax
from jax.experimental import pallas as pl
from jax.experimental.pallas import tpu as pltpu

F32 = jnp.float32
BF16 = jnp.bfloat16

D_MODEL = 1024
CHUNK = 64
NORM_EPS = 1e-6
N_HEADS = 4
HEAD_DIM = 128
BRANCH_WIDTH = N_HEADS * HEAD_DIM
GLA_DK = 64
GLA_RANK = 16
GLA_GATE_NORMALIZER = 16.0
DN_LANE0 = GLA_RANK
DN_CONV = 4
N_BRANCH = 3
FFN_HIDDEN = 2816
FFN_CONV = 3
LANES = 128
CONV_PAD = 8
N_LEVELS = 6
LOG2E = 1.4426950408889634
SAFE_LOG2_DECAY = -160.0

TILE_MERGE = 512
TILE_FFN = 512
VMEM_LIMIT = 56 * 1024 * 1024


def _sigmoid(x):
    return 1.0 / (1.0 + jnp.exp(-x))


def _silu(x):
    return x * _sigmoid(x)


def _softplus(x):
    return jnp.maximum(x, 0.0) + jnp.log1p(jnp.exp(-jnp.abs(x)))


def _log_sigmoid(x):
    return -_softplus(-x)


def _rms_norm(x, g):
    y = x * lax.rsqrt(jnp.mean(x * x, axis=-1, keepdims=True) + NORM_EPS)
    return y * g


def _dot(a, b):
    return jnp.dot(a, b, preferred_element_type=F32)


def _bdot(a, b):
    return lax.dot_general(a, b, (((2,), (1,)), ((0,), (0,))), preferred_element_type=F32)


def _bdot_nt(a, b):
    return lax.dot_general(a, b, (((2,), (2,)), ((0,), (0,))), preferred_element_type=F32)


def _bdot_tn(a, b):
    return lax.dot_general(a, b, (((1,), (1,)), ((0,), (0,))), preferred_element_type=F32)


def _split_bf16(x):
    hi = x.astype(BF16)
    lo = (x - hi.astype(F32)).astype(BF16)
    return hi, lo


def _head(h):
    return slice(h * HEAD_DIM, (h + 1) * HEAD_DIM)


def _heads_to_problems(x3):
    return jnp.concatenate([x3[:, :, _head(h)] for h in range(N_HEADS)], axis=0)


def _to_problems(x2d, nb):
    return _heads_to_problems(x2d.reshape(nb, x2d.shape[0] // nb, x2d.shape[1]))


def _key_problems(x, nb, query=False):
    x3 = x if x.ndim == 3 else x.reshape(nb, x.shape[0] // nb, x.shape[1])
    if x3.shape[-1] == BRANCH_WIDTH:
        return _heads_to_problems(x3)
    lane = lax.broadcasted_iota(jnp.int32, (1, 1, LANES), 2)
    parts = []
    for h in range(N_HEADS):
        xp = x3[:, :, (h // 2) * LANES:(h // 2 + 1) * LANES]
        if query:
            own = (lane >= LANES // 2) if h % 2 else (lane < LANES // 2)
            xp = jnp.where(own, xp, jnp.zeros_like(xp))
        parts.append(xp)
    return jnp.concatenate(parts, axis=0)


def _from_problems(xp, nb):
    rows = nb * xp.shape[1]
    return jnp.concatenate([xp[h * nb:(h + 1) * nb].reshape(rows, HEAD_DIM) for h in range(N_HEADS)], axis=-1)


def _run_interleaved(*stage_generators):
    live = list(stage_generators)
    while live:
        live = [s for s in live if next(s, True) is None]


def _causal_masks():
    row = lax.broadcasted_iota(jnp.int32, (CHUNK, CHUNK), 0)
    col = lax.broadcasted_iota(jnp.int32, (CHUNK, CHUNK), 1)
    return row, col, (row >= col)[None], (row > col)[None]


def _prefix_suffix_rows():
    t = np.arange(CHUNK)[:, None]
    u = np.arange(CHUNK)[None, :]
    return [u <= t, u > t]


def _edge_table():
    tab = np.concatenate(_prefix_suffix_rows(), axis=0).astype(np.float32)
    return np.concatenate([tab, tab], axis=1)


def _level_table():
    t = np.arange(CHUNK)
    u = np.arange(CHUNK)
    rows = []
    for lvl in range(N_LEVELS):
        p = CHUNK >> (lvl + 1)
        mid = (t // (2 * p)) * (2 * p) + p - 1
        upper = t > mid
        rows.append(np.where(upper[:, None], (u[None, :] > mid[:, None]) & (u[None, :] <= t[:, None]),
                             (u[None, :] > t[:, None]) & (u[None, :] <= mid[:, None])))
    tab = np.concatenate(rows, axis=0).astype(np.float32)
    return np.concatenate([tab, tab], axis=1)


def _level_masks():
    t = np.arange(CHUNK)[:, None]
    s = np.arange(CHUNK)[None, :]
    out = []
    for lvl in range(N_LEVELS):
        p = CHUNK >> (lvl + 1)
        out.append((t // (2 * p) == s // (2 * p)) & (t % (2 * p) >= p) & (s % (2 * p) < p))
    out.append(t == s)
    return np.stack(out).astype(np.float32)


def _gated_head_norm(o, zg, g):
    outs = []
    for h in range(N_HEADS):
        oh = o[:, _head(h)]
        inv = lax.rsqrt(jnp.mean(oh * oh, axis=-1, keepdims=True) + NORM_EPS)
        outs.append(oh * inv)
    return jnp.concatenate(outs, axis=-1) * g * zg


def _zero(*refs):
    for r in refs:
        r[...] = jnp.zeros(r.shape, r.dtype)


def _gla_front(kind, x_ref, ng_ref, w_ref, aux_refs, edge_ref, slot, q_s, k_s, v_s, zg_s, gs_s, tab_s, safe_s, p_s):
    nb, _, d = x_ref.shape
    rows = nb * CHUNK
    w = BRANCH_WIDTH
    h = _rms_norm(x_ref[...].reshape(rows, d), ng_ref[...]).astype(BF16)
    yield

    def project(c0, c1):
        p_s[:, c0:c1] = _dot(h, w_ref[:, c0:c1])

    def store_decay(g):
        g_hi, g_lo = _split_bf16(g * LOG2E)
        for b in range(nb):
            r = slice(b * CHUNK, (b + 1) * CHUNK)
            gs_s[slot, b] = jnp.concatenate([g_hi[r], g_lo[r]], axis=0)

    if kind == "hgrn2":
        lb_ref, = aux_refs
        for c0 in range(0, 4 * w, w):
            project(c0, c0 + w)
            yield
        q_s[slot] = (_silu(p_s[:, 0:w]) * (HEAD_DIM ** -0.5)).astype(BF16)
        yield
        z = p_s[:, w:2 * w]
        lb = lb_ref[0:1, :]
        log_lb = lb_ref[1:2, :]
        c = lb_ref[2:3, :] + _log_sigmoid(z)
        store_decay(jnp.maximum(log_lb, c) + jnp.log1p(jnp.exp(-jnp.abs(log_lb - c))))
        k_s[slot] = ((1.0 - lb) * _sigmoid(-z)).astype(BF16)
        yield
        v_s[slot] = p_s[:, 2 * w:3 * w].astype(BF16)
        zg_s[slot] = _silu(p_s[:, 3 * w:4 * w])
        yield
    else:
        w2_ref, b2_ref, side_ref = aux_refs
        kw = q_s.shape[-1]
        low = 2 * kw + 2 * w
        project(low, low + LANES)
        side_ref[...] = p_s[:, low:low + LANES].reshape(side_ref.shape)
        yield
        project(0, 2 * kw)
        yield
        gk = _dot(p_s[:, low:low + LANES].astype(BF16), w2_ref[...]) + b2_ref[...]
        store_decay(_log_sigmoid(gk) * (1.0 / GLA_GATE_NORMALIZER))
        yield
        project(2 * kw, 2 * kw + w)
        yield
    edge = edge_ref[...]
    low_mark = None
    for b in range(nb):
        tab = _dot(edge, gs_s[slot, b])
        tab_s[slot, b] = tab
        total = tab[CHUNK - 1:CHUNK, :]
        low_mark = total if low_mark is None else jnp.minimum(low_mark, total)
        if b % 4 == 3:
            yield
    safe_s[slot] = (jnp.min(low_mark) >= SAFE_LOG2_DECAY).astype(jnp.int32)
    if kind != "hgrn2":
        project(2 * kw + w, 2 * kw + 2 * w)
        yield
        q_s[slot] = (p_s[:, 0:kw] * (GLA_DK ** -0.5)).astype(BF16)
        k_s[slot] = p_s[:, kw:2 * kw].astype(BF16)
        yield
        v_s[slot] = p_s[:, 2 * kw:2 * kw + w].astype(BF16)
        zg_s[slot] = _silu(p_s[:, 2 * kw + w:2 * kw + 2 * w])


def _gla_scores_centered(q, k, tab, nb):
    _, _, incl, _ = _causal_masks()
    pre = tab[:, 0:CHUNK, :]
    mid = 0.5 * pre[:, CHUNK - 1:CHUNK, :]
    rows = nb * CHUNK
    qf = q * jnp.exp2(pre - mid).astype(BF16).reshape(rows, -1)
    kf = k * jnp.exp2(mid - pre).astype(BF16).reshape(rows, -1)
    yield
    s = _bdot_nt(_key_problems(qf, nb, query=True), _key_problems(kf, nb))
    yield jnp.where(incl, s, 0.0)


def _gla_scores_levels(q, k, gs, level_ref, lmask_ref, nb):
    sc = _bdot_nt(_key_problems(q, nb, query=True), _key_problems(k, nb)) * lmask_ref[N_LEVELS][None]
    for l in range(N_LEVELS):
        wt = level_ref[l * CHUNK:(l + 1) * CHUNK, :]
        e = jnp.exp2(jnp.concatenate([_dot(wt, gs[b]) for b in range(nb)], axis=0)).astype(BF16)
        sc = sc + _bdot_nt(_key_problems(q * e, nb, query=True), _key_problems(k * e, nb)) * lmask_ref[l][None]
    return sc


def _gla_back(centered, slot, q_s, k_s, v_s, zg_s, gs_s, tab_s, st_ref, hn_ref, level_ref, lmask_ref, o_ref):
    nb = o_ref.shape[0]
    rows = nb * CHUNK
    q = q_s[slot]
    k = k_s[slot]
    tab = tab_s[slot]
    if centered:
        scores = _gla_scores_centered(q, k, tab, nb)
        yield next(scores)
        sc = next(scores)
    else:
        sc = _gla_scores_levels(q, k, [gs_s[slot, b] for b in range(nb)], level_ref, lmask_ref, nb)
    yield
    pre = tab[:, 0:CHUNK, :]
    e_in = jnp.exp2(pre).astype(BF16).reshape(rows, -1)
    e_out = jnp.exp2(tab[:, CHUNK:2 * CHUNK, :]).astype(BF16).reshape(rows, -1)
    e_last = _key_problems(jnp.exp2(pre[:, CHUNK - 1:CHUNK, :]), nb)
    v_p = _to_problems(v_s[slot], nb)
    st = st_ref[...]
    o = _bdot(sc.astype(BF16), v_p) + _bdot_nt(_key_problems(q * e_in, nb, query=True), st.astype(BF16))
    yield
    st_ref[...] = st * e_last + _bdot_tn(v_p, _key_problems(k * e_out, nb))
    yield
    y = _gated_head_norm(_from_problems(o, nb), zg_s[slot], hn_ref[...])
    o_ref[...] = y.reshape(o_ref.shape).astype(o_ref.dtype)


def _gla_kernel(kind, n_aux, x_ref, ng_ref, w_ref, *rest):
    aux_refs, rest = rest[:n_aux], rest[n_aux:]
    (hn_ref, edge_ref, level_ref, lmask_ref, o_ref), rest = rest[:5], rest[5:]
    if kind != "hgrn2":
        aux_refs, rest = aux_refs + rest[:1], rest[1:]
    st_ref, q_s, k_s, v_s, zg_s, gs_s, tab_s, safe_s, p_s = rest
    j = pl.program_id(0)
    slot = j % 2
    prev = 1 - slot
    bufs = (q_s, k_s, v_s, zg_s, gs_s, tab_s)

    @pl.when(j == 0)
    def _():
        _zero(st_ref, *bufs)
        safe_s[0] = 1
        safe_s[1] = 1

    def front():
        return _gla_front(kind, x_ref, ng_ref, w_ref, aux_refs, edge_ref, slot, *bufs, safe_s, p_s)

    def back(centered):
        return _gla_back(centered, prev, *bufs, st_ref, hn_ref, level_ref, lmask_ref, o_ref)

    @pl.when(safe_s[prev] == 1)
    def _():
        _run_interleaved(front(), back(True))

    @pl.when(safe_s[prev] != 1)
    def _():
        _run_interleaved(back(False))
        _run_interleaved(front())


def _dn_front(x_ref, side_ref, ng_ref, w_ref, cw_ref, ab_ref, slot, hist_s, ext_s, q_s, k_s, v_s, zg_s, bg_s):
    nb, _, d = x_ref.shape
    rows = nb * CHUNK
    w = BRANCH_WIDTH
    h = _rms_norm(x_ref[...].reshape(rows, d), ng_ref[...]).astype(BF16)
    yield
    for part, dst in enumerate((q_s, k_s, v_s)):
        cols = slice(part * w, (part + 1) * w)
        ext_s[:, 0:CONV_PAD, cols] = hist_s[:, :, cols]
        ext_s[:, CONV_PAD:CONV_PAD + CHUNK, cols] = _dot(h, w_ref[:, cols]).reshape(nb, CHUNK, w)
        hist_s[:, :, cols] = ext_s[:, CHUNK:CHUNK + CONV_PAD, cols]
        yield
        ext = ext_s[:, :, cols]
        conv = cw_ref[DN_CONV - 1:DN_CONV, cols][None] * ext[:, CONV_PAD:CONV_PAD + CHUNK]
        for back in range(1, DN_CONV):
            tap = DN_CONV - 1 - back
            shifted = pltpu.roll(ext, back, axis=1)
            conv = conv + cw_ref[tap:tap + 1, cols][None] * shifted[:, CONV_PAD:CONV_PAD + CHUNK]
        act = _silu(conv.reshape(rows, w))
        if part < 2:
            scale = HEAD_DIM ** -0.5 if part == 0 else 1.0
            heads = []
            for hd in range(N_HEADS):
                xh = act[:, _head(hd)]
                heads.append(xh * lax.rsqrt(jnp.sum(xh * xh, axis=-1, keepdims=True) + NORM_EPS) * scale)
            act = jnp.concatenate(heads, axis=-1)
        dst[slot] = _to_problems(act, nb)
        yield
    zg_s[slot] = _silu(_dot(h, w_ref[:, 3 * w:4 * w]))
    yield
    ba = side_ref[...].reshape(rows, LANES)
    bg_s[slot, 0] = _sigmoid(ba)
    bg_s[slot, 1] = -jnp.exp(ab_ref[0:1, :]) * _softplus(ba + ab_ref[1:2, :])


def _dn_back(slot, q_s, k_s, v_s, zg_s, bg_s, st_ref, hn_ref, tri_ref, o_ref):
    nb = o_ref.shape[0]
    q_p = q_s[slot]
    k_p = k_s[slot]
    v_p = v_s[slot]
    beta = bg_s[slot, 0]
    g_hi, g_lo = _split_bf16(bg_s[slot, 1])
    tri = tri_ref[...]
    gcols, grows = [], []
    for b in range(nb):
        r = slice(b * CHUNK, (b + 1) * CHUNK)
        gcum = _dot(tri, jnp.concatenate([g_hi[r], g_lo[r]], axis=0))
        gcols.append(gcum)
        grows.append(gcum.T)
    ln = lambda h: slice(DN_LANE0 + N_HEADS + h, DN_LANE0 + N_HEADS + h + 1)
    gc = jnp.concatenate([gcols[b][None, :, ln(h)] for h in range(N_HEADS) for b in range(nb)], axis=0)
    gr = jnp.concatenate([grows[b][None, ln(h), :] for h in range(N_HEADS) for b in range(nb)], axis=0)
    beta3 = beta.reshape(nb, CHUNK, LANES)
    bc = jnp.concatenate([beta3[:, :, DN_LANE0 + h:DN_LANE0 + h + 1] for h in range(N_HEADS)], axis=0)

    row, col, incl, strict = _causal_masks()
    decay = jnp.exp(jnp.where(incl, gc - gr, -jnp.inf))
    g_last = gc[:, CHUNK - 1:CHUNK, :]
    e_in = jnp.exp(gc)
    e_out = jnp.exp(g_last - gc)
    e_last = jnp.exp(g_last)

    kb = k_p * bc
    kq = _bdot_nt(jnp.concatenate([kb, q_p], axis=1).astype(BF16), k_p.astype(BF16))
    yield
    m = jnp.where(strict, kq[:, 0:CHUNK] * decay, 0.0)
    sc = jnp.where(incl, kq[:, CHUNK:2 * CHUNK] * decay, 0.0).astype(BF16)
    eye = (row == col).astype(F32)[None]
    t = None
    blk = 1
    while blk < CHUNK:
        sel = ((row // (2 * blk) == col // (2 * blk)) & (row % (2 * blk) >= blk) & (col % (2 * blk) < blk))[None]
        off = jnp.where(sel, m, 0.0)
        if blk == 1:
            t = eye - off
        else:
            t_bf = t.astype(BF16)
            half = _bdot(t_bf, off.astype(BF16)).astype(BF16)
            yield
            t = t - _bdot(half, t_bf)
            yield
        blk *= 2
    rhs = jnp.concatenate([v_p * bc, kb * e_in], axis=-1).astype(BF16)
    uw = _bdot(t.astype(BF16), rhs).astype(BF16)
    yield
    sc_uw = _bdot(sc, uw)
    k_uw = _bdot_tn((k_p * e_out).astype(BF16), uw)
    yield
    q_eff = (q_p * e_in - sc_uw[:, :, HEAD_DIM:]).astype(BF16)
    st = st_ref[...]
    st_bf = st.astype(BF16)
    o = _bdot(q_eff, st_bf) + sc_uw[:, :, 0:HEAD_DIM]
    st_ref[...] = st * e_last + k_uw[:, :, 0:HEAD_DIM] - _bdot(k_uw[:, :, HEAD_DIM:].astype(BF16), st_bf)
    yield
    y = _gated_head_norm(_from_problems(o, nb), zg_s[slot], hn_ref[...])
    o_ref[...] = y.reshape(o_ref.shape).astype(o_ref.dtype)


def _dn_kernel(x_ref, side_ref, ng_ref, w_ref, cw_ref, ab_ref, hn_ref, tri_ref, o_ref,
               st_ref, hist_s, ext_s, q_s, k_s, v_s, zg_s, bg_s):
    j = pl.program_id(0)
    slot = j % 2
    bufs = (q_s, k_s, v_s, zg_s, bg_s)

    @pl.when(j == 0)
    def _():
        _zero(st_ref, hist_s, *bufs)

    _run_interleaved(
        _dn_front(x_ref, side_ref, ng_ref, w_ref, cw_ref, ab_ref, slot, hist_s, ext_s, *bufs),
        _dn_back(1 - slot, *bufs, st_ref, hn_ref, tri_ref, o_ref))


def _merge_kernel(x_ref, ng_ref, wg_ref, gb_ref, oa_ref, ob_ref, oc_ref, wb_ref, wo_ref, y_ref):
    x = x_ref[...]
    h = _rms_norm(x, ng_ref[...]).astype(BF16)
    gates = _sigmoid(_dot(h, wg_ref[...]) + gb_ref[...])
    merged = None
    for n, o_n in enumerate((oa_ref, ob_ref, oc_ref)):
        term = gates[:, n * D_MODEL:(n + 1) * D_MODEL] * _dot(o_n[...], wb_ref[n])
        merged = term if merged is None else merged + term
    y_ref[...] = x + _dot(merged.astype(BF16), wo_ref[...])


def _ffn_kernel(x_ref, ng_ref, wu_ref, cw_ref, cb_ref, wd_ref, fg_ref, y_ref, ext_s, hist_s, act_s,
                *, final_norm):
    @pl.when(pl.program_id(1) == 0)
    def _():
        hist_s[...] = jnp.zeros(hist_s.shape, hist_s.dtype)

    tile = x_ref.shape[0]
    x = x_ref[...]
    h = _rms_norm(x, ng_ref[...]).astype(BF16)
    ext_s[0:CONV_PAD, :] = hist_s[...]
    ext_s[CONV_PAD:CONV_PAD + tile, :] = _dot(h, wu_ref[...])
    hist_s[...] = ext_s[tile:tile + CONV_PAD, :]
    blk = 2 * LANES
    split = (FFN_HIDDEN // blk // 2 + 1) * blk
    y = x
    for j0 in range(0, FFN_HIDDEN, blk):
        halves = []
        for base in (j0, FFN_HIDDEN + j0):
            cols = slice(base, base + blk)
            u = cb_ref[0:1, cols] + cw_ref[0:1, cols] * ext_s[CONV_PAD - 2:CONV_PAD - 2 + tile, cols]
            for j in range(1, FFN_CONV):
                u = u + cw_ref[j:j + 1, cols] * ext_s[CONV_PAD - 2 + j:CONV_PAD - 2 + j + tile, cols]
            halves.append(u)
        act_s[:, j0:j0 + blk] = (_silu(halves[0]) * halves[1]).astype(BF16)
        if j0 + blk == split:
            y = y + _dot(act_s[:, 0:split], wd_ref[0:split, :])
    y = y + _dot(act_s[:, split:], wd_ref[split:, :])
    if final_norm:
        y = _rms_norm(y, fg_ref[...])
    y_ref[...] = y


def _const_spec(shape, grid_rank):
    nd = len(shape)
    if grid_rank == 1:
        return pl.BlockSpec(shape, lambda i, _nd=nd: (0,) * _nd, pipeline_mode=pl.Buffered(1))
    return pl.BlockSpec(shape, lambda b, i, _nd=nd: (0,) * _nd, pipeline_mode=pl.Buffered(1))


def _row_spec(tile, width, steps):
    return pl.BlockSpec((tile, width), lambda b, i, _s=steps: (b * _s + i, 0))


def _mixer_call(body, name, x3, consts, scratch, side_in=None, side_out=False):
    batch, seq, _ = x3.shape
    n = seq // CHUNK
    front = lambda j: (0, jnp.minimum(j, n - 1), 0)
    back = lambda j: (0, jnp.maximum(j - 1, 0), 0)
    side_spec = pl.BlockSpec((batch, CHUNK, LANES), front)
    streamed = [x3] + ([side_in] if side_in is not None else [])
    in_specs = ([pl.BlockSpec((batch, CHUNK, D_MODEL), front)] + [side_spec] * (len(streamed) - 1)
                + [_const_spec(c.shape, 1) for c in consts])
    out_specs = [pl.BlockSpec((batch, CHUNK, BRANCH_WIDTH), back)]
    out_shape = [jax.ShapeDtypeStruct((batch, seq, BRANCH_WIDTH), BF16)]
    if side_out:
        out_specs.append(side_spec)
        out_shape.append(jax.ShapeDtypeStruct((batch, seq, LANES), F32))
    outs = pl.pallas_call(
        body,
        grid=(n + 1,),
        in_specs=in_specs,
        out_specs=out_specs,
        out_shape=out_shape,
        scratch_shapes=scratch,
        compiler_params=pltpu.CompilerParams(dimension_semantics=("arbitrary",), vmem_limit_bytes=VMEM_LIMIT),
        name=name,
    )(*streamed, *consts)
    return outs if side_out else outs[0]


def _gla_scratch(batch, proj_width, key_width):
    rows = batch * CHUNK
    w = BRANCH_WIDTH
    return [pltpu.VMEM((N_HEADS * batch, HEAD_DIM, HEAD_DIM), F32),
            pltpu.VMEM((2, rows, key_width), BF16), pltpu.VMEM((2, rows, key_width), BF16),
            pltpu.VMEM((2, rows, w), BF16),
            pltpu.VMEM((2, rows, w), F32),
            pltpu.VMEM((2, batch, 2 * CHUNK, key_width), BF16),
            pltpu.VMEM((2, batch, 2 * CHUNK, key_width), F32),
            pltpu.SMEM((2,), jnp.int32),
            pltpu.VMEM((rows, proj_width), F32)]


def _dn_scratch(batch):
    rows = batch * CHUNK
    prob = (2, N_HEADS * batch, CHUNK, HEAD_DIM)
    return [pltpu.VMEM((N_HEADS * batch, HEAD_DIM, HEAD_DIM), F32),
            pltpu.VMEM((batch, CONV_PAD, 3 * BRANCH_WIDTH), F32),
            pltpu.VMEM((batch, CHUNK + CONV_PAD, 3 * BRANCH_WIDTH), F32),
            pltpu.VMEM(prob, F32), pltpu.VMEM(prob, F32), pltpu.VMEM(prob, F32),
            pltpu.VMEM((2, rows, BRANCH_WIDTH), F32),
            pltpu.VMEM((2, 2, rows, LANES), F32)]


def _params2():
    return pltpu.CompilerParams(dimension_semantics=("arbitrary", "arbitrary"), vmem_limit_bytes=VMEM_LIMIT)


def _merge_call(x2, batch, consts_a, branches, consts_b):
    rows = x2.shape[0]
    steps = rows // batch // TILE_MERGE
    in_specs = ([_row_spec(TILE_MERGE, D_MODEL, steps)] + [_const_spec(c.shape, 2) for c in consts_a]
                + [_row_spec(TILE_MERGE, BRANCH_WIDTH, steps) for _ in branches]
                + [_const_spec(c.shape, 2) for c in consts_b])
    return pl.pallas_call(
        _merge_kernel,
        grid=(batch, steps),
        in_specs=in_specs,
        out_specs=_row_spec(TILE_MERGE, D_MODEL, steps),
        out_shape=jax.ShapeDtypeStruct((rows, D_MODEL), F32),
        compiler_params=_params2(),
        name="merge",
    )(x2, *consts_a, *branches, *consts_b)


def _ffn_call(x2, batch, consts, final_norm):
    rows = x2.shape[0]
    steps = rows // batch // TILE_FFN
    in_specs = [_row_spec(TILE_FFN, D_MODEL, steps)] + [_const_spec(c.shape, 2) for c in consts]
    return pl.pallas_call(
        functools.partial(_ffn_kernel, final_norm=final_norm),
        grid=(batch, steps),
        in_specs=in_specs,
        out_specs=_row_spec(TILE_FFN, D_MODEL, steps),
        out_shape=jax.ShapeDtypeStruct((rows, D_MODEL), F32),
        scratch_shapes=[pltpu.VMEM((TILE_FFN + CONV_PAD, 2 * FFN_HIDDEN), F32),
                        pltpu.VMEM((CONV_PAD, 2 * FFN_HIDDEN), F32),
                        pltpu.VMEM((TILE_FFN, FFN_HIDDEN), BF16)],
        compiler_params=_params2(),
        name="ffn_final" if final_norm else "ffn",
    )(x2, *consts)


def _pad_cols(wcols, width):
    return jnp.pad(wcols, [(0, 0)] * (wcols.ndim - 1) + [(0, width - wcols.shape[-1])])


def _row(v):
    return v.reshape(1, -1).astype(F32)


def kernel(x, norm_mix_g, w_in, hg_lower_bounds, hg_norm_g, gla_gk_w2, gla_gk_b, gla_norm_g, dn_conv_w,
           dn_A_log, dn_dt_bias, dn_norm_g, w_branch, gate_b, w_out, norm_ffn_g, w_up, ffn_conv_w,
           ffn_conv_b, w_down, norm_final_g):
    batch, seq, d = x.shape
    depth = w_in.shape[0]
    assert d == D_MODEL and seq % max(CHUNK, TILE_MERGE, TILE_FFN) == 0

    lb_all = jnp.cumsum(jax.nn.softmax(hg_lower_bounds.astype(F32), axis=0), axis=0)
    lb_all = lb_all - lb_all[:1]
    edge = jnp.asarray(_edge_table(), BF16)
    level = jnp.asarray(_level_table(), BF16)
    lmask = jnp.asarray(_level_masks(), F32)
    tri1 = np.tril(np.ones((CHUNK, CHUNK), np.float32))
    tri = jnp.asarray(np.concatenate([tri1, tri1], axis=1), BF16)

    hw = BRANCH_WIDTH
    c0 = 4 * hw
    c1 = c0 + 2 * N_HEADS * GLA_DK
    c2 = c1 + hw
    c3 = c2 + GLA_RANK
    c4 = c3 + hw
    c5 = c4 + 3 * hw
    c6 = c5 + hw
    c7 = c6 + N_HEADS
    c8 = c7 + N_HEADS

    x3 = x
    for l in range(depth):
        wl = w_in[l]
        ng = _row(norm_mix_g[l])
        lb = lb_all[l]
        lb_rows = jnp.stack([lb, jnp.log(lb), jnp.log1p(-lb)]).astype(F32)
        o_hg = _mixer_call(
            functools.partial(_gla_kernel, "hgrn2", 1), "hgrn2", x3,
            [ng, wl[:, :c0].astype(BF16), lb_rows, _row(jnp.tile(hg_norm_g[l], N_HEADS)), edge, level, lmask],
            _gla_scratch(batch, c0, hw))

        gate_cols = _pad_cols(jnp.concatenate([wl[:, c2:c3], wl[:, c6:c8]], axis=1), LANES)
        w_gla = jnp.concatenate([wl[:, c0:c2], wl[:, c3:c4], gate_cols], axis=1).astype(BF16)
        w2 = jnp.pad(gla_gk_w2[l], ((0, LANES - GLA_RANK), (0, 0))).astype(BF16)
        b2 = _row(gla_gk_b[l])
        o_gla, dn_gates = _mixer_call(
            functools.partial(_gla_kernel, "gla", 2), "gla", x3,
            [ng, w_gla, w2, b2, _row(jnp.tile(gla_norm_g[l], N_HEADS)), edge, level, lmask],
            _gla_scratch(batch, w_gla.shape[1], N_HEADS * GLA_DK), side_out=True)

        ab = jnp.zeros((2, LANES), F32)
        ab = ab.at[0, DN_LANE0 + N_HEADS:DN_LANE0 + 2 * N_HEADS].set(dn_A_log[l].astype(F32))
        ab = ab.at[1, DN_LANE0 + N_HEADS:DN_LANE0 + 2 * N_HEADS].set(dn_dt_bias[l].astype(F32))
        o_dn = _mixer_call(
            _dn_kernel, "deltanet", x3,
            [ng, wl[:, c4:c6].astype(BF16), dn_conv_w[l].astype(F32), ab, _row(jnp.tile(dn_norm_g[l], N_HEADS)), tri],
            _dn_scratch(batch), side_in=dn_gates)

        x2 = _merge_call(
            x3.reshape(batch * seq, d), batch, [ng, wl[:, c8:].astype(BF16), _row(gate_b[l])],
            [o.reshape(batch * seq, hw) for o in (o_hg, o_gla, o_dn)],
            [w_branch[l].astype(BF16), w_out[l].astype(BF16)])

        x2 = _ffn_call(
            x2, batch,
            [_row(norm_ffn_g[l]), w_up[l].astype(BF16), ffn_conv_w[l].astype(F32), _row(ffn_conv_b[l]),
             w_down[l].astype(BF16), _row(norm_final_g)],
            final_norm=(l == depth - 1))
        x3 = x2.reshape(batch, seq, d)
    return x3
```

```python
import functools

import jax
import jax.numpy as jnp
import numpy as np
from jax import lax
from jax.experimental import pallas as pl
from jax.experimental.pallas import tpu as pltpu

F32 = jnp.float32
BF16 = jnp.bfloat16

D_MODEL = 1024
CHUNK = 64
NORM_EPS = 1e-6
N_HEADS = 4
HEAD_DIM = 128
BRANCH_WIDTH = N_HEADS * HEAD_DIM
GLA_DK = 64
GLA_RANK = 16
GLA_GATE_NORMALIZER = 16.0
DN_LANE0 = GLA_RANK
DN_CONV = 4
N_BRANCH = 3
FFN_HIDDEN = 2816
FFN_CONV = 3
LANES = 128
CONV_PAD = 8
N_LEVELS = 6
LOG2E = 1.4426950408889634
SAFE_LOG2_DECAY = -160.0

TILE_MERGE = 512
TILE_FFN = 512
VMEM_LIMIT = 56 * 1024 * 1024


def _sigmoid(x):
    return 1.0 / (1.0 + jnp.exp(-x))


def _silu(x):
    return x * _sigmoid(x)


def _softplus(x):
    return jnp.maximum(x, 0.0) + jnp.log1p(jnp.exp(-jnp.abs(x)))


def _log_sigmoid(x):
    return -_softplus(-x)


def _rms_norm(x, g):
    y = x * lax.rsqrt(jnp.mean(x * x, axis=-1, keepdims=True) + NORM_EPS)
    return y * g


def _dot(a, b):
    return jnp.dot(a, b, preferred_element_type=F32)


def _bdot(a, b):
    return lax.dot_general(a, b, (((2,), (1,)), ((0,), (0,))), preferred_element_type=F32)


def _bdot_nt(a, b):
    return lax.dot_general(a, b, (((2,), (2,)), ((0,), (0,))), preferred_element_type=F32)


def _bdot_tn(a, b):
    return lax.dot_general(a, b, (((1,), (1,)), ((0,), (0,))), preferred_element_type=F32)


def _split_bf16(x):
    hi = x.astype(BF16)
    lo = (x - hi.astype(F32)).astype(BF16)
    return hi, lo


def _head(h):
    return slice(h * HEAD_DIM, (h + 1) * HEAD_DIM)


def _heads_to_problems(x3):
    return jnp.concatenate([x3[:, :, _head(h)] for h in range(N_HEADS)], axis=0)


def _to_problems(x2d, nb):
    return _heads_to_problems(x2d.reshape(nb, x2d.shape[0] // nb, x2d.shape[1]))


def _key_problems(x, nb, query=False):
    x3 = x if x.ndim == 3 else x.reshape(nb, x.shape[0] // nb, x.shape[1])
    if x3.shape[-1] == BRANCH_WIDTH:
        return _heads_to_problems(x3)
    lane = lax.broadcasted_iota(jnp.int32, (1, 1, LANES), 2)
    parts = []
    for h in range(N_HEADS):
        xp = x3[:, :, (h // 2) * LANES:(h // 2 + 1) * LANES]
        if query:
            own = (lane >= LANES // 2) if h % 2 else (lane < LANES // 2)
            xp = jnp.where(own, xp, jnp.zeros_like(xp))
        parts.append(xp)
    return jnp.concatenate(parts, axis=0)


def _from_problems(xp, nb):
    rows = nb * xp.shape[1]
    return jnp.concatenate([xp[h * nb:(h + 1) * nb].reshape(rows, HEAD_DIM) for h in range(N_HEADS)], axis=-1)


def _run_interleaved(*stage_generators):
    live = list(stage_generators)
    while live:
        live = [s for s in live if next(s, True) is None]


def _causal_masks():
    row = lax.broadcasted_iota(jnp.int32, (CHUNK, CHUNK), 0)
    col = lax.broadcasted_iota(jnp.int32, (CHUNK, CHUNK), 1)
    return row, col, (row >= col)[None], (row > col)[None]


def _prefix_suffix_rows():
    t = np.arange(CHUNK)[:, None]
    u = np.arange(CHUNK)[None, :]
    return [u <= t, u > t]


def _edge_table():
    tab = np.concatenate(_prefix_suffix_rows(), axis=0).astype(np.float32)
    return np.concatenate([tab, tab], axis=1)


def _level_table():
    t = np.arange(CHUNK)
    u = np.arange(CHUNK)
    rows = []
    for lvl in range(N_LEVELS):
        p = CHUNK >> (lvl + 1)
        mid = (t // (2 * p)) * (2 * p) + p - 1
        upper = t > mid
        rows.append(np.where(upper[:, None], (u[None, :] > mid[:, None]) & (u[None, :] <= t[:, None]),
                             (u[None, :] > t[:, None]) & (u[None, :] <= mid[:, None])))
    tab = np.concatenate(rows, axis=0).astype(np.float32)
    return np.concatenate([tab, tab], axis=1)


def _level_masks():
    t = np.arange(CHUNK)[:, None]
    s = np.arange(CHUNK)[None, :]
    out = []
    for lvl in range(N_LEVELS):
        p = CHUNK >> (lvl + 1)
        out.append((t // (2 * p) == s // (2 * p)) & (t % (2 * p) >= p) & (s % (2 * p) < p))
    out.append(t == s)
    return np.stack(out).astype(np.float32)


def _gated_head_norm(o, zg, g):
    outs = []
    for h in range(N_HEADS):
        oh = o[:, _head(h)]
        inv = lax.rsqrt(jnp.mean(oh * oh, axis=-1, keepdims=True) + NORM_EPS)
        outs.append(oh * inv)
    return jnp.concatenate(outs, axis=-1) * g * zg


def _zero(*refs):
    for r in refs:
        r[...] = jnp.zeros(r.shape, r.dtype)


def _gla_front(kind, x_ref, ng_ref, w_ref, aux_refs, edge_ref, slot, q_s, k_s, v_s, zg_s, gs_s, tab_s, safe_s, p_s):
    nb, _, d = x_ref.shape
    rows = nb * CHUNK
    w = BRANCH_WIDTH
    h = _rms_norm(x_ref[...].reshape(rows, d), ng_ref[...]).astype(BF16)
    yield

    def project(c0, c1):
        p_s[:, c0:c1] = _dot(h, w_ref[:, c0:c1])

    def store_decay(g):
        g_hi, g_lo = _split_bf16(g * LOG2E)
        for b in range(nb):
            r = slice(b * CHUNK, (b + 1) * CHUNK)
            gs_s[slot, b] = jnp.concatenate([g_hi[r], g_lo[r]], axis=0)

    if kind == "hgrn2":
        lb_ref, = aux_refs
        for c0 in range(0, 4 * w, w):
            project(c0, c0 + w)
            yield
        q_s[slot] = (_silu(p_s[:, 0:w]) * (HEAD_DIM ** -0.5)).astype(BF16)
        yield
        z = p_s[:, w:2 * w]
        lb = lb_ref[0:1, :]
        log_lb = lb_ref[1:2, :]
        c = lb_ref[2:3, :] + _log_sigmoid(z)
        store_decay(jnp.maximum(log_lb, c) + jnp.log1p(jnp.exp(-jnp.abs(log_lb - c))))
        k_s[slot] = ((1.0 - lb) * _sigmoid(-z)).astype(BF16)
        yield
        v_s[slot] = p_s[:, 2 * w:3 * w].astype(BF16)
        zg_s[slot] = _silu(p_s[:, 3 * w:4 * w])
        yield
    else:
        w2_ref, b2_ref, side_ref = aux_refs
        kw = q_s.shape[-1]
        low = 2 * kw + 2 * w
        project(low, low + LANES)
        side_ref[...] = p_s[:, low:low + LANES].reshape(side_ref.shape)
        yield
        project(0, 2 * kw)
        yield
        gk = _dot(p_s[:, low:low + LANES].astype(BF16), w2_ref[...]) + b2_ref[...]
        store_decay(_log_sigmoid(gk) * (1.0 / GLA_GATE_NORMALIZER))
        yield
        project(2 * kw, 2 * kw + w)
        yield
    edge = edge_ref[...]
    low_mark = None
    for b in range(nb):
        tab = _dot(edge, gs_s[slot, b])
        tab_s[slot, b] = tab
        total = tab[CHUNK - 1:CHUNK, :]
        low_mark = total if low_mark is None else jnp.minimum(low_mark, total)
        if b % 4 == 3:
            yield
    safe_s[slot] = (jnp.min(low_mark) >= SAFE_LOG2_DECAY).astype(jnp.int32)
    if kind != "hgrn2":
        project(2 * kw + w, 2 * kw + 2 * w)
        yield
        q_s[slot] = (p_s[:, 0:kw] * (GLA_DK ** -0.5)).astype(BF16)
        k_s[slot] = p_s[:, kw:2 * kw].astype(BF16)
        yield
        v_s[slot] = p_s[:, 2 * kw:2 * kw + w].astype(BF16)
        zg_s[slot] = _silu(p_s[:, 2 * kw + w:2 * kw + 2 * w])


def _gla_scores_centered(q, k, tab, nb):
    _, _, incl, _ = _causal_masks()
    pre = tab[:, 0:CHUNK, :]
    mid = 0.5 * pre[:, CHUNK - 1:CHUNK, :]
    rows = nb * CHUNK
    qf = q * jnp.exp2(pre - mid).astype(BF16).reshape(rows, -1)
    kf = k * jnp.exp2(mid - pre).astype(BF16).reshape(rows, -1)
    yield
    s = _bdot_nt(_key_problems(qf, nb, query=True), _key_problems(kf, nb))
    yield jnp.where(incl, s, 0.0)


def _gla_scores_levels(q, k, gs, level_ref, lmask_ref, nb):
    sc = _bdot_nt(_key_problems(q, nb, query=True), _key_problems(k, nb)) * lmask_ref[N_LEVELS][None]
    for l in range(N_LEVELS):
        wt = level_ref[l * CHUNK:(l + 1) * CHUNK, :]
        e = jnp.exp2(jnp.concatenate([_dot(wt, gs[b]) for b in range(nb)], axis=0)).astype(BF16)
        sc = sc + _bdot_nt(_key_problems(q * e, nb, query=True), _key_problems(k * e, nb)) * lmask_ref[l][None]
    return sc


def _gla_back(centered, slot, q_s, k_s, v_s, zg_s, gs_s, tab_s, st_ref, hn_ref, level_ref, lmask_ref, o_ref):
    nb = o_ref.shape[0]
    rows = nb * CHUNK
    q = q_s[slot]
    k = k_s[slot]
    tab = tab_s[slot]
    if centered:
        scores = _gla_scores_centered(q, k, tab, nb)
        yield next(scores)
        sc = next(scores)
    else:
        sc = _gla_scores_levels(q, k, [gs_s[slot, b] for b in range(nb)], level_ref, lmask_ref, nb)
    yield
    pre = tab[:, 0:CHUNK, :]
    e_in = jnp.exp2(pre).astype(BF16).reshape(rows, -1)
    e_out = jnp.exp2(tab[:, CHUNK:2 * CHUNK, :]).astype(BF16).reshape(rows, -1)
    e_last = _key_problems(jnp.exp2(pre[:, CHUNK - 1:CHUNK, :]), nb)
    v_p = _to_problems(v_s[slot], nb)
    st = st_ref[...]
    o = _bdot(sc.astype(BF16), v_p) + _bdot(_key_problems(q * e_in, nb, query=True), st.astype(BF16))
    yield
    decay = jnp.swapaxes(jnp.broadcast_to(e_last, st.shape), 1, 2)
    st_ref[...] = st * decay + _bdot_tn(_key_problems(k * e_out, nb), v_p)
    yield
    y = _gated_head_norm(_from_problems(o, nb), zg_s[slot], hn_ref[...])
    o_ref[...] = y.reshape(o_ref.shape).astype(o_ref.dtype)


def _gla_kernel(kind, n_aux, x_ref, ng_ref, w_ref, *rest):
    aux_refs, rest = rest[:n_aux], rest[n_aux:]
    (hn_ref, edge_ref, level_ref, lmask_ref, o_ref), rest = rest[:5], rest[5:]
    if kind != "hgrn2":
        aux_refs, rest = aux_refs + rest[:1], rest[1:]
    st_ref, q_s, k_s, v_s, zg_s, gs_s, tab_s, safe_s, p_s = rest
    j = pl.program_id(0)
    slot = j % 2
    prev = 1 - slot
    bufs = (q_s, k_s, v_s, zg_s, gs_s, tab_s)

    @pl.when(j == 0)
    def _():
        _zero(st_ref, *bufs)
        safe_s[0] = 1
        safe_s[1] = 1

    def front():
        return _gla_front(kind, x_ref, ng_ref, w_ref, aux_refs, edge_ref, slot, *bufs, safe_s, p_s)

    def back(centered):
        return _gla_back(centered, prev, *bufs, st_ref, hn_ref, level_ref, lmask_ref, o_ref)

    @pl.when(safe_s[prev] == 1)
    def _():
        _run_interleaved(front(), back(True))

    @pl.when(safe_s[prev] != 1)
    def _():
        _run_interleaved(back(False))
        _run_interleaved(front())


def _dn_front(x_ref, side_ref, ng_ref, w_ref, cw_ref, ab_ref, slot, hist_s, ext_s, q_s, k_s, v_s, zg_s, bg_s):
    nb, _, d = x_ref.shape
    rows = nb * CHUNK
    w = BRANCH_WIDTH
    h = _rms_norm(x_ref[...].reshape(rows, d), ng_ref[...]).astype(BF16)
    yield
    for part, dst in enumerate((q_s, k_s, v_s)):
        cols = slice(part * w, (part + 1) * w)
        ext_s[:, 0:CONV_PAD, cols] = hist_s[:, :, cols]
        ext_s[:, CONV_PAD:CONV_PAD + CHUNK, cols] = _dot(h, w_ref[:, cols]).reshape(nb, CHUNK, w)
        hist_s[:, :, cols] = ext_s[:, CHUNK:CHUNK + CONV_PAD, cols]
        yield
        ext = ext_s[:, :, cols]
        conv = cw_ref[DN_CONV - 1:DN_CONV, cols][None] * ext[:, CONV_PAD:CONV_PAD + CHUNK]
        for back in range(1, DN_CONV):
            tap = DN_CONV - 1 - back
            shifted = pltpu.roll(ext, back, axis=1)
            conv = conv + cw_ref[tap:tap + 1, cols][None] * shifted[:, CONV_PAD:CONV_PAD + CHUNK]
        act = _silu(conv.reshape(rows, w))
        if part < 2:
            scale = HEAD_DIM ** -0.5 if part == 0 else 1.0
            heads = []
            for hd in range(N_HEADS):
                xh = act[:, _head(hd)]
                heads.append(xh * lax.rsqrt(jnp.sum(xh * xh, axis=-1, keepdims=True) + NORM_EPS) * scale)
            act = jnp.concatenate(heads, axis=-1)
        dst[slot] = _to_problems(act, nb)
        yield
    zg_s[slot] = _silu(_dot(h, w_ref[:, 3 * w:4 * w]))
    yield
    ba = side_ref[...].reshape(rows, LANES)
    bg_s[slot, 0] = _sigmoid(ba)
    bg_s[slot, 1] = -jnp.exp(ab_ref[0:1, :]) * _softplus(ba + ab_ref[1:2, :])


def _dn_back(slot, q_s, k_s, v_s, zg_s, bg_s, st_ref, hn_ref, tri_ref, o_ref):
    nb = o_ref.shape[0]
    q_p = q_s[slot]
    k_p = k_s[slot]
    v_p = v_s[slot]
    beta = bg_s[slot, 0]
    g_hi, g_lo = _split_bf16(bg_s[slot, 1])
    tri = tri_ref[...]
    gcols, grows = [], []
    for b in range(nb):
        r = slice(b * CHUNK, (b + 1) * CHUNK)
        gcum = _dot(tri, jnp.concatenate([g_hi[r], g_lo[r]], axis=0))
        gcols.append(gcum)
        grows.append(gcum.T)
    ln = lambda h: slice(DN_LANE0 + N_HEADS + h, DN_LANE0 + N_HEADS + h + 1)
    gc = jnp.concatenate([gcols[b][None, :, ln(h)] for h in range(N_HEADS) for b in range(nb)], axis=0)
    gr = jnp.concatenate([grows[b][None, ln(h), :] for h in range(N_HEADS) for b in range(nb)], axis=0)
    beta3 = beta.reshape(nb, CHUNK, LANES)
    bc = jnp.concatenate([beta3[:, :, DN_LANE0 + h:DN_LANE0 + h + 1] for h in range(N_HEADS)], axis=0)

    row, col, incl, strict = _causal_masks()
    decay = jnp.exp(jnp.where(incl, gc - gr, -jnp.inf))
    g_last = gc[:, CHUNK - 1:CHUNK, :]
    e_in = jnp.exp(gc)
    e_out = jnp.exp(g_last - gc)
    e_last = jnp.exp(g_last)

    kb = k_p * bc
    kq = _bdot_nt(jnp.concatenate([kb, q_p], axis=1).astype(BF16), k_p.astype(BF16))
    yield
    m = jnp.where(strict, kq[:, 0:CHUNK] * decay, 0.0)
    sc = jnp.where(incl, kq[:, CHUNK:2 * CHUNK] * decay, 0.0).astype(BF16)
    eye = (row == col).astype(F32)[None]
    t = None
    blk = 1
    while blk < CHUNK:
        sel = ((row // (2 * blk) == col // (2 * blk)) & (row % (2 * blk) >= blk) & (col % (2 * blk) < blk))[None]
        off = jnp.where(sel, m, 0.0)
        if blk == 1:
            t = eye - off
        else:
            t_bf = t.astype(BF16)
            half = _bdot(t_bf, off.astype(BF16)).astype(BF16)
            yield
            t = t - _bdot(half, t_bf)
            yield
        blk *= 2
    rhs = jnp.concatenate([v_p * bc, kb * e_in], axis=-1).astype(BF16)
    uw = _bdot(t.astype(BF16), rhs).astype(BF16)
    yield
    sc_uw = _bdot(sc, uw)
    k_uw = _bdot_tn((k_p * e_out).astype(BF16), uw)
    yield
    q_eff = (q_p * e_in - sc_uw[:, :, HEAD_DIM:]).astype(BF16)
    st = st_ref[...]
    st_bf = st.astype(BF16)
    o = _bdot(q_eff, st_bf) + sc_uw[:, :, 0:HEAD_DIM]
    st_ref[...] = st * e_last + k_uw[:, :, 0:HEAD_DIM] - _bdot(k_uw[:, :, HEAD_DIM:].astype(BF16), st_bf)
    yield
    y = _gated_head_norm(_from_problems(o, nb), zg_s[slot], hn_ref[...])
    o_ref[...] = y.reshape(o_ref.shape).astype(o_ref.dtype)


def _dn_kernel(x_ref, side_ref, ng_ref, w_ref, cw_ref, ab_ref, hn_ref, tri_ref, o_ref,
               st_ref, hist_s, ext_s, q_s, k_s, v_s, zg_s, bg_s):
    j = pl.program_id(0)
    slot = j % 2
    bufs = (q_s, k_s, v_s, zg_s, bg_s)

    @pl.when(j == 0)
    def _():
        _zero(st_ref, hist_s, *bufs)

    _run_interleaved(
        _dn_front(x_ref, side_ref, ng_ref, w_ref, cw_ref, ab_ref, slot, hist_s, ext_s, *bufs),
        _dn_back(1 - slot, *bufs, st_ref, hn_ref, tri_ref, o_ref))


def _merge_kernel(x_ref, ng_ref, wg_ref, gb_ref, oa_ref, ob_ref, oc_ref, wb_ref, wo_ref, y_ref):
    x = x_ref[...]
    h = _rms_norm(x, ng_ref[...]).astype(BF16)
    gates = _sigmoid(_dot(h, wg_ref[...]) + gb_ref[...])
    merged = None
    for n, o_n in enumerate((oa_ref, ob_ref, oc_ref)):
        term = gates[:, n * D_MODEL:(n + 1) * D_MODEL] * _dot(o_n[...], wb_ref[n])
        merged = term if merged is None else merged + term
    y_ref[...] = x + _dot(merged.astype(BF16), wo_ref[...])


def _ffn_kernel(x_ref, ng_ref, wu_ref, cw_ref, cb_ref, wd_ref, fg_ref, y_ref, ext_s, hist_s, act_s,
                *, final_norm):
    @pl.when(pl.program_id(1) == 0)
    def _():
        hist_s[...] = jnp.zeros(hist_s.shape, hist_s.dtype)

    tile = x_ref.shape[0]
    x = x_ref[...]
    h = _rms_norm(x, ng_ref[...]).astype(BF16)
    ext_s[0:CONV_PAD, :] = hist_s[...]
    ext_s[CONV_PAD:CONV_PAD + tile, :] = _dot(h, wu_ref[...])
    hist_s[...] = ext_s[tile:tile + CONV_PAD, :]
    blk = 2 * LANES
    split = (FFN_HIDDEN // blk // 2 + 1) * blk
    y = x
    for j0 in range(0, FFN_HIDDEN, blk):
        halves = []
        for base in (j0, FFN_HIDDEN + j0):
            cols = slice(base, base + blk)
            u = cb_ref[0:1, cols] + cw_ref[0:1, cols] * ext_s[CONV_PAD - 2:CONV_PAD - 2 + tile, cols]
            for j in range(1, FFN_CONV):
                u = u + cw_ref[j:j + 1, cols] * ext_s[CONV_PAD - 2 + j:CONV_PAD - 2 + j + tile, cols]
            halves.append(u)
        act_s[:, j0:j0 + blk] = (_silu(halves[0]) * halves[1]).astype(BF16)
        if j0 + blk == split:
            y = y + _dot(act_s[:, 0:split], wd_ref[0:split, :])
    y = y + _dot(act_s[:, split:], wd_ref[split:, :])
    if final_norm:
        y = _rms_norm(y, fg_ref[...])
    y_ref[...] = y


def _const_spec(shape, grid_rank):
    nd = len(shape)
    if grid_rank == 1:
        return pl.BlockSpec(shape, lambda i, _nd=nd: (0,) * _nd, pipeline_mode=pl.Buffered(1))
    return pl.BlockSpec(shape, lambda b, i, _nd=nd: (0,) * _nd, pipeline_mode=pl.Buffered(1))


def _row_spec(tile, width, steps):
    return pl.BlockSpec((tile, width), lambda b, i, _s=steps: (b * _s + i, 0))


def _mixer_call(body, name, x3, consts, scratch, side_in=None, side_out=False):
    batch, seq, _ = x3.shape
    n = seq // CHUNK
    front = lambda j: (0, jnp.minimum(j, n - 1), 0)
    back = lambda j: (0, jnp.maximum(j - 1, 0), 0)
    side_spec = pl.BlockSpec((batch, CHUNK, LANES), front)
    streamed = [x3] + ([side_in] if side_in is not None else [])
    in_specs = ([pl.BlockSpec((batch, CHUNK, D_MODEL), front)] + [side_spec] * (len(streamed) - 1)
                + [_const_spec(c.shape, 1) for c in consts])
    out_specs = [pl.BlockSpec((batch, CHUNK, BRANCH_WIDTH), back)]
    out_shape = [jax.ShapeDtypeStruct((batch, seq, BRANCH_WIDTH), BF16)]
    if side_out:
        out_specs.append(side_spec)
        out_shape.append(jax.ShapeDtypeStruct((batch, seq, LANES), F32))
    outs = pl.pallas_call(
        body,
        grid=(n + 1,),
        in_specs=in_specs,
        out_specs=out_specs,
        out_shape=out_shape,
        scratch_shapes=scratch,
        compiler_params=pltpu.CompilerParams(dimension_semantics=("arbitrary",), vmem_limit_bytes=VMEM_LIMIT),
        name=name,
    )(*streamed, *consts)
    return outs if side_out else outs[0]


def _gla_scratch(batch, proj_width, key_width):
    rows = batch * CHUNK
    w = BRANCH_WIDTH
    return [pltpu.VMEM((N_HEADS * batch, HEAD_DIM, HEAD_DIM), F32),
            pltpu.VMEM((2, rows, key_width), BF16), pltpu.VMEM((2, rows, key_width), BF16),
            pltpu.VMEM((2, rows, w), BF16),
            pltpu.VMEM((2, rows, w), F32),
            pltpu.VMEM((2, batch, 2 * CHUNK, key_width), BF16),
            pltpu.VMEM((2, batch, 2 * CHUNK, key_width), F32),
            pltpu.SMEM((2,), jnp.int32),
            pltpu.VMEM((rows, proj_width), F32)]


def _dn_scratch(batch):
    rows = batch * CHUNK
    prob = (2, N_HEADS * batch, CHUNK, HEAD_DIM)
    return [pltpu.VMEM((N_HEADS * batch, HEAD_DIM, HEAD_DIM), F32),
            pltpu.VMEM((batch, CONV_PAD, 3 * BRANCH_WIDTH), F32),
            pltpu.VMEM((batch, CHUNK + CONV_PAD, 3 * BRANCH_WIDTH), F32),
            pltpu.VMEM(prob, F32), pltpu.VMEM(prob, F32), pltpu.VMEM(prob, F32),
            pltpu.VMEM((2, rows, BRANCH_WIDTH), F32),
            pltpu.VMEM((2, 2, rows, LANES), F32)]


def _params2():
    return pltpu.CompilerParams(dimension_semantics=("arbitrary", "arbitrary"), vmem_limit_bytes=VMEM_LIMIT)


def _merge_call(x2, batch, consts_a, branches, consts_b):
    rows = x2.shape[0]
    steps = rows // batch // TILE_MERGE
    in_specs = ([_row_spec(TILE_MERGE, D_MODEL, steps)] + [_const_spec(c.shape, 2) for c in consts_a]
                + [_row_spec(TILE_MERGE, BRANCH_WIDTH, steps) for _ in branches]
                + [_const_spec(c.shape, 2) for c in consts_b])
    return pl.pallas_call(
        _merge_kernel,
        grid=(batch, steps),
        in_specs=in_specs,
        out_specs=_row_spec(TILE_MERGE, D_MODEL, steps),
        out_shape=jax.ShapeDtypeStruct((rows, D_MODEL), F32),
        compiler_params=_params2(),
        name="merge",
    )(x2, *consts_a, *branches, *consts_b)


def _ffn_call(x2, batch, consts, final_norm):
    rows = x2.shape[0]
    steps = rows // batch // TILE_FFN
    in_specs = [_row_spec(TILE_FFN, D_MODEL, steps)] + [_const_spec(c.shape, 2) for c in consts]
    return pl.pallas_call(
        functools.partial(_ffn_kernel, final_norm=final_norm),
        grid=(batch, steps),
        in_specs=in_specs,
        out_specs=_row_spec(TILE_FFN, D_MODEL, steps),
        out_shape=jax.ShapeDtypeStruct((rows, D_MODEL), F32),
        scratch_shapes=[pltpu.VMEM((TILE_FFN + CONV_PAD, 2 * FFN_HIDDEN), F32),
                        pltpu.VMEM((CONV_PAD, 2 * FFN_HIDDEN), F32),
                        pltpu.VMEM((TILE_FFN, FFN_HIDDEN), BF16)],
        compiler_params=_params2(),
        name="ffn_final" if final_norm else "ffn",
    )(x2, *consts)


def _pad_cols(wcols, width):
    return jnp.pad(wcols, [(0, 0)] * (wcols.ndim - 1) + [(0, width - wcols.shape[-1])])


def _row(v):
    return v.reshape(1, -1).astype(F32)


def kernel(x, norm_mix_g, w_in, hg_lower_bounds, hg_norm_g, gla_gk_w2, gla_gk_b, gla_norm_g, dn_conv_w,
           dn_A_log, dn_dt_bias, dn_norm_g, w_branch, gate_b, w_out, norm_ffn_g, w_up, ffn_conv_w,
           ffn_conv_b, w_down, norm_final_g):
    batch, seq, d = x.shape
    depth = w_in.shape[0]
    assert d == D_MODEL and seq % max(CHUNK, TILE_MERGE, TILE_FFN) == 0

    lb_all = jnp.cumsum(jax.nn.softmax(hg_lower_bounds.astype(F32), axis=0), axis=0)
    lb_all = lb_all - lb_all[:1]
    edge = jnp.asarray(_edge_table(), BF16)
    level = jnp.asarray(_level_table(), BF16)
    lmask = jnp.asarray(_level_masks(), F32)
    tri1 = np.tril(np.ones((CHUNK, CHUNK), np.float32))
    tri = jnp.asarray(np.concatenate([tri1, tri1], axis=1), BF16)

    hw = BRANCH_WIDTH
    c0 = 4 * hw
    c1 = c0 + 2 * N_HEADS * GLA_DK
    c2 = c1 + hw
    c3 = c2 + GLA_RANK
    c4 = c3 + hw
    c5 = c4 + 3 * hw
    c6 = c5 + hw
    c7 = c6 + N_HEADS
    c8 = c7 + N_HEADS

    x3 = x
    for l in range(depth):
        wl = w_in[l]
        ng = _row(norm_mix_g[l])
        lb = lb_all[l]
        lb_rows = jnp.stack([lb, jnp.log(lb), jnp.log1p(-lb)]).astype(F32)
        o_hg = _mixer_call(
            functools.partial(_gla_kernel, "hgrn2", 1), "hgrn2", x3,
            [ng, wl[:, :c0].astype(BF16), lb_rows, _row(jnp.tile(hg_norm_g[l], N_HEADS)), edge, level, lmask],
            _gla_scratch(batch, c0, hw))

        gate_cols = _pad_cols(jnp.concatenate([wl[:, c2:c3], wl[:, c6:c8]], axis=1), LANES)
        w_gla = jnp.concatenate([wl[:, c0:c2], wl[:, c3:c4], gate_cols], axis=1).astype(BF16)
        w2 = jnp.pad(gla_gk_w2[l], ((0, LANES - GLA_RANK), (0, 0))).astype(BF16)
        b2 = _row(gla_gk_b[l])
        o_gla, dn_gates = _mixer_call(
            functools.partial(_gla_kernel, "gla", 2), "gla", x3,
            [ng, w_gla, w2, b2, _row(jnp.tile(gla_norm_g[l], N_HEADS)), edge, level, lmask],
            _gla_scratch(batch, w_gla.shape[1], N_HEADS * GLA_DK), side_out=True)

        ab = jnp.zeros((2, LANES), F32)
        ab = ab.at[0, DN_LANE0 + N_HEADS:DN_LANE0 + 2 * N_HEADS].set(dn_A_log[l].astype(F32))
        ab = ab.at[1, DN_LANE0 + N_HEADS:DN_LANE0 + 2 * N_HEADS].set(dn_dt_bias[l].astype(F32))
        o_dn = _mixer_call(
            _dn_kernel, "deltanet", x3,
            [ng, wl[:, c4:c6].astype(BF16), dn_conv_w[l].astype(F32), ab, _row(jnp.tile(dn_norm_g[l], N_HEADS)), tri],
            _dn_scratch(batch), side_in=dn_gates)

        x2 = _merge_call(
            x3.reshape(batch * seq, d), batch, [ng, wl[:, c8:].astype(BF16), _row(gate_b[l])],
            [o.reshape(batch * seq, hw) for o in (o_hg, o_gla, o_dn)],
            [w_branch[l].astype(BF16), w_out[l].astype(BF16)])

        x2 = _ffn_call(
            x2, batch,
            [_row(norm_ffn_g[l]), w_up[l].astype(BF16), ffn_conv_w[l].astype(F32), _row(ffn_conv_b[l]),
             w_down[l].astype(BF16), _row(norm_final_g)],
            final_norm=(l == depth - 1))
        x3 = x2.reshape(batch, seq, d)
    return x3
```

```python
import functools

import jax
import jax.numpy as jnp
import numpy as np
from jax import lax
from jax.experimental import pallas as pl
from jax.experimental.pallas import tpu as pltpu

F32 = jnp.float32
BF16 = jnp.bfloat16

D_MODEL = 1024
CHUNK = 64
NORM_EPS = 1e-6
N_HEADS = 4
HEAD_DIM = 128
BRANCH_WIDTH = N_HEADS * HEAD_DIM
GLA_DK = 64
GLA_RANK = 16
GLA_GATE_NORMALIZER = 16.0
DN_LANE0 = GLA_RANK
DN_CONV = 4
N_BRANCH = 3
FFN_HIDDEN = 2816
FFN_CONV = 3
LANES = 128
CONV_PAD = 8
N_LEVELS = 6
LOG2E = 1.4426950408889634
SAFE_LOG2_DECAY = -160.0

TILE_MERGE = 512
TILE_FFN = 512
VMEM_LIMIT = 56 * 1024 * 1024


def _sigmoid(x):
    return 1.0 / (1.0 + jnp.exp(-x))


def _silu(x):
    return x * _sigmoid(x)


def _softplus(x):
    return jnp.maximum(x, 0.0) + jnp.log1p(jnp.exp(-jnp.abs(x)))


def _log_sigmoid(x):
    return -_softplus(-x)


def _rms_norm(x, g):
    y = x * lax.rsqrt(jnp.mean(x * x, axis=-1, keepdims=True) + NORM_EPS)
    return y * g


def _dot(a, b):
    return jnp.dot(a, b, preferred_element_type=F32)


def _bdot(a, b):
    return lax.dot_general(a, b, (((2,), (1,)), ((0,), (0,))), preferred_element_type=F32)


def _bdot_nt(a, b):
    return lax.dot_general(a, b, (((2,), (2,)), ((0,), (0,))), preferred_element_type=F32)


def _bdot_tn(a, b):
    return lax.dot_general(a, b, (((1,), (1,)), ((0,), (0,))), preferred_element_type=F32)


def _split_bf16(x):
    hi = x.astype(BF16)
    lo = (x - hi.astype(F32)).astype(BF16)
    return hi, lo


def _head(h):
    return slice(h * HEAD_DIM, (h + 1) * HEAD_DIM)


def _heads_to_problems(x3):
    return jnp.concatenate([x3[:, :, _head(h)] for h in range(N_HEADS)], axis=0)


def _to_problems(x2d, nb):
    return _heads_to_problems(x2d.reshape(nb, x2d.shape[0] // nb, x2d.shape[1]))


def _key_problems(x, nb, query=False):
    x3 = x if x.ndim == 3 else x.reshape(nb, x.shape[0] // nb, x.shape[1])
    if x3.shape[-1] == BRANCH_WIDTH:
        return _heads_to_problems(x3)
    lane = lax.broadcasted_iota(jnp.int32, (1, 1, LANES), 2)
    parts = []
    for h in range(N_HEADS):
        xp = x3[:, :, (h // 2) * LANES:(h // 2 + 1) * LANES]
        if query:
            own = (lane >= LANES // 2) if h % 2 else (lane < LANES // 2)
            xp = jnp.where(own, xp, jnp.zeros_like(xp))
        parts.append(xp)
    return jnp.concatenate(parts, axis=0)


def _from_problems(xp, nb):
    rows = nb * xp.shape[1]
    return jnp.concatenate([xp[h * nb:(h + 1) * nb].reshape(rows, HEAD_DIM) for h in range(N_HEADS)], axis=-1)


def _run_interleaved(*stage_generators):
    live = list(stage_generators)
    while live:
        live = [s for s in live if next(s, True) is None]


def _causal_masks():
    row = lax.broadcasted_iota(jnp.int32, (CHUNK, CHUNK), 0)
    col = lax.broadcasted_iota(jnp.int32, (CHUNK, CHUNK), 1)
    return row, col, (row >= col)[None], (row > col)[None]


def _edge_table():
    tab = np.tril(np.ones((CHUNK, CHUNK), np.float32))
    return np.concatenate([tab, tab], axis=1)


def _level_table():
    t = np.arange(CHUNK)
    u = np.arange(CHUNK)
    rows = []
    for lvl in range(N_LEVELS):
        p = CHUNK >> (lvl + 1)
        mid = (t // (2 * p)) * (2 * p) + p - 1
        upper = t > mid
        rows.append(np.where(upper[:, None], (u[None, :] > mid[:, None]) & (u[None, :] <= t[:, None]),
                             (u[None, :] > t[:, None]) & (u[None, :] <= mid[:, None])))
    tab = np.concatenate(rows, axis=0).astype(np.float32)
    return np.concatenate([tab, tab], axis=1)


def _level_masks():
    t = np.arange(CHUNK)[:, None]
    s = np.arange(CHUNK)[None, :]
    out = []
    for lvl in range(N_LEVELS):
        p = CHUNK >> (lvl + 1)
        out.append((t // (2 * p) == s // (2 * p)) & (t % (2 * p) >= p) & (s % (2 * p) < p))
    out.append(t == s)
    return np.stack(out).astype(np.float32)


def _gated_head_norm(o, zg, g):
    outs = []
    for h in range(N_HEADS):
        oh = o[:, _head(h)]
        inv = lax.rsqrt(jnp.mean(oh * oh, axis=-1, keepdims=True) + NORM_EPS)
        outs.append(oh * inv)
    return jnp.concatenate(outs, axis=-1) * g * zg


def _zero(*refs):
    for r in refs:
        r[...] = jnp.zeros(r.shape, r.dtype)


def _gla_front(kind, x_ref, ng_ref, w_ref, aux_refs, edge_ref, slot, q_s, k_s, v_s, zg_s, gs_s, tab_s, safe_s, p_s):
    nb, _, d = x_ref.shape
    rows = nb * CHUNK
    w = BRANCH_WIDTH
    h = _rms_norm(x_ref[...].reshape(rows, d), ng_ref[...]).astype(BF16)
    yield

    def project(c0, c1):
        p_s[:, c0:c1] = _dot(h, w_ref[:, c0:c1])

    def store_decay(g):
        g_hi, g_lo = _split_bf16(g * LOG2E)
        for b in range(nb):
            r = slice(b * CHUNK, (b + 1) * CHUNK)
            gs_s[slot, b] = jnp.concatenate([g_hi[r], g_lo[r]], axis=0)

    if kind == "hgrn2":
        lb_ref, = aux_refs
        for c0 in range(0, 4 * w, w):
            project(c0, c0 + w)
            yield
        q_s[slot] = (_silu(p_s[:, 0:w]) * (HEAD_DIM ** -0.5)).astype(BF16)
        yield
        z = p_s[:, w:2 * w]
        lb = lb_ref[0:1, :]
        log_lb = lb_ref[1:2, :]
        c = lb_ref[2:3, :] + _log_sigmoid(z)
        store_decay(jnp.maximum(log_lb, c) + jnp.log1p(jnp.exp(-jnp.abs(log_lb - c))))
        k_s[slot] = ((1.0 - lb) * _sigmoid(-z)).astype(BF16)
        yield
        v_s[slot] = p_s[:, 2 * w:3 * w].astype(BF16)
        zg_s[slot] = _silu(p_s[:, 3 * w:4 * w])
        yield
    else:
        w2_ref, b2_ref, side_ref = aux_refs
        kw = q_s.shape[-1]
        low = 2 * kw + 2 * w
        project(low, low + LANES)
        side_ref[...] = p_s[:, low:low + LANES].reshape(side_ref.shape)
        yield
        project(0, 2 * kw)
        yield
        gk = _dot(p_s[:, low:low + LANES].astype(BF16), w2_ref[...]) + b2_ref[...]
        store_decay(_log_sigmoid(gk) * (1.0 / GLA_GATE_NORMALIZER))
        yield
        project(2 * kw, 2 * kw + w)
        yield
    edge = edge_ref[...]
    low_mark = None
    for b in range(nb):
        tab = _dot(edge, gs_s[slot, b])
        tab_s[slot, b] = tab
        total = tab[CHUNK - 1:CHUNK, :]
        low_mark = total if low_mark is None else jnp.minimum(low_mark, total)
        if b % 4 == 3:
            yield
    safe_s[slot] = (jnp.min(low_mark) >= SAFE_LOG2_DECAY).astype(jnp.int32)
    if kind != "hgrn2":
        project(2 * kw + w, 2 * kw + 2 * w)
        yield
        q_s[slot] = (p_s[:, 0:kw] * (GLA_DK ** -0.5)).astype(BF16)
        k_s[slot] = p_s[:, kw:2 * kw].astype(BF16)
        yield
        v_s[slot] = p_s[:, 2 * kw:2 * kw + w].astype(BF16)
        zg_s[slot] = _silu(p_s[:, 2 * kw + w:2 * kw + 2 * w])


def _gla_scores_centered(q, k, tab, nb):
    _, _, incl, _ = _causal_masks()
    pre = tab
    mid = 0.5 * pre[:, CHUNK - 1:CHUNK, :]
    rows = nb * CHUNK
    qf = q * jnp.exp2(pre - mid).astype(BF16).reshape(rows, -1)
    kf = k * jnp.exp2(mid - pre).astype(BF16).reshape(rows, -1)
    yield
    s = _bdot_nt(_key_problems(qf, nb, query=True), _key_problems(kf, nb))
    yield jnp.where(incl, s, 0.0)


def _gla_scores_levels(q, k, gs, level_ref, lmask_ref, nb):
    sc = _bdot_nt(_key_problems(q, nb, query=True), _key_problems(k, nb)) * lmask_ref[N_LEVELS][None]
    for l in range(N_LEVELS):
        wt = level_ref[l * CHUNK:(l + 1) * CHUNK, :]
        e = jnp.exp2(jnp.concatenate([_dot(wt, gs[b]) for b in range(nb)], axis=0)).astype(BF16)
        sc = sc + _bdot_nt(_key_problems(q * e, nb, query=True), _key_problems(k * e, nb)) * lmask_ref[l][None]
    return sc


def _gla_back(centered, slot, q_s, k_s, v_s, zg_s, gs_s, tab_s, st_ref, hn_ref, level_ref, lmask_ref, o_ref):
    nb = o_ref.shape[0]
    rows = nb * CHUNK
    q = q_s[slot]
    k = k_s[slot]
    tab = tab_s[slot]
    if centered:
        scores = _gla_scores_centered(q, k, tab, nb)
        yield next(scores)
        sc = next(scores)
    else:
        sc = _gla_scores_levels(q, k, [gs_s[slot, b] for b in range(nb)], level_ref, lmask_ref, nb)
    yield
    pre = tab
    e_in = jnp.exp2(pre).astype(BF16).reshape(rows, -1)
    e_out = jnp.exp2(pre[:, CHUNK - 1:CHUNK, :] - pre).astype(BF16).reshape(rows, -1)
    e_last = _key_problems(jnp.exp2(pre[:, CHUNK - 1:CHUNK, :]), nb)
    v_p = _to_problems(v_s[slot], nb)
    st = st_ref[...]
    o = _bdot(sc.astype(BF16), v_p) + _bdot(_key_problems(q * e_in, nb, query=True), st.astype(BF16))
    yield
    decay = jnp.swapaxes(jnp.broadcast_to(e_last, st.shape), 1, 2)
    st_ref[...] = st * decay + _bdot_tn(_key_problems(k * e_out, nb), v_p)
    yield
    y = _gated_head_norm(_from_problems(o, nb), zg_s[slot], hn_ref[...])
    o_ref[...] = y.reshape(o_ref.shape).astype(o_ref.dtype)


def _gla_kernel(kind, n_aux, x_ref, ng_ref, w_ref, *rest):
    aux_refs, rest = rest[:n_aux], rest[n_aux:]
    (hn_ref, edge_ref, level_ref, lmask_ref, o_ref), rest = rest[:5], rest[5:]
    if kind != "hgrn2":
        aux_refs, rest = aux_refs + rest[:1], rest[1:]
    st_ref, q_s, k_s, v_s, zg_s, gs_s, tab_s, safe_s, p_s = rest
    j = pl.program_id(0)
    slot = j % 2
    prev = 1 - slot
    bufs = (q_s, k_s, v_s, zg_s, gs_s, tab_s)

    @pl.when(j == 0)
    def _():
        _zero(st_ref, *bufs)
        safe_s[0] = 1
        safe_s[1] = 1

    def front():
        return _gla_front(kind, x_ref, ng_ref, w_ref, aux_refs, edge_ref, slot, *bufs, safe_s, p_s)

    def back(centered):
        return _gla_back(centered, prev, *bufs, st_ref, hn_ref, level_ref, lmask_ref, o_ref)

    @pl.when(safe_s[prev] == 1)
    def _():
        _run_interleaved(front(), back(True))

    @pl.when(safe_s[prev] != 1)
    def _():
        _run_interleaved(back(False))
        _run_interleaved(front())


def _dn_front(x_ref, side_ref, ng_ref, w_ref, cw_ref, ab_ref, slot, hist_s, ext_s, q_s, k_s, v_s, zg_s, bg_s):
    nb, _, d = x_ref.shape
    rows = nb * CHUNK
    w = BRANCH_WIDTH
    h = _rms_norm(x_ref[...].reshape(rows, d), ng_ref[...]).astype(BF16)
    yield
    for part, dst in enumerate((q_s, k_s, v_s)):
        cols = slice(part * w, (part + 1) * w)
        ext_s[:, 0:CONV_PAD, cols] = hist_s[:, :, cols]
        ext_s[:, CONV_PAD:CONV_PAD + CHUNK, cols] = _dot(h, w_ref[:, cols]).reshape(nb, CHUNK, w)
        hist_s[:, :, cols] = ext_s[:, CHUNK:CHUNK + CONV_PAD, cols]
        yield
        ext = ext_s[:, :, cols]
        conv = cw_ref[DN_CONV - 1:DN_CONV, cols][None] * ext[:, CONV_PAD:CONV_PAD + CHUNK]
        for back in range(1, DN_CONV):
            tap = DN_CONV - 1 - back
            shifted = pltpu.roll(ext, back, axis=1)
            conv = conv + cw_ref[tap:tap + 1, cols][None] * shifted[:, CONV_PAD:CONV_PAD + CHUNK]
        act = _silu(conv.reshape(rows, w))
        if part < 2:
            scale = HEAD_DIM ** -0.5 if part == 0 else 1.0
            heads = []
            for hd in range(N_HEADS):
                xh = act[:, _head(hd)]
                heads.append(xh * lax.rsqrt(jnp.sum(xh * xh, axis=-1, keepdims=True) + NORM_EPS) * scale)
            act = jnp.concatenate(heads, axis=-1)
        dst[slot] = _to_problems(act, nb)
        yield
    zg_s[slot] = _silu(_dot(h, w_ref[:, 3 * w:4 * w]))
    yield
    ba = side_ref[...].reshape(rows, LANES)
    bg_s[slot, 0] = _sigmoid(ba)
    bg_s[slot, 1] = -jnp.exp(ab_ref[0:1, :]) * _softplus(ba + ab_ref[1:2, :])


def _dn_back(slot, q_s, k_s, v_s, zg_s, bg_s, st_ref, hn_ref, tri_ref, o_ref):
    nb = o_ref.shape[0]
    q_p = q_s[slot]
    k_p = k_s[slot]
    v_p = v_s[slot]
    beta = bg_s[slot, 0]
    g_hi, g_lo = _split_bf16(bg_s[slot, 1])
    tri = tri_ref[...]
    gcols, grows = [], []
    for b in range(nb):
        r = slice(b * CHUNK, (b + 1) * CHUNK)
        gcum = _dot(tri, jnp.concatenate([g_hi[r], g_lo[r]], axis=0))
        gcols.append(gcum)
        grows.append(gcum.T)
    ln = lambda h: slice(DN_LANE0 + N_HEADS + h, DN_LANE0 + N_HEADS + h + 1)
    gc = jnp.concatenate([gcols[b][None, :, ln(h)] for h in range(N_HEADS) for b in range(nb)], axis=0)
    gr = jnp.concatenate([grows[b][None, ln(h), :] for h in range(N_HEADS) for b in range(nb)], axis=0)
    beta3 = beta.reshape(nb, CHUNK, LANES)
    bc = jnp.concatenate([beta3[:, :, DN_LANE0 + h:DN_LANE0 + h + 1] for h in range(N_HEADS)], axis=0)

    row, col, incl, strict = _causal_masks()
    decay = jnp.exp(jnp.where(incl, gc - gr, -jnp.inf))
    g_last = gc[:, CHUNK - 1:CHUNK, :]
    e_in = jnp.exp(gc)
    e_out = jnp.exp(g_last - gc)
    e_last = jnp.exp(g_last)

    kb = k_p * bc
    kq = _bdot_nt(jnp.concatenate([kb, q_p], axis=1).astype(BF16), k_p.astype(BF16))
    yield
    m = jnp.where(strict, kq[:, 0:CHUNK] * decay, 0.0)
    sc = jnp.where(incl, kq[:, CHUNK:2 * CHUNK] * decay, 0.0).astype(BF16)
    eye = (row == col).astype(F32)[None]
    t = None
    blk = 1
    while blk < CHUNK:
        sel = ((row // (2 * blk) == col // (2 * blk)) & (row % (2 * blk) >= blk) & (col % (2 * blk) < blk))[None]
        off = jnp.where(sel, m, 0.0)
        if blk == 1:
            t = eye - off
        else:
            t_bf = t.astype(BF16)
            half = _bdot(t_bf, off.astype(BF16)).astype(BF16)
            yield
            t = t - _bdot(half, t_bf)
            yield
        blk *= 2
    rhs = jnp.concatenate([v_p * bc, kb * e_in], axis=-1).astype(BF16)
    uw = _bdot(t.astype(BF16), rhs).astype(BF16)
    yield
    sc_uw = _bdot(sc, uw)
    k_uw = _bdot_tn((k_p * e_out).astype(BF16), uw)
    yield
    q_eff = (q_p * e_in - sc_uw[:, :, HEAD_DIM:]).astype(BF16)
    st = st_ref[...]
    st_bf = st.astype(BF16)
    o = _bdot(q_eff, st_bf) + sc_uw[:, :, 0:HEAD_DIM]
    st_ref[...] = st * e_last + k_uw[:, :, 0:HEAD_DIM] - _bdot(k_uw[:, :, HEAD_DIM:].astype(BF16), st_bf)
    yield
    y = _gated_head_norm(_from_problems(o, nb), zg_s[slot], hn_ref[...])
    o_ref[...] = y.reshape(o_ref.shape).astype(o_ref.dtype)


def _dn_kernel(x_ref, side_ref, ng_ref, w_ref, cw_ref, ab_ref, hn_ref, tri_ref, o_ref,
               st_ref, hist_s, ext_s, q_s, k_s, v_s, zg_s, bg_s):
    j = pl.program_id(0)
    slot = j % 2
    bufs = (q_s, k_s, v_s, zg_s, bg_s)

    @pl.when(j == 0)
    def _():
        _zero(st_ref, hist_s, *bufs)

    _run_interleaved(
        _dn_front(x_ref, side_ref, ng_ref, w_ref, cw_ref, ab_ref, slot, hist_s, ext_s, *bufs),
        _dn_back(1 - slot, *bufs, st_ref, hn_ref, tri_ref, o_ref))


def _merge_kernel(x_ref, ng_ref, wg_ref, gb_ref, oa_ref, ob_ref, oc_ref, wb_ref, wo_ref, y_ref):
    x = x_ref[...]
    h = _rms_norm(x, ng_ref[...]).astype(BF16)
    gates = _sigmoid(_dot(h, wg_ref[...]) + gb_ref[...])
    merged = None
    for n, o_n in enumerate((oa_ref, ob_ref, oc_ref)):
        term = gates[:, n * D_MODEL:(n + 1) * D_MODEL] * _dot(o_n[...], wb_ref[n])
        merged = term if merged is None else merged + term
    y_ref[...] = x + _dot(merged.astype(BF16), wo_ref[...])


def _ffn_kernel(x_ref, ng_ref, wu_ref, cw_ref, cb_ref, wd_ref, fg_ref, y_ref, ext_s, hist_s, act_s,
                *, final_norm):
    @pl.when(pl.program_id(1) == 0)
    def _():
        hist_s[...] = jnp.zeros(hist_s.shape, hist_s.dtype)

    tile = x_ref.shape[0]
    x = x_ref[...]
    h = _rms_norm(x, ng_ref[...]).astype(BF16)
    ext_s[0:CONV_PAD, :] = hist_s[...]
    ext_s[CONV_PAD:CONV_PAD + tile, :] = _dot(h, wu_ref[...])
    hist_s[...] = ext_s[tile:tile + CONV_PAD, :]
    blk = 2 * LANES
    split = (FFN_HIDDEN // blk // 2 + 1) * blk
    y = x
    for j0 in range(0, FFN_HIDDEN, blk):
        halves = []
        for base in (j0, FFN_HIDDEN + j0):
            cols = slice(base, base + blk)
            u = cb_ref[0:1, cols] + cw_ref[0:1, cols] * ext_s[CONV_PAD - 2:CONV_PAD - 2 + tile, cols]
            for j in range(1, FFN_CONV):
                u = u + cw_ref[j:j + 1, cols] * ext_s[CONV_PAD - 2 + j:CONV_PAD - 2 + j + tile, cols]
            halves.append(u)
        act_s[:, j0:j0 + blk] = (_silu(halves[0]) * halves[1]).astype(BF16)
        if j0 + blk == split:
            y = y + _dot(act_s[:, 0:split], wd_ref[0:split, :])
    y = y + _dot(act_s[:, split:], wd_ref[split:, :])
    if final_norm:
        y = _rms_norm(y, fg_ref[...])
    y_ref[...] = y


def _const_spec(shape, grid_rank):
    nd = len(shape)
    if grid_rank == 1:
        return pl.BlockSpec(shape, lambda i, _nd=nd: (0,) * _nd, pipeline_mode=pl.Buffered(1))
    return pl.BlockSpec(shape, lambda b, i, _nd=nd: (0,) * _nd, pipeline_mode=pl.Buffered(1))


def _row_spec(tile, width, steps):
    return pl.BlockSpec((tile, width), lambda b, i, _s=steps: (b * _s + i, 0))


def _mixer_call(body, name, x3, consts, scratch, side_in=None, side_out=False):
    batch, seq, _ = x3.shape
    n = seq // CHUNK
    front = lambda j: (0, jnp.minimum(j, n - 1), 0)
    back = lambda j: (0, jnp.maximum(j - 1, 0), 0)
    side_spec = pl.BlockSpec((batch, CHUNK, LANES), front)
    streamed = [x3] + ([side_in] if side_in is not None else [])
    in_specs = ([pl.BlockSpec((batch, CHUNK, D_MODEL), front)] + [side_spec] * (len(streamed) - 1)
                + [_const_spec(c.shape, 1) for c in consts])
    out_specs = [pl.BlockSpec((batch, CHUNK, BRANCH_WIDTH), back)]
    out_shape = [jax.ShapeDtypeStruct((batch, seq, BRANCH_WIDTH), BF16)]
    if side_out:
        out_specs.append(side_spec)
        out_shape.append(jax.ShapeDtypeStruct((batch, seq, LANES), F32))
    outs = pl.pallas_call(
        body,
        grid=(n + 1,),
        in_specs=in_specs,
        out_specs=out_specs,
        out_shape=out_shape,
        scratch_shapes=scratch,
        compiler_params=pltpu.CompilerParams(dimension_semantics=("arbitrary",), vmem_limit_bytes=VMEM_LIMIT),
        name=name,
    )(*streamed, *consts)
    return outs if side_out else outs[0]


def _gla_scratch(batch, proj_width, key_width):
    rows = batch * CHUNK
    w = BRANCH_WIDTH
    return [pltpu.VMEM((N_HEADS * batch, HEAD_DIM, HEAD_DIM), F32),
            pltpu.VMEM((2, rows, key_width), BF16), pltpu.VMEM((2, rows, key_width), BF16),
            pltpu.VMEM((2, rows, w), BF16),
            pltpu.VMEM((2, rows, w), F32),
            pltpu.VMEM((2, batch, 2 * CHUNK, key_width), BF16),
            pltpu.VMEM((2, batch, CHUNK, key_width), F32),
            pltpu.SMEM((2,), jnp.int32),
            pltpu.VMEM((rows, proj_width), F32)]


def _dn_scratch(batch):
    rows = batch * CHUNK
    prob = (2, N_HEADS * batch, CHUNK, HEAD_DIM)
    return [pltpu.VMEM((N_HEADS * batch, HEAD_DIM, HEAD_DIM), F32),
            pltpu.VMEM((batch, CONV_PAD, 3 * BRANCH_WIDTH), F32),
            pltpu.VMEM((batch, CHUNK + CONV_PAD, 3 * BRANCH_WIDTH), F32),
            pltpu.VMEM(prob, F32), pltpu.VMEM(prob, F32), pltpu.VMEM(prob, F32),
            pltpu.VMEM((2, rows, BRANCH_WIDTH), F32),
            pltpu.VMEM((2, 2, rows, LANES), F32)]


def _params2():
    return pltpu.CompilerParams(dimension_semantics=("arbitrary", "arbitrary"), vmem_limit_bytes=VMEM_LIMIT)


def _merge_call(x2, batch, consts_a, branches, consts_b):
    rows = x2.shape[0]
    steps = rows // batch // TILE_MERGE
    in_specs = ([_row_spec(TILE_MERGE, D_MODEL, steps)] + [_const_spec(c.shape, 2) for c in consts_a]
                + [_row_spec(TILE_MERGE, BRANCH_WIDTH, steps) for _ in branches]
                + [_const_spec(c.shape, 2) for c in consts_b])
    return pl.pallas_call(
        _merge_kernel,
        grid=(batch, steps),
        in_specs=in_specs,
        out_specs=_row_spec(TILE_MERGE, D_MODEL, steps),
        out_shape=jax.ShapeDtypeStruct((rows, D_MODEL), F32),
        compiler_params=_params2(),
        name="merge",
    )(x2, *consts_a, *branches, *consts_b)


def _ffn_call(x2, batch, consts, final_norm):
    rows = x2.shape[0]
    steps = rows // batch // TILE_FFN
    in_specs = [_row_spec(TILE_FFN, D_MODEL, steps)] + [_const_spec(c.shape, 2) for c in consts]
    return pl.pallas_call(
        functools.partial(_ffn_kernel, final_norm=final_norm),
        grid=(batch, steps),
        in_specs=in_specs,
        out_specs=_row_spec(TILE_FFN, D_MODEL, steps),
        out_shape=jax.ShapeDtypeStruct((rows, D_MODEL), F32),
        scratch_shapes=[pltpu.VMEM((TILE_FFN + CONV_PAD, 2 * FFN_HIDDEN), F32),
                        pltpu.VMEM((CONV_PAD, 2 * FFN_HIDDEN), F32),
                        pltpu.VMEM((TILE_FFN, FFN_HIDDEN), BF16)],
        compiler_params=_params2(),
        name="ffn_final" if final_norm else "ffn",
    )(x2, *consts)


def _pad_cols(wcols, width):
    return jnp.pad(wcols, [(0, 0)] * (wcols.ndim - 1) + [(0, width - wcols.shape[-1])])


def _row(v):
    return v.reshape(1, -1).astype(F32)


def kernel(x, norm_mix_g, w_in, hg_lower_bounds, hg_norm_g, gla_gk_w2, gla_gk_b, gla_norm_g, dn_conv_w,
           dn_A_log, dn_dt_bias, dn_norm_g, w_branch, gate_b, w_out, norm_ffn_g, w_up, ffn_conv_w,
           ffn_conv_b, w_down, norm_final_g):
    batch, seq, d = x.shape
    depth = w_in.shape[0]
    assert d == D_MODEL and seq % max(CHUNK, TILE_MERGE, TILE_FFN) == 0

    lb_all = jnp.cumsum(jax.nn.softmax(hg_lower_bounds.astype(F32), axis=0), axis=0)
    lb_all = lb_all - lb_all[:1]
    edge = jnp.asarray(_edge_table(), BF16)
    level = jnp.asarray(_level_table(), BF16)
    lmask = jnp.asarray(_level_masks(), F32)
    tri1 = np.tril(np.ones((CHUNK, CHUNK), np.float32))
    tri = jnp.asarray(np.concatenate([tri1, tri1], axis=1), BF16)

    hw = BRANCH_WIDTH
    c0 = 4 * hw
    c1 = c0 + 2 * N_HEADS * GLA_DK
    c2 = c1 + hw
    c3 = c2 + GLA_RANK
    c4 = c3 + hw
    c5 = c4 + 3 * hw
    c6 = c5 + hw
    c7 = c6 + N_HEADS
    c8 = c7 + N_HEADS

    x3 = x
    for l in range(depth):
        wl = w_in[l]
        ng = _row(norm_mix_g[l])
        lb = lb_all[l]
        lb_rows = jnp.stack([lb, jnp.log(lb), jnp.log1p(-lb)]).astype(F32)
        o_hg = _mixer_call(
            functools.partial(_gla_kernel, "hgrn2", 1), "hgrn2", x3,
            [ng, wl[:, :c0].astype(BF16), lb_rows, _row(jnp.tile(hg_norm_g[l], N_HEADS)), edge, level, lmask],
            _gla_scratch(batch, c0, hw))

        gate_cols = _pad_cols(jnp.concatenate([wl[:, c2:c3], wl[:, c6:c8]], axis=1), LANES)
        w_gla = jnp.concatenate([wl[:, c0:c2], wl[:, c3:c4], gate_cols], axis=1).astype(BF16)
        w2 = jnp.pad(gla_gk_w2[l], ((0, LANES - GLA_RANK), (0, 0))).astype(BF16)
        b2 = _row(gla_gk_b[l])
        o_gla, dn_gates = _mixer_call(
            functools.partial(_gla_kernel, "gla", 2), "gla", x3,
            [ng, w_gla, w2, b2, _row(jnp.tile(gla_norm_g[l], N_HEADS)), edge, level, lmask],
            _gla_scratch(batch, w_gla.shape[1], N_HEADS * GLA_DK), side_out=True)

        ab = jnp.zeros((2, LANES), F32)
        ab = ab.at[0, DN_LANE0 + N_HEADS:DN_LANE0 + 2 * N_HEADS].set(dn_A_log[l].astype(F32))
        ab = ab.at[1, DN_LANE0 + N_HEADS:DN_LANE0 + 2 * N_HEADS].set(dn_dt_bias[l].astype(F32))
        o_dn = _mixer_call(
            _dn_kernel, "deltanet", x3,
            [ng, wl[:, c4:c6].astype(BF16), dn_conv_w[l].astype(F32), ab, _row(jnp.tile(dn_norm_g[l], N_HEADS)), tri],
            _dn_scratch(batch), side_in=dn_gates)

        x2 = _merge_call(
            x3.reshape(batch * seq, d), batch, [ng, wl[:, c8:].astype(BF16), _row(gate_b[l])],
            [o.reshape(batch * seq, hw) for o in (o_hg, o_gla, o_dn)],
            [w_branch[l].astype(BF16), w_out[l].astype(BF16)])

        x2 = _ffn_call(
            x2, batch,
            [_row(norm_ffn_g[l]), w_up[l].astype(BF16), ffn_conv_w[l].astype(F32), _row(ffn_conv_b[l]),
             w_down[l].astype(BF16), _row(norm_final_g)],
            final_norm=(l == depth - 1))
        x3 = x2.reshape(batch, seq, d)
    return x3
```

```python
import functools

import jax
import jax.numpy as jnp
import numpy as np
from jax import lax
from jax.experimental import pallas as pl
from jax.experimental.pallas import tpu as pltpu

F32 = jnp.float32
BF16 = jnp.bfloat16

D_MODEL = 1024
CHUNK = 64
NORM_EPS = 1e-6
N_HEADS = 4
HEAD_DIM = 128
BRANCH_WIDTH = N_HEADS * HEAD_DIM
GLA_DK = 64
GLA_RANK = 16
GLA_GATE_NORMALIZER = 16.0
DN_LANE0 = GLA_RANK
DN_CONV = 4
N_BRANCH = 3
FFN_HIDDEN = 2816
FFN_CONV = 3
LANES = 128
CONV_PAD = 8
N_LEVELS = 6
LOG2E = 1.4426950408889634
SAFE_LOG2_DECAY = -160.0

TILE_MERGE = 1024
TILE_FFN = 512
VMEM_LIMIT = 56 * 1024 * 1024


def _sigmoid(x):
    return 1.0 / (1.0 + jnp.exp(-x))


def _silu(x):
    return x * _sigmoid(x)


def _softplus(x):
    return jnp.maximum(x, 0.0) + jnp.log1p(jnp.exp(-jnp.abs(x)))


def _log_sigmoid(x):
    return -_softplus(-x)


def _rms_norm(x, g):
    y = x * lax.rsqrt(jnp.mean(x * x, axis=-1, keepdims=True) + NORM_EPS)
    return y * g


def _dot(a, b):
    return jnp.dot(a, b, preferred_element_type=F32)


def _bdot(a, b):
    return lax.dot_general(a, b, (((2,), (1,)), ((0,), (0,))), preferred_element_type=F32)


def _bdot_nt(a, b):
    return lax.dot_general(a, b, (((2,), (2,)), ((0,), (0,))), preferred_element_type=F32)


def _bdot_tn(a, b):
    return lax.dot_general(a, b, (((1,), (1,)), ((0,), (0,))), preferred_element_type=F32)


def _split_bf16(x):
    hi = x.astype(BF16)
    lo = (x - hi.astype(F32)).astype(BF16)
    return hi, lo


def _head(h):
    return slice(h * HEAD_DIM, (h + 1) * HEAD_DIM)


def _heads_to_problems(x3):
    return jnp.concatenate([x3[:, :, _head(h)] for h in range(N_HEADS)], axis=0)


def _to_problems(x2d, nb):
    return _heads_to_problems(x2d.reshape(nb, x2d.shape[0] // nb, x2d.shape[1]))


def _key_problems(x, nb, query=False):
    x3 = x if x.ndim == 3 else x.reshape(nb, x.shape[0] // nb, x.shape[1])
    if x3.shape[-1] == BRANCH_WIDTH:
        return _heads_to_problems(x3)
    lane = lax.broadcasted_iota(jnp.int32, (1, 1, LANES), 2)
    parts = []
    for h in range(N_HEADS):
        xp = x3[:, :, (h // 2) * LANES:(h // 2 + 1) * LANES]
        if query:
            own = (lane >= LANES // 2) if h % 2 else (lane < LANES // 2)
            xp = jnp.where(own, xp, jnp.zeros_like(xp))
        parts.append(xp)
    return jnp.concatenate(parts, axis=0)


def _from_problems(xp, nb):
    rows = nb * xp.shape[1]
    return jnp.concatenate([xp[h * nb:(h + 1) * nb].reshape(rows, HEAD_DIM) for h in range(N_HEADS)], axis=-1)


def _run_interleaved(*stage_generators):
    live = list(stage_generators)
    while live:
        live = [s for s in live if next(s, True) is None]


def _causal_masks():
    row = lax.broadcasted_iota(jnp.int32, (CHUNK, CHUNK), 0)
    col = lax.broadcasted_iota(jnp.int32, (CHUNK, CHUNK), 1)
    return row, col, (row >= col)[None], (row > col)[None]


def _edge_table():
    tab = np.tril(np.ones((CHUNK, CHUNK), np.float32))
    return np.concatenate([tab, tab], axis=1)


def _level_table():
    t = np.arange(CHUNK)
    u = np.arange(CHUNK)
    rows = []
    for lvl in range(N_LEVELS):
        p = CHUNK >> (lvl + 1)
        mid = (t // (2 * p)) * (2 * p) + p - 1
        upper = t > mid
        rows.append(np.where(upper[:, None], (u[None, :] > mid[:, None]) & (u[None, :] <= t[:, None]),
                             (u[None, :] > t[:, None]) & (u[None, :] <= mid[:, None])))
    tab = np.concatenate(rows, axis=0).astype(np.float32)
    return np.concatenate([tab, tab], axis=1)


def _level_masks():
    t = np.arange(CHUNK)[:, None]
    s = np.arange(CHUNK)[None, :]
    out = []
    for lvl in range(N_LEVELS):
        p = CHUNK >> (lvl + 1)
        out.append((t // (2 * p) == s // (2 * p)) & (t % (2 * p) >= p) & (s % (2 * p) < p))
    out.append(t == s)
    return np.stack(out).astype(np.float32)


def _gated_head_norm(o, zg, g):
    outs = []
    for h in range(N_HEADS):
        oh = o[:, _head(h)]
        inv = lax.rsqrt(jnp.mean(oh * oh, axis=-1, keepdims=True) + NORM_EPS)
        outs.append(oh * inv)
    return jnp.concatenate(outs, axis=-1) * g * zg


def _zero(*refs):
    for r in refs:
        r[...] = jnp.zeros(r.shape, r.dtype)


def _gla_front(kind, x_ref, ng_ref, w_ref, aux_refs, edge_ref, slot, q_s, k_s, v_s, zg_s, gs_s, tab_s, safe_s, p_s):
    nb, _, d = x_ref.shape
    rows = nb * CHUNK
    w = BRANCH_WIDTH
    h = _rms_norm(x_ref[...].reshape(rows, d), ng_ref[...]).astype(BF16)
    yield

    def project(c0, c1):
        p_s[:, c0:c1] = _dot(h, w_ref[:, c0:c1])

    def store_decay(g):
        g_hi, g_lo = _split_bf16(g * LOG2E)
        for b in range(nb):
            r = slice(b * CHUNK, (b + 1) * CHUNK)
            gs_s[slot, b] = jnp.concatenate([g_hi[r], g_lo[r]], axis=0)

    if kind == "hgrn2":
        lb_ref, = aux_refs
        for c0 in range(0, 4 * w, w):
            project(c0, c0 + w)
            yield
        q_s[slot] = (_silu(p_s[:, 0:w]) * (HEAD_DIM ** -0.5)).astype(BF16)
        yield
        z = p_s[:, w:2 * w]
        lb = lb_ref[0:1, :]
        log_lb = lb_ref[1:2, :]
        c = lb_ref[2:3, :] + _log_sigmoid(z)
        store_decay(jnp.maximum(log_lb, c) + jnp.log1p(jnp.exp(-jnp.abs(log_lb - c))))
        k_s[slot] = ((1.0 - lb) * _sigmoid(-z)).astype(BF16)
        yield
        v_s[slot] = p_s[:, 2 * w:3 * w].astype(BF16)
        zg_s[slot] = _silu(p_s[:, 3 * w:4 * w])
        yield
    else:
        w2_ref, b2_ref, side_ref = aux_refs
        kw = q_s.shape[-1]
        low = 2 * kw + 2 * w
        project(low, low + LANES)
        side_ref[...] = p_s[:, low:low + LANES].reshape(side_ref.shape)
        yield
        project(0, 2 * kw)
        yield
        gk = _dot(p_s[:, low:low + LANES].astype(BF16), w2_ref[...]) + b2_ref[...]
        store_decay(_log_sigmoid(gk) * (1.0 / GLA_GATE_NORMALIZER))
        yield
        project(2 * kw, 2 * kw + w)
        yield
    edge = edge_ref[...]
    low_mark = None
    for b in range(nb):
        tab = _dot(edge, gs_s[slot, b])
        tab_s[slot, b] = tab
        total = tab[CHUNK - 1:CHUNK, :]
        low_mark = total if low_mark is None else jnp.minimum(low_mark, total)
        if b % 4 == 3:
            yield
    safe_s[slot] = (jnp.min(low_mark) >= SAFE_LOG2_DECAY).astype(jnp.int32)
    if kind != "hgrn2":
        project(2 * kw + w, 2 * kw + 2 * w)
        yield
        q_s[slot] = (p_s[:, 0:kw] * (GLA_DK ** -0.5)).astype(BF16)
        k_s[slot] = p_s[:, kw:2 * kw].astype(BF16)
        yield
        v_s[slot] = p_s[:, 2 * kw:2 * kw + w].astype(BF16)
        zg_s[slot] = _silu(p_s[:, 2 * kw + w:2 * kw + 2 * w])


def _gla_scores_centered(q, k, tab, nb):
    _, _, incl, _ = _causal_masks()
    pre = tab
    mid = 0.5 * pre[:, CHUNK - 1:CHUNK, :]
    rows = nb * CHUNK
    qf = q * jnp.exp2(pre - mid).astype(BF16).reshape(rows, -1)
    kf = k * jnp.exp2(mid - pre).astype(BF16).reshape(rows, -1)
    yield
    s = _bdot_nt(_key_problems(qf, nb, query=True), _key_problems(kf, nb))
    yield jnp.where(incl, s, 0.0)


def _gla_scores_levels(q, k, gs, level_ref, lmask_ref, nb):
    sc = _bdot_nt(_key_problems(q, nb, query=True), _key_problems(k, nb)) * lmask_ref[N_LEVELS][None]
    for l in range(N_LEVELS):
        wt = level_ref[l * CHUNK:(l + 1) * CHUNK, :]
        e = jnp.exp2(jnp.concatenate([_dot(wt, gs[b]) for b in range(nb)], axis=0)).astype(BF16)
        sc = sc + _bdot_nt(_key_problems(q * e, nb, query=True), _key_problems(k * e, nb)) * lmask_ref[l][None]
    return sc


def _gla_back(centered, slot, q_s, k_s, v_s, zg_s, gs_s, tab_s, st_ref, hn_ref, level_ref, lmask_ref, o_ref):
    nb = o_ref.shape[0]
    rows = nb * CHUNK
    q = q_s[slot]
    k = k_s[slot]
    tab = tab_s[slot]
    if centered:
        scores = _gla_scores_centered(q, k, tab, nb)
        yield next(scores)
        sc = next(scores)
    else:
        sc = _gla_scores_levels(q, k, [gs_s[slot, b] for b in range(nb)], level_ref, lmask_ref, nb)
    yield
    pre = tab
    e_in = jnp.exp2(pre).astype(BF16).reshape(rows, -1)
    e_out = jnp.exp2(pre[:, CHUNK - 1:CHUNK, :] - pre).astype(BF16).reshape(rows, -1)
    e_last = _key_problems(jnp.exp2(pre[:, CHUNK - 1:CHUNK, :]), nb)
    v_p = _to_problems(v_s[slot], nb)
    st = st_ref[...]
    o = _bdot(sc.astype(BF16), v_p) + _bdot(_key_problems(q * e_in, nb, query=True), st.astype(BF16))
    yield
    decay = jnp.swapaxes(jnp.broadcast_to(e_last, st.shape), 1, 2)
    st_ref[...] = st * decay + _bdot_tn(_key_problems(k * e_out, nb), v_p)
    yield
    y = _gated_head_norm(_from_problems(o, nb), zg_s[slot], hn_ref[...])
    o_ref[...] = y.reshape(o_ref.shape).astype(o_ref.dtype)


def _gla_kernel(kind, n_aux, x_ref, ng_ref, w_ref, *rest):
    aux_refs, rest = rest[:n_aux], rest[n_aux:]
    (hn_ref, edge_ref, level_ref, lmask_ref, o_ref), rest = rest[:5], rest[5:]
    if kind != "hgrn2":
        aux_refs, rest = aux_refs + rest[:1], rest[1:]
    st_ref, q_s, k_s, v_s, zg_s, gs_s, tab_s, safe_s, p_s = rest
    j = pl.program_id(0)
    slot = j % 2
    prev = 1 - slot
    bufs = (q_s, k_s, v_s, zg_s, gs_s, tab_s)

    @pl.when(j == 0)
    def _():
        _zero(st_ref, *bufs)
        safe_s[0] = 1
        safe_s[1] = 1

    def front():
        return _gla_front(kind, x_ref, ng_ref, w_ref, aux_refs, edge_ref, slot, *bufs, safe_s, p_s)

    def back(centered):
        return _gla_back(centered, prev, *bufs, st_ref, hn_ref, level_ref, lmask_ref, o_ref)

    @pl.when(safe_s[prev] == 1)
    def _():
        _run_interleaved(front(), back(True))

    @pl.when(safe_s[prev] != 1)
    def _():
        _run_interleaved(back(False))
        _run_interleaved(front())


def _dn_front(x_ref, side_ref, ng_ref, w_ref, cw_ref, ab_ref, slot, hist_s, ext_s, q_s, k_s, v_s, zg_s, bg_s):
    nb, _, d = x_ref.shape
    rows = nb * CHUNK
    w = BRANCH_WIDTH
    h = _rms_norm(x_ref[...].reshape(rows, d), ng_ref[...]).astype(BF16)
    yield
    for part, dst in enumerate((q_s, k_s, v_s)):
        cols = slice(part * w, (part + 1) * w)
        ext_s[:, 0:CONV_PAD, cols] = hist_s[:, :, cols]
        ext_s[:, CONV_PAD:CONV_PAD + CHUNK, cols] = _dot(h, w_ref[:, cols]).reshape(nb, CHUNK, w)
        hist_s[:, :, cols] = ext_s[:, CHUNK:CHUNK + CONV_PAD, cols]
        yield
        ext = ext_s[:, :, cols]
        conv = cw_ref[DN_CONV - 1:DN_CONV, cols][None] * ext[:, CONV_PAD:CONV_PAD + CHUNK]
        for back in range(1, DN_CONV):
            tap = DN_CONV - 1 - back
            shifted = pltpu.roll(ext, back, axis=1)
            conv = conv + cw_ref[tap:tap + 1, cols][None] * shifted[:, CONV_PAD:CONV_PAD + CHUNK]
        act = _silu(conv.reshape(rows, w))
        if part < 2:
            scale = HEAD_DIM ** -0.5 if part == 0 else 1.0
            heads = []
            for hd in range(N_HEADS):
                xh = act[:, _head(hd)]
                heads.append(xh * lax.rsqrt(jnp.sum(xh * xh, axis=-1, keepdims=True) + NORM_EPS) * scale)
            act = jnp.concatenate(heads, axis=-1)
        dst[slot] = _to_problems(act, nb)
        yield
    zg_s[slot] = _silu(_dot(h, w_ref[:, 3 * w:4 * w]))
    yield
    ba = side_ref[...].reshape(rows, LANES)
    bg_s[slot, 0] = _sigmoid(ba)
    bg_s[slot, 1] = -jnp.exp(ab_ref[0:1, :]) * _softplus(ba + ab_ref[1:2, :])


def _dn_back(slot, q_s, k_s, v_s, zg_s, bg_s, st_ref, hn_ref, tri_ref, o_ref):
    nb = o_ref.shape[0]
    q_p = q_s[slot]
    k_p = k_s[slot]
    v_p = v_s[slot]
    beta = bg_s[slot, 0]
    g_hi, g_lo = _split_bf16(bg_s[slot, 1])
    tri = tri_ref[...]
    gcols, grows = [], []
    for b in range(nb):
        r = slice(b * CHUNK, (b + 1) * CHUNK)
        gcum = _dot(tri, jnp.concatenate([g_hi[r], g_lo[r]], axis=0))
        gcols.append(gcum)
        grows.append(gcum.T)
    ln = lambda h: slice(DN_LANE0 + N_HEADS + h, DN_LANE0 + N_HEADS + h + 1)
    gc = jnp.concatenate([gcols[b][None, :, ln(h)] for h in range(N_HEADS) for b in range(nb)], axis=0)
    gr = jnp.concatenate([grows[b][None, ln(h), :] for h in range(N_HEADS) for b in range(nb)], axis=0)
    beta3 = beta.reshape(nb, CHUNK, LANES)
    bc = jnp.concatenate([beta3[:, :, DN_LANE0 + h:DN_LANE0 + h + 1] for h in range(N_HEADS)], axis=0)

    row, col, incl, strict = _causal_masks()
    decay = jnp.exp(jnp.where(incl, gc - gr, -jnp.inf))
    g_last = gc[:, CHUNK - 1:CHUNK, :]
    e_in = jnp.exp(gc)
    e_out = jnp.exp(g_last - gc)
    e_last = jnp.exp(g_last)

    kb = k_p * bc
    kq = _bdot_nt(jnp.concatenate([kb, q_p], axis=1).astype(BF16), k_p.astype(BF16))
    yield
    m = jnp.where(strict, kq[:, 0:CHUNK] * decay, 0.0)
    sc = jnp.where(incl, kq[:, CHUNK:2 * CHUNK] * decay, 0.0).astype(BF16)
    eye = (row == col).astype(F32)[None]
    t = None
    blk = 1
    while blk < CHUNK:
        sel = ((row // (2 * blk) == col // (2 * blk)) & (row % (2 * blk) >= blk) & (col % (2 * blk) < blk))[None]
        off = jnp.where(sel, m, 0.0)
        if blk == 1:
            t = eye - off
        else:
            t_bf = t.astype(BF16)
            half = _bdot(t_bf, off.astype(BF16)).astype(BF16)
            yield
            t = t - _bdot(half, t_bf)
            yield
        blk *= 2
    rhs = jnp.concatenate([v_p * bc, kb * e_in], axis=-1).astype(BF16)
    uw = _bdot(t.astype(BF16), rhs).astype(BF16)
    yield
    sc_uw = _bdot(sc, uw)
    k_uw = _bdot_tn((k_p * e_out).astype(BF16), uw)
    yield
    q_eff = (q_p * e_in - sc_uw[:, :, HEAD_DIM:]).astype(BF16)
    st = st_ref[...]
    st_bf = st.astype(BF16)
    o = _bdot(q_eff, st_bf) + sc_uw[:, :, 0:HEAD_DIM]
    st_ref[...] = st * e_last + k_uw[:, :, 0:HEAD_DIM] - _bdot(k_uw[:, :, HEAD_DIM:].astype(BF16), st_bf)
    yield
    y = _gated_head_norm(_from_problems(o, nb), zg_s[slot], hn_ref[...])
    o_ref[...] = y.reshape(o_ref.shape).astype(o_ref.dtype)


def _dn_kernel(x_ref, side_ref, ng_ref, w_ref, cw_ref, ab_ref, hn_ref, tri_ref, o_ref,
               st_ref, hist_s, ext_s, q_s, k_s, v_s, zg_s, bg_s):
    j = pl.program_id(0)
    slot = j % 2
    bufs = (q_s, k_s, v_s, zg_s, bg_s)

    @pl.when(j == 0)
    def _():
        _zero(st_ref, hist_s, *bufs)

    _run_interleaved(
        _dn_front(x_ref, side_ref, ng_ref, w_ref, cw_ref, ab_ref, slot, hist_s, ext_s, *bufs),
        _dn_back(1 - slot, *bufs, st_ref, hn_ref, tri_ref, o_ref))


def _merge_kernel(x_ref, ng_ref, wg_ref, gb_ref, oa_ref, ob_ref, oc_ref, wb_ref, wo_ref, y_ref):
    x = x_ref[...]
    h = _rms_norm(x, ng_ref[...]).astype(BF16)
    gates = _sigmoid(_dot(h, wg_ref[...]) + gb_ref[...])
    merged = None
    for n, o_n in enumerate((oa_ref, ob_ref, oc_ref)):
        term = gates[:, n * D_MODEL:(n + 1) * D_MODEL] * _dot(o_n[...], wb_ref[n])
        merged = term if merged is None else merged + term
    y_ref[...] = x + _dot(merged.astype(BF16), wo_ref[...])


def _ffn_kernel(x_ref, ng_ref, wu_ref, cw_ref, cb_ref, wd_ref, fg_ref, y_ref, ext_s, hist_s, act_s,
                *, final_norm):
    @pl.when(pl.program_id(1) == 0)
    def _():
        hist_s[...] = jnp.zeros(hist_s.shape, hist_s.dtype)

    tile = x_ref.shape[0]
    x = x_ref[...]
    h = _rms_norm(x, ng_ref[...]).astype(BF16)
    ext_s[0:CONV_PAD, :] = hist_s[...]
    ext_s[CONV_PAD:CONV_PAD + tile, :] = _dot(h, wu_ref[...])
    hist_s[...] = ext_s[tile:tile + CONV_PAD, :]
    blk = 2 * LANES
    split = (FFN_HIDDEN // blk // 2 + 1) * blk
    y = x
    for j0 in range(0, FFN_HIDDEN, blk):
        halves = []
        for base in (j0, FFN_HIDDEN + j0):
            cols = slice(base, base + blk)
            u = cb_ref[0:1, cols] + cw_ref[0:1, cols] * ext_s[CONV_PAD - 2:CONV_PAD - 2 + tile, cols]
            for j in range(1, FFN_CONV):
                u = u + cw_ref[j:j + 1, cols] * ext_s[CONV_PAD - 2 + j:CONV_PAD - 2 + j + tile, cols]
            halves.append(u)
        act_s[:, j0:j0 + blk] = (_silu(halves[0]) * halves[1]).astype(BF16)
        if j0 + blk == split:
            y = y + _dot(act_s[:, 0:split], wd_ref[0:split, :])
    y = y + _dot(act_s[:, split:], wd_ref[split:, :])
    if final_norm:
        y = _rms_norm(y, fg_ref[...])
    y_ref[...] = y


def _const_spec(shape, grid_rank):
    nd = len(shape)
    if grid_rank == 1:
        return pl.BlockSpec(shape, lambda i, _nd=nd: (0,) * _nd, pipeline_mode=pl.Buffered(1))
    return pl.BlockSpec(shape, lambda b, i, _nd=nd: (0,) * _nd, pipeline_mode=pl.Buffered(1))


def _row_spec(tile, width, steps):
    return pl.BlockSpec((tile, width), lambda b, i, _s=steps: (b * _s + i, 0))


def _mixer_call(body, name, x3, consts, scratch, side_in=None, side_out=False):
    batch, seq, _ = x3.shape
    n = seq // CHUNK
    front = lambda j: (0, jnp.minimum(j, n - 1), 0)
    back = lambda j: (0, jnp.maximum(j - 1, 0), 0)
    side_spec = pl.BlockSpec((batch, CHUNK, LANES), front)
    streamed = [x3] + ([side_in] if side_in is not None else [])
    in_specs = ([pl.BlockSpec((batch, CHUNK, D_MODEL), front)] + [side_spec] * (len(streamed) - 1)
                + [_const_spec(c.shape, 1) for c in consts])
    out_specs = [pl.BlockSpec((batch, CHUNK, BRANCH_WIDTH), back)]
    out_shape = [jax.ShapeDtypeStruct((batch, seq, BRANCH_WIDTH), BF16)]
    if side_out:
        out_specs.append(side_spec)
        out_shape.append(jax.ShapeDtypeStruct((batch, seq, LANES), F32))
    outs = pl.pallas_call(
        body,
        grid=(n + 1,),
        in_specs=in_specs,
        out_specs=out_specs,
        out_shape=out_shape,
        scratch_shapes=scratch,
        compiler_params=pltpu.CompilerParams(dimension_semantics=("arbitrary",), vmem_limit_bytes=VMEM_LIMIT),
        name=name,
    )(*streamed, *consts)
    return outs if side_out else outs[0]


def _gla_scratch(batch, proj_width, key_width):
    rows = batch * CHUNK
    w = BRANCH_WIDTH
    return [pltpu.VMEM((N_HEADS * batch, HEAD_DIM, HEAD_DIM), F32),
            pltpu.VMEM((2, rows, key_width), BF16), pltpu.VMEM((2, rows, key_width), BF16),
            pltpu.VMEM((2, rows, w), BF16),
            pltpu.VMEM((2, rows, w), F32),
            pltpu.VMEM((2, batch, 2 * CHUNK, key_width), BF16),
            pltpu.VMEM((2, batch, CHUNK, key_width), F32),
            pltpu.SMEM((2,), jnp.int32),
            pltpu.VMEM((rows, proj_width), F32)]


def _dn_scratch(batch):
    rows = batch * CHUNK
    prob = (2, N_HEADS * batch, CHUNK, HEAD_DIM)
    return [pltpu.VMEM((N_HEADS * batch, HEAD_DIM, HEAD_DIM), F32),
            pltpu.VMEM((batch, CONV_PAD, 3 * BRANCH_WIDTH), F32),
            pltpu.VMEM((batch, CHUNK + CONV_PAD, 3 * BRANCH_WIDTH), F32),
            pltpu.VMEM(prob, F32), pltpu.VMEM(prob, F32), pltpu.VMEM(prob, F32),
            pltpu.VMEM((2, rows, BRANCH_WIDTH), F32),
            pltpu.VMEM((2, 2, rows, LANES), F32)]


def _params2():
    return pltpu.CompilerParams(dimension_semantics=("arbitrary", "arbitrary"), vmem_limit_bytes=VMEM_LIMIT)


def _merge_call(x2, batch, consts_a, branches, consts_b):
    rows = x2.shape[0]
    steps = rows // batch // TILE_MERGE
    in_specs = ([_row_spec(TILE_MERGE, D_MODEL, steps)] + [_const_spec(c.shape, 2) for c in consts_a]
                + [_row_spec(TILE_MERGE, BRANCH_WIDTH, steps) for _ in branches]
                + [_const_spec(c.shape, 2) for c in consts_b])
    return pl.pallas_call(
        _merge_kernel,
        grid=(batch, steps),
        in_specs=in_specs,
        out_specs=_row_spec(TILE_MERGE, D_MODEL, steps),
        out_shape=jax.ShapeDtypeStruct((rows, D_MODEL), F32),
        compiler_params=_params2(),
        name="merge",
    )(x2, *consts_a, *branches, *consts_b)


def _ffn_call(x2, batch, consts, final_norm):
    rows = x2.shape[0]
    steps = rows // batch // TILE_FFN
    in_specs = [_row_spec(TILE_FFN, D_MODEL, steps)] + [_const_spec(c.shape, 2) for c in consts]
    return pl.pallas_call(
        functools.partial(_ffn_kernel, final_norm=final_norm),
        grid=(batch, steps),
        in_specs=in_specs,
        out_specs=_row_spec(TILE_FFN, D_MODEL, steps),
        out_shape=jax.ShapeDtypeStruct((rows, D_MODEL), F32),
        scratch_shapes=[pltpu.VMEM((TILE_FFN + CONV_PAD, 2 * FFN_HIDDEN), F32),
                        pltpu.VMEM((CONV_PAD, 2 * FFN_HIDDEN), F32),
                        pltpu.VMEM((TILE_FFN, FFN_HIDDEN), BF16)],
        compiler_params=_params2(),
        name="ffn_final" if final_norm else "ffn",
    )(x2, *consts)


def _pad_cols(wcols, width):
    return jnp.pad(wcols, [(0, 0)] * (wcols.ndim - 1) + [(0, width - wcols.shape[-1])])


def _row(v):
    return v.reshape(1, -1).astype(F32)


def kernel(x, norm_mix_g, w_in, hg_lower_bounds, hg_norm_g, gla_gk_w2, gla_gk_b, gla_norm_g, dn_conv_w,
           dn_A_log, dn_dt_bias, dn_norm_g, w_branch, gate_b, w_out, norm_ffn_g, w_up, ffn_conv_w,
           ffn_conv_b, w_down, norm_final_g):
    batch, seq, d = x.shape
    depth = w_in.shape[0]
    assert d == D_MODEL and seq % max(CHUNK, TILE_MERGE, TILE_FFN) == 0

    lb_all = jnp.cumsum(jax.nn.softmax(hg_lower_bounds.astype(F32), axis=0), axis=0)
    lb_all = lb_all - lb_all[:1]
    edge = jnp.asarray(_edge_table(), BF16)
    level = jnp.asarray(_level_table(), BF16)
    lmask = jnp.asarray(_level_masks(), F32)
    tri1 = np.tril(np.ones((CHUNK, CHUNK), np.float32))
    tri = jnp.asarray(np.concatenate([tri1, tri1], axis=1), BF16)

    hw = BRANCH_WIDTH
    c0 = 4 * hw
    c1 = c0 + 2 * N_HEADS * GLA_DK
    c2 = c1 + hw
    c3 = c2 + GLA_RANK
    c4 = c3 + hw
    c5 = c4 + 3 * hw
    c6 = c5 + hw
    c7 = c6 + N_HEADS
    c8 = c7 + N_HEADS

    x3 = x
    for l in range(depth):
        wl = w_in[l]
        ng = _row(norm_mix_g[l])
        lb = lb_all[l]
        lb_rows = jnp.stack([lb, jnp.log(lb), jnp.log1p(-lb)]).astype(F32)
        o_hg = _mixer_call(
            functools.partial(_gla_kernel, "hgrn2", 1), "hgrn2", x3,
            [ng, wl[:, :c0].astype(BF16), lb_rows, _row(jnp.tile(hg_norm_g[l], N_HEADS)), edge, level, lmask],
            _gla_scratch(batch, c0, hw))

        gate_cols = _pad_cols(jnp.concatenate([wl[:, c2:c3], wl[:, c6:c8]], axis=1), LANES)
        w_gla = jnp.concatenate([wl[:, c0:c2], wl[:, c3:c4], gate_cols], axis=1).astype(BF16)
        w2 = jnp.pad(gla_gk_w2[l], ((0, LANES - GLA_RANK), (0, 0))).astype(BF16)
        b2 = _row(gla_gk_b[l])
        o_gla, dn_gates = _mixer_call(
            functools.partial(_gla_kernel, "gla", 2), "gla", x3,
            [ng, w_gla, w2, b2, _row(jnp.tile(gla_norm_g[l], N_HEADS)), edge, level, lmask],
            _gla_scratch(batch, w_gla.shape[1], N_HEADS * GLA_DK), side_out=True)

        ab = jnp.zeros((2, LANES), F32)
        ab = ab.at[0, DN_LANE0 + N_HEADS:DN_LANE0 + 2 * N_HEADS].set(dn_A_log[l].astype(F32))
        ab = ab.at[1, DN_LANE0 + N_HEADS:DN_LANE0 + 2 * N_HEADS].set(dn_dt_bias[l].astype(F32))
        o_dn = _mixer_call(
            _dn_kernel, "deltanet", x3,
            [ng, wl[:, c4:c6].astype(BF16), dn_conv_w[l].astype(F32), ab, _row(jnp.tile(dn_norm_g[l], N_HEADS)), tri],
            _dn_scratch(batch), side_in=dn_gates)

        x2 = _merge_call(
            x3.reshape(batch * seq, d), batch, [ng, wl[:, c8:].astype(BF16), _row(gate_b[l])],
            [o.reshape(batch * seq, hw) for o in (o_hg, o_gla, o_dn)],
            [w_branch[l].astype(BF16), w_out[l].astype(BF16)])

        x2 = _ffn_call(
            x2, batch,
            [_row(norm_ffn_g[l]), w_up[l].astype(BF16), ffn_conv_w[l].astype(F32), _row(ffn_conv_b[l]),
             w_down[l].astype(BF16), _row(norm_final_g)],
            final_norm=(l == depth - 1))
        x3 = x2.reshape(batch, seq, d)
    return x3
```

```python
import functools

import jax
import jax.numpy as jnp
import numpy as np
from jax import lax
from jax.experimental import pallas as pl
from jax.experimental.pallas import tpu as pltpu

F32 = jnp.float32
BF16 = jnp.bfloat16

D_MODEL = 1024
CHUNK = 64
NORM_EPS = 1e-6
N_HEADS = 4
HEAD_DIM = 128
BRANCH_WIDTH = N_HEADS * HEAD_DIM
GLA_DK = 64
GLA_RANK = 16
GLA_GATE_NORMALIZER = 16.0
DN_LANE0 = GLA_RANK
DN_CONV = 4
N_BRANCH = 3
FFN_HIDDEN = 2816
FFN_CONV = 3
LANES = 128
CONV_PAD = 8
N_LEVELS = 6
LOG2E = 1.4426950408889634
SAFE_LOG2_DECAY = -160.0

TILE_MERGE = 1024
TILE_FFN = 1024
FFN_HALF = (FFN_HIDDEN // (2 * LANES) // 2 + 1) * 2 * LANES
VMEM_LIMIT = 56 * 1024 * 1024


def _sigmoid(x):
    return 1.0 / (1.0 + jnp.exp(-x))


def _silu(x):
    return x * _sigmoid(x)


def _softplus(x):
    return jnp.maximum(x, 0.0) + jnp.log1p(jnp.exp(-jnp.abs(x)))


def _log_sigmoid(x):
    return -_softplus(-x)


def _rms_norm(x, g):
    y = x * lax.rsqrt(jnp.mean(x * x, axis=-1, keepdims=True) + NORM_EPS)
    return y * g


def _dot(a, b):
    return jnp.dot(a, b, preferred_element_type=F32)


def _bdot(a, b):
    return lax.dot_general(a, b, (((2,), (1,)), ((0,), (0,))), preferred_element_type=F32)


def _bdot_nt(a, b):
    return lax.dot_general(a, b, (((2,), (2,)), ((0,), (0,))), preferred_element_type=F32)


def _bdot_tn(a, b):
    return lax.dot_general(a, b, (((1,), (1,)), ((0,), (0,))), preferred_element_type=F32)


def _split_bf16(x):
    hi = x.astype(BF16)
    lo = (x - hi.astype(F32)).astype(BF16)
    return hi, lo


def _head(h):
    return slice(h * HEAD_DIM, (h + 1) * HEAD_DIM)


def _heads_to_problems(x3):
    return jnp.concatenate([x3[:, :, _head(h)] for h in range(N_HEADS)], axis=0)


def _to_problems(x2d, nb):
    return _heads_to_problems(x2d.reshape(nb, x2d.shape[0] // nb, x2d.shape[1]))


def _key_problems(x, nb, query=False):
    x3 = x if x.ndim == 3 else x.reshape(nb, x.shape[0] // nb, x.shape[1])
    if x3.shape[-1] == BRANCH_WIDTH:
        return _heads_to_problems(x3)
    lane = lax.broadcasted_iota(jnp.int32, (1, 1, LANES), 2)
    parts = []
    for h in range(N_HEADS):
        xp = x3[:, :, (h // 2) * LANES:(h // 2 + 1) * LANES]
        if query:
            own = (lane >= LANES // 2) if h % 2 else (lane < LANES // 2)
            xp = jnp.where(own, xp, jnp.zeros_like(xp))
        parts.append(xp)
    return jnp.concatenate(parts, axis=0)


def _from_problems(xp, nb):
    rows = nb * xp.shape[1]
    return jnp.concatenate([xp[h * nb:(h + 1) * nb].reshape(rows, HEAD_DIM) for h in range(N_HEADS)], axis=-1)


def _run_interleaved(*stage_generators):
    live = list(stage_generators)
    while live:
        live = [s for s in live if next(s, True) is None]


def _causal_masks():
    row = lax.broadcasted_iota(jnp.int32, (CHUNK, CHUNK), 0)
    col = lax.broadcasted_iota(jnp.int32, (CHUNK, CHUNK), 1)
    return row, col, (row >= col)[None], (row > col)[None]


def _edge_table():
    tab = np.tril(np.ones((CHUNK, CHUNK), np.float32))
    return np.concatenate([tab, tab], axis=1)


def _level_table():
    t = np.arange(CHUNK)
    u = np.arange(CHUNK)
    rows = []
    for lvl in range(N_LEVELS):
        p = CHUNK >> (lvl + 1)
        mid = (t // (2 * p)) * (2 * p) + p - 1
        upper = t > mid
        rows.append(np.where(upper[:, None], (u[None, :] > mid[:, None]) & (u[None, :] <= t[:, None]),
                             (u[None, :] > t[:, None]) & (u[None, :] <= mid[:, None])))
    tab = np.concatenate(rows, axis=0).astype(np.float32)
    return np.concatenate([tab, tab], axis=1)


def _level_masks():
    t = np.arange(CHUNK)[:, None]
    s = np.arange(CHUNK)[None, :]
    out = []
    for lvl in range(N_LEVELS):
        p = CHUNK >> (lvl + 1)
        out.append((t // (2 * p) == s // (2 * p)) & (t % (2 * p) >= p) & (s % (2 * p) < p))
    out.append(t == s)
    return np.stack(out).astype(np.float32)


def _gated_head_norm(o, zg, g):
    outs = []
    for h in range(N_HEADS):
        oh = o[:, _head(h)]
        inv = lax.rsqrt(jnp.mean(oh * oh, axis=-1, keepdims=True) + NORM_EPS)
        outs.append(oh * inv)
    return jnp.concatenate(outs, axis=-1) * g * zg


def _zero(*refs):
    for r in refs:
        r[...] = jnp.zeros(r.shape, r.dtype)


def _gla_front(kind, x_ref, ng_ref, w_ref, aux_refs, edge_ref, slot, q_s, k_s, v_s, zg_s, gs_s, tab_s, safe_s, p_s):
    nb, _, d = x_ref.shape
    rows = nb * CHUNK
    w = BRANCH_WIDTH
    h = _rms_norm(x_ref[...].reshape(rows, d), ng_ref[...]).astype(BF16)
    yield

    def project(c0, c1):
        p_s[:, c0:c1] = _dot(h, w_ref[:, c0:c1])

    def store_decay(g):
        g_hi, g_lo = _split_bf16(g * LOG2E)
        for b in range(nb):
            r = slice(b * CHUNK, (b + 1) * CHUNK)
            gs_s[slot, b] = jnp.concatenate([g_hi[r], g_lo[r]], axis=0)

    if kind == "hgrn2":
        lb_ref, = aux_refs
        for c0 in range(0, 4 * w, w):
            project(c0, c0 + w)
            yield
        q_s[slot] = (_silu(p_s[:, 0:w]) * (HEAD_DIM ** -0.5)).astype(BF16)
        yield
        z = p_s[:, w:2 * w]
        lb = lb_ref[0:1, :]
        log_lb = lb_ref[1:2, :]
        c = lb_ref[2:3, :] + _log_sigmoid(z)
        store_decay(jnp.maximum(log_lb, c) + jnp.log1p(jnp.exp(-jnp.abs(log_lb - c))))
        k_s[slot] = ((1.0 - lb) * _sigmoid(-z)).astype(BF16)
        yield
        v_s[slot] = p_s[:, 2 * w:3 * w].astype(BF16)
        zg_s[slot] = _silu(p_s[:, 3 * w:4 * w])
        yield
    else:
        w2_ref, b2_ref, side_ref = aux_refs
        kw = q_s.shape[-1]
        low = 2 * kw + 2 * w
        project(low, low + LANES)
        side_ref[...] = p_s[:, low:low + LANES].reshape(side_ref.shape)
        yield
        project(0, 2 * kw)
        yield
        gk = _dot(p_s[:, low:low + LANES].astype(BF16), w2_ref[...]) + b2_ref[...]
        store_decay(_log_sigmoid(gk) * (1.0 / GLA_GATE_NORMALIZER))
        yield
        project(2 * kw, 2 * kw + w)
        yield
    edge = edge_ref[...]
    low_mark = None
    for b in range(nb):
        tab = _dot(edge, gs_s[slot, b])
        tab_s[slot, b] = tab
        total = tab[CHUNK - 1:CHUNK, :]
        low_mark = total if low_mark is None else jnp.minimum(low_mark, total)
        if b % 4 == 3:
            yield
    safe_s[slot] = (jnp.min(low_mark) >= SAFE_LOG2_DECAY).astype(jnp.int32)
    if kind != "hgrn2":
        project(2 * kw + w, 2 * kw + 2 * w)
        yield
        q_s[slot] = (p_s[:, 0:kw] * (GLA_DK ** -0.5)).astype(BF16)
        k_s[slot] = p_s[:, kw:2 * kw].astype(BF16)
        yield
        v_s[slot] = p_s[:, 2 * kw:2 * kw + w].astype(BF16)
        zg_s[slot] = _silu(p_s[:, 2 * kw + w:2 * kw + 2 * w])


def _gla_scores_centered(q, k, tab, nb):
    _, _, incl, _ = _causal_masks()
    pre = tab
    mid = 0.5 * pre[:, CHUNK - 1:CHUNK, :]
    rows = nb * CHUNK
    qf = q * jnp.exp2(pre - mid).astype(BF16).reshape(rows, -1)
    kf = k * jnp.exp2(mid - pre).astype(BF16).reshape(rows, -1)
    yield
    s = _bdot_nt(_key_problems(qf, nb, query=True), _key_problems(kf, nb))
    yield jnp.where(incl, s, 0.0)


def _gla_scores_levels(q, k, gs, level_ref, lmask_ref, nb):
    sc = _bdot_nt(_key_problems(q, nb, query=True), _key_problems(k, nb)) * lmask_ref[N_LEVELS][None]
    for l in range(N_LEVELS):
        wt = level_ref[l * CHUNK:(l + 1) * CHUNK, :]
        e = jnp.exp2(jnp.concatenate([_dot(wt, gs[b]) for b in range(nb)], axis=0)).astype(BF16)
        sc = sc + _bdot_nt(_key_problems(q * e, nb, query=True), _key_problems(k * e, nb)) * lmask_ref[l][None]
    return sc


def _gla_back(centered, slot, q_s, k_s, v_s, zg_s, gs_s, tab_s, st_ref, hn_ref, level_ref, lmask_ref, o_ref):
    nb = o_ref.shape[0]
    rows = nb * CHUNK
    q = q_s[slot]
    k = k_s[slot]
    tab = tab_s[slot]
    if centered:
        scores = _gla_scores_centered(q, k, tab, nb)
        yield next(scores)
        sc = next(scores)
    else:
        sc = _gla_scores_levels(q, k, [gs_s[slot, b] for b in range(nb)], level_ref, lmask_ref, nb)
    yield
    pre = tab
    e_in = jnp.exp2(pre).astype(BF16).reshape(rows, -1)
    e_out = jnp.exp2(pre[:, CHUNK - 1:CHUNK, :] - pre).astype(BF16).reshape(rows, -1)
    e_last = _key_problems(jnp.exp2(pre[:, CHUNK - 1:CHUNK, :]), nb)
    v_p = _to_problems(v_s[slot], nb)
    st = st_ref[...]
    o = _bdot(sc.astype(BF16), v_p) + _bdot(_key_problems(q * e_in, nb, query=True), st.astype(BF16))
    yield
    decay = jnp.swapaxes(jnp.broadcast_to(e_last, st.shape), 1, 2)
    st_ref[...] = st * decay + _bdot_tn(_key_problems(k * e_out, nb), v_p)
    yield
    y = _gated_head_norm(_from_problems(o, nb), zg_s[slot], hn_ref[...])
    o_ref[...] = y.reshape(o_ref.shape).astype(o_ref.dtype)


def _gla_kernel(kind, n_aux, x_ref, ng_ref, w_ref, *rest):
    aux_refs, rest = rest[:n_aux], rest[n_aux:]
    (hn_ref, edge_ref, level_ref, lmask_ref, o_ref), rest = rest[:5], rest[5:]
    if kind != "hgrn2":
        aux_refs, rest = aux_refs + rest[:1], rest[1:]
    st_ref, q_s, k_s, v_s, zg_s, gs_s, tab_s, safe_s, p_s = rest
    j = pl.program_id(0)
    slot = j % 2
    prev = 1 - slot
    bufs = (q_s, k_s, v_s, zg_s, gs_s, tab_s)

    @pl.when(j == 0)
    def _():
        _zero(st_ref, *bufs)
        safe_s[0] = 1
        safe_s[1] = 1

    def front():
        return _gla_front(kind, x_ref, ng_ref, w_ref, aux_refs, edge_ref, slot, *bufs, safe_s, p_s)

    def back(centered):
        return _gla_back(centered, prev, *bufs, st_ref, hn_ref, level_ref, lmask_ref, o_ref)

    @pl.when(safe_s[prev] == 1)
    def _():
        _run_interleaved(front(), back(True))

    @pl.when(safe_s[prev] != 1)
    def _():
        _run_interleaved(back(False))
        _run_interleaved(front())


def _dn_front(x_ref, side_ref, ng_ref, w_ref, cw_ref, ab_ref, slot, hist_s, ext_s, q_s, k_s, v_s, zg_s, bg_s):
    nb, _, d = x_ref.shape
    rows = nb * CHUNK
    w = BRANCH_WIDTH
    h = _rms_norm(x_ref[...].reshape(rows, d), ng_ref[...]).astype(BF16)
    yield
    for part, dst in enumerate((q_s, k_s, v_s)):
        cols = slice(part * w, (part + 1) * w)
        ext_s[:, 0:CONV_PAD, cols] = hist_s[:, :, cols]
        ext_s[:, CONV_PAD:CONV_PAD + CHUNK, cols] = _dot(h, w_ref[:, cols]).reshape(nb, CHUNK, w)
        hist_s[:, :, cols] = ext_s[:, CHUNK:CHUNK + CONV_PAD, cols]
        yield
        ext = ext_s[:, :, cols]
        conv = cw_ref[DN_CONV - 1:DN_CONV, cols][None] * ext[:, CONV_PAD:CONV_PAD + CHUNK]
        for back in range(1, DN_CONV):
            tap = DN_CONV - 1 - back
            shifted = pltpu.roll(ext, back, axis=1)
            conv = conv + cw_ref[tap:tap + 1, cols][None] * shifted[:, CONV_PAD:CONV_PAD + CHUNK]
        act = _silu(conv.reshape(rows, w))
        if part < 2:
            scale = HEAD_DIM ** -0.5 if part == 0 else 1.0
            heads = []
            for hd in range(N_HEADS):
                xh = act[:, _head(hd)]
                heads.append(xh * lax.rsqrt(jnp.sum(xh * xh, axis=-1, keepdims=True) + NORM_EPS) * scale)
            act = jnp.concatenate(heads, axis=-1)
        dst[slot] = _to_problems(act, nb)
        yield
    zg_s[slot] = _silu(_dot(h, w_ref[:, 3 * w:4 * w]))
    yield
    ba = side_ref[...].reshape(rows, LANES)
    bg_s[slot, 0] = _sigmoid(ba)
    bg_s[slot, 1] = -jnp.exp(ab_ref[0:1, :]) * _softplus(ba + ab_ref[1:2, :])


def _dn_back(slot, q_s, k_s, v_s, zg_s, bg_s, st_ref, hn_ref, tri_ref, o_ref):
    nb = o_ref.shape[0]
    q_p = q_s[slot]
    k_p = k_s[slot]
    v_p = v_s[slot]
    beta = bg_s[slot, 0]
    g_hi, g_lo = _split_bf16(bg_s[slot, 1])
    tri = tri_ref[...]
    gcols, grows = [], []
    for b in range(nb):
        r = slice(b * CHUNK, (b + 1) * CHUNK)
        gcum = _dot(tri, jnp.concatenate([g_hi[r], g_lo[r]], axis=0))
        gcols.append(gcum)
        grows.append(gcum.T)
    ln = lambda h: slice(DN_LANE0 + N_HEADS + h, DN_LANE0 + N_HEADS + h + 1)
    gc = jnp.concatenate([gcols[b][None, :, ln(h)] for h in range(N_HEADS) for b in range(nb)], axis=0)
    gr = jnp.concatenate([grows[b][None, ln(h), :] for h in range(N_HEADS) for b in range(nb)], axis=0)
    beta3 = beta.reshape(nb, CHUNK, LANES)
    bc = jnp.concatenate([beta3[:, :, DN_LANE0 + h:DN_LANE0 + h + 1] for h in range(N_HEADS)], axis=0)

    row, col, incl, strict = _causal_masks()
    decay = jnp.exp(jnp.where(incl, gc - gr, -jnp.inf))
    g_last = gc[:, CHUNK - 1:CHUNK, :]
    e_in = jnp.exp(gc)
    e_out = jnp.exp(g_last - gc)
    e_last = jnp.exp(g_last)

    kb = k_p * bc
    kq = _bdot_nt(jnp.concatenate([kb, q_p], axis=1).astype(BF16), k_p.astype(BF16))
    yield
    m = jnp.where(strict, kq[:, 0:CHUNK] * decay, 0.0)
    sc = jnp.where(incl, kq[:, CHUNK:2 * CHUNK] * decay, 0.0).astype(BF16)
    eye = (row == col).astype(F32)[None]
    t = None
    blk = 1
    while blk < CHUNK:
        sel = ((row // (2 * blk) == col // (2 * blk)) & (row % (2 * blk) >= blk) & (col % (2 * blk) < blk))[None]
        off = jnp.where(sel, m, 0.0)
        if blk == 1:
            t = eye - off
        else:
            t_bf = t.astype(BF16)
            half = _bdot(t_bf, off.astype(BF16)).astype(BF16)
            yield
            t = t - _bdot(half, t_bf)
            yield
        blk *= 2
    rhs = jnp.concatenate([v_p * bc, kb * e_in], axis=-1).astype(BF16)
    uw = _bdot(t.astype(BF16), rhs).astype(BF16)
    yield
    sc_uw = _bdot(sc, uw)
    k_uw = _bdot_tn((k_p * e_out).astype(BF16), uw)
    yield
    q_eff = (q_p * e_in - sc_uw[:, :, HEAD_DIM:]).astype(BF16)
    st = st_ref[...]
    st_bf = st.astype(BF16)
    o = _bdot(q_eff, st_bf) + sc_uw[:, :, 0:HEAD_DIM]
    st_ref[...] = st * e_last + k_uw[:, :, 0:HEAD_DIM] - _bdot(k_uw[:, :, HEAD_DIM:].astype(BF16), st_bf)
    yield
    y = _gated_head_norm(_from_problems(o, nb), zg_s[slot], hn_ref[...])
    o_ref[...] = y.reshape(o_ref.shape).astype(o_ref.dtype)


def _dn_kernel(x_ref, side_ref, ng_ref, w_ref, cw_ref, ab_ref, hn_ref, tri_ref, o_ref,
               st_ref, hist_s, ext_s, q_s, k_s, v_s, zg_s, bg_s):
    j = pl.program_id(0)
    slot = j % 2
    bufs = (q_s, k_s, v_s, zg_s, bg_s)

    @pl.when(j == 0)
    def _():
        _zero(st_ref, hist_s, *bufs)

    _run_interleaved(
        _dn_front(x_ref, side_ref, ng_ref, w_ref, cw_ref, ab_ref, slot, hist_s, ext_s, *bufs),
        _dn_back(1 - slot, *bufs, st_ref, hn_ref, tri_ref, o_ref))


def _merge_kernel(x_ref, ng_ref, wg_ref, gb_ref, oa_ref, ob_ref, oc_ref, wb_ref, wo_ref, y_ref):
    x = x_ref[...]
    h = _rms_norm(x, ng_ref[...]).astype(BF16)
    gates = _sigmoid(_dot(h, wg_ref[...]) + gb_ref[...])
    merged = None
    for n, o_n in enumerate((oa_ref, ob_ref, oc_ref)):
        term = gates[:, n * D_MODEL:(n + 1) * D_MODEL] * _dot(o_n[...], wb_ref[n])
        merged = term if merged is None else merged + term
    y_ref[...] = x + _dot(merged.astype(BF16), wo_ref[...])


def _ffn_kernel(x_ref, ng_ref, wu_ref, cw_ref, cb_ref, wd_ref, fg_ref, y_ref, ext_s, hist_s, act_s,
                *, final_norm):
    @pl.when(pl.program_id(1) == 0)
    def _():
        hist_s[...] = jnp.zeros(hist_s.shape, hist_s.dtype)

    tile = x_ref.shape[0]
    x = x_ref[...]
    h = _rms_norm(x, ng_ref[...]).astype(BF16)
    blk = 2 * LANES
    split = ext_s.shape[1] // 2
    y = x
    for h0, h1 in ((0, split), (split, FFN_HIDDEN)):
        for part, base in enumerate((h0, FFN_HIDDEN + h0)):
            dst = slice(part * split, part * split + h1 - h0)
            src = slice(base, base + h1 - h0)
            ext_s[0:CONV_PAD, dst] = hist_s[:, src]
            ext_s[CONV_PAD:CONV_PAD + tile, dst] = _dot(h, wu_ref[:, src])
            hist_s[:, src] = ext_s[tile:tile + CONV_PAD, dst]
        for j0 in range(0, h1 - h0, blk):
            halves = []
            for part, base in enumerate((h0, FFN_HIDDEN + h0)):
                ecols = slice(part * split + j0, part * split + j0 + blk)
                pcols = slice(base + j0, base + j0 + blk)
                u = cb_ref[0:1, pcols] + cw_ref[0:1, pcols] * ext_s[CONV_PAD - 2:CONV_PAD - 2 + tile, ecols]
                for j in range(1, FFN_CONV):
                    u = u + cw_ref[j:j + 1, pcols] * ext_s[CONV_PAD - 2 + j:CONV_PAD - 2 + j + tile, ecols]
                halves.append(u)
            act_s[:, h0 + j0:h0 + j0 + blk] = (_silu(halves[0]) * halves[1]).astype(BF16)
        y = y + _dot(act_s[:, h0:h1], wd_ref[h0:h1, :])
    if final_norm:
        y = _rms_norm(y, fg_ref[...])
    y_ref[...] = y


def _const_spec(shape, grid_rank):
    nd = len(shape)
    if grid_rank == 1:
        return pl.BlockSpec(shape, lambda i, _nd=nd: (0,) * _nd, pipeline_mode=pl.Buffered(1))
    return pl.BlockSpec(shape, lambda b, i, _nd=nd: (0,) * _nd, pipeline_mode=pl.Buffered(1))


def _row_spec(tile, width, steps):
    return pl.BlockSpec((tile, width), lambda b, i, _s=steps: (b * _s + i, 0))


def _mixer_call(body, name, x3, consts, scratch, side_in=None, side_out=False):
    batch, seq, _ = x3.shape
    n = seq // CHUNK
    front = lambda j: (0, jnp.minimum(j, n - 1), 0)
    back = lambda j: (0, jnp.maximum(j - 1, 0), 0)
    side_spec = pl.BlockSpec((batch, CHUNK, LANES), front)
    streamed = [x3] + ([side_in] if side_in is not None else [])
    in_specs = ([pl.BlockSpec((batch, CHUNK, D_MODEL), front)] + [side_spec] * (len(streamed) - 1)
                + [_const_spec(c.shape, 1) for c in consts])
    out_specs = [pl.BlockSpec((batch, CHUNK, BRANCH_WIDTH), back)]
    out_shape = [jax.ShapeDtypeStruct((batch, seq, BRANCH_WIDTH), BF16)]
    if side_out:
        out_specs.append(side_spec)
        out_shape.append(jax.ShapeDtypeStruct((batch, seq, LANES), F32))
    outs = pl.pallas_call(
        body,
        grid=(n + 1,),
        in_specs=in_specs,
        out_specs=out_specs,
        out_shape=out_shape,
        scratch_shapes=scratch,
        compiler_params=pltpu.CompilerParams(dimension_semantics=("arbitrary",), vmem_limit_bytes=VMEM_LIMIT),
        name=name,
    )(*streamed, *consts)
    return outs if side_out else outs[0]


def _gla_scratch(batch, proj_width, key_width):
    rows = batch * CHUNK
    w = BRANCH_WIDTH
    return [pltpu.VMEM((N_HEADS * batch, HEAD_DIM, HEAD_DIM), F32),
            pltpu.VMEM((2, rows, key_width), BF16), pltpu.VMEM((2, rows, key_width), BF16),
            pltpu.VMEM((2, rows, w), BF16),
            pltpu.VMEM((2, rows, w), F32),
            pltpu.VMEM((2, batch, 2 * CHUNK, key_width), BF16),
            pltpu.VMEM((2, batch, CHUNK, key_width), F32),
            pltpu.SMEM((2,), jnp.int32),
            pltpu.VMEM((rows, proj_width), F32)]


def _dn_scratch(batch):
    rows = batch * CHUNK
    prob = (2, N_HEADS * batch, CHUNK, HEAD_DIM)
    return [pltpu.VMEM((N_HEADS * batch, HEAD_DIM, HEAD_DIM), F32),
            pltpu.VMEM((batch, CONV_PAD, 3 * BRANCH_WIDTH), F32),
            pltpu.VMEM((batch, CHUNK + CONV_PAD, 3 * BRANCH_WIDTH), F32),
            pltpu.VMEM(prob, F32), pltpu.VMEM(prob, F32), pltpu.VMEM(prob, F32),
            pltpu.VMEM((2, rows, BRANCH_WIDTH), F32),
            pltpu.VMEM((2, 2, rows, LANES), F32)]


def _params2():
    return pltpu.CompilerParams(dimension_semantics=("arbitrary", "arbitrary"), vmem_limit_bytes=VMEM_LIMIT)


def _merge_call(x2, batch, consts_a, branches, consts_b):
    rows = x2.shape[0]
    steps = rows // batch // TILE_MERGE
    in_specs = ([_row_spec(TILE_MERGE, D_MODEL, steps)] + [_const_spec(c.shape, 2) for c in consts_a]
                + [_row_spec(TILE_MERGE, BRANCH_WIDTH, steps) for _ in branches]
                + [_const_spec(c.shape, 2) for c in consts_b])
    return pl.pallas_call(
        _merge_kernel,
        grid=(batch, steps),
        in_specs=in_specs,
        out_specs=_row_spec(TILE_MERGE, D_MODEL, steps),
        out_shape=jax.ShapeDtypeStruct((rows, D_MODEL), F32),
        compiler_params=_params2(),
        name="merge",
    )(x2, *consts_a, *branches, *consts_b)


def _ffn_call(x2, batch, consts, final_norm):
    rows = x2.shape[0]
    steps = rows // batch // TILE_FFN
    in_specs = [_row_spec(TILE_FFN, D_MODEL, steps)] + [_const_spec(c.shape, 2) for c in consts]
    return pl.pallas_call(
        functools.partial(_ffn_kernel, final_norm=final_norm),
        grid=(batch, steps),
        in_specs=in_specs,
        out_specs=_row_spec(TILE_FFN, D_MODEL, steps),
        out_shape=jax.ShapeDtypeStruct((rows, D_MODEL), F32),
        scratch_shapes=[pltpu.VMEM((TILE_FFN + CONV_PAD, 2 * FFN_HALF), F32),
                        pltpu.VMEM((CONV_PAD, 2 * FFN_HIDDEN), F32),
                        pltpu.VMEM((TILE_FFN, FFN_HIDDEN), BF16)],
        compiler_params=_params2(),
        name="ffn_final" if final_norm else "ffn",
    )(x2, *consts)


def _pad_cols(wcols, width):
    return jnp.pad(wcols, [(0, 0)] * (wcols.ndim - 1) + [(0, width - wcols.shape[-1])])


def _row(v):
    return v.reshape(1, -1).astype(F32)


def kernel(x, norm_mix_g, w_in, hg_lower_bounds, hg_norm_g, gla_gk_w2, gla_gk_b, gla_norm_g, dn_conv_w,
           dn_A_log, dn_dt_bias, dn_norm_g, w_branch, gate_b, w_out, norm_ffn_g, w_up, ffn_conv_w,
           ffn_conv_b, w_down, norm_final_g):
    batch, seq, d = x.shape
    depth = w_in.shape[0]
    assert d == D_MODEL and seq % max(CHUNK, TILE_MERGE, TILE_FFN) == 0

    lb_all = jnp.cumsum(jax.nn.softmax(hg_lower_bounds.astype(F32), axis=0), axis=0)
    lb_all = lb_all - lb_all[:1]
    edge = jnp.asarray(_edge_table(), BF16)
    level = jnp.asarray(_level_table(), BF16)
    lmask = jnp.asarray(_level_masks(), F32)
    tri1 = np.tril(np.ones((CHUNK, CHUNK), np.float32))
    tri = jnp.asarray(np.concatenate([tri1, tri1], axis=1), BF16)

    hw = BRANCH_WIDTH
    c0 = 4 * hw
    c1 = c0 + 2 * N_HEADS * GLA_DK
    c2 = c1 + hw
    c3 = c2 + GLA_RANK
    c4 = c3 + hw
    c5 = c4 + 3 * hw
    c6 = c5 + hw
    c7 = c6 + N_HEADS
    c8 = c7 + N_HEADS

    x3 = x
    for l in range(depth):
        wl = w_in[l]
        ng = _row(norm_mix_g[l])
        lb = lb_all[l]
        lb_rows = jnp.stack([lb, jnp.log(lb), jnp.log1p(-lb)]).astype(F32)
        o_hg = _mixer_call(
            functools.partial(_gla_kernel, "hgrn2", 1), "hgrn2", x3,
            [ng, wl[:, :c0].astype(BF16), lb_rows, _row(jnp.tile(hg_norm_g[l], N_HEADS)), edge, level, lmask],
            _gla_scratch(batch, c0, hw))

        gate_cols = _pad_cols(jnp.concatenate([wl[:, c2:c3], wl[:, c6:c8]], axis=1), LANES)
        w_gla = jnp.concatenate([wl[:, c0:c2], wl[:, c3:c4], gate_cols], axis=1).astype(BF16)
        w2 = jnp.pad(gla_gk_w2[l], ((0, LANES - GLA_RANK), (0, 0))).astype(BF16)
        b2 = _row(gla_gk_b[l])
        o_gla, dn_gates = _mixer_call(
            functools.partial(_gla_kernel, "gla", 2), "gla", x3,
            [ng, w_gla, w2, b2, _row(jnp.tile(gla_norm_g[l], N_HEADS)), edge, level, lmask],
            _gla_scratch(batch, w_gla.shape[1], N_HEADS * GLA_DK), side_out=True)

        ab = jnp.zeros((2, LANES), F32)
        ab = ab.at[0, DN_LANE0 + N_HEADS:DN_LANE0 + 2 * N_HEADS].set(dn_A_log[l].astype(F32))
        ab = ab.at[1, DN_LANE0 + N_HEADS:DN_LANE0 + 2 * N_HEADS].set(dn_dt_bias[l].astype(F32))
        o_dn = _mixer_call(
            _dn_kernel, "deltanet", x3,
            [ng, wl[:, c4:c6].astype(BF16), dn_conv_w[l].astype(F32), ab, _row(jnp.tile(dn_norm_g[l], N_HEADS)), tri],
            _dn_scratch(batch), side_in=dn_gates)

        x2 = _merge_call(
            x3.reshape(batch * seq, d), batch, [ng, wl[:, c8:].astype(BF16), _row(gate_b[l])],
            [o.reshape(batch * seq, hw) for o in (o_hg, o_gla, o_dn)],
            [w_branch[l].astype(BF16), w_out[l].astype(BF16)])

        x2 = _ffn_call(
            x2, batch,
            [_row(norm_ffn_g[l]), w_up[l].astype(BF16), ffn_conv_w[l].astype(F32), _row(ffn_conv_b[l]),
             w_down[l].astype(BF16), _row(norm_final_g)],
            final_norm=(l == depth - 1))
        x3 = x2.reshape(batch, seq, d)
    return x3
```
